```python
import math
import jax
import jax.numpy as jnp
from jax import lax
import numpy as np

D_MODEL = 2048
BATCH = 4
SEQ = 2048
DEPTH = 2
DEC_BATCH = 128
DEC_SEQ = 8
PAST_LEN = 16384
PAGE_SIZE = 128

N_AB = (DEPTH + 1) // 2
N_CD = DEPTH // 2
H_A = 4
DK_A = 128
DV_A = 256
ROPE_BASE = 10000.0
H_B = 8
DK_B = 128
DV_B = 128
H_C = 4
DK_C = 256
DV_C = 256
CONV_W = 4
H_D = 4
DK_D = 128
DV_D = 256
GLA_RANK = 16
GLA_TAU = 16.0
FF_DIM = 5632
CHUNK = 64
EPS = 1e-6

AB_SIZES = [H_A * DK_A, H_A * DK_A, H_A * DV_A, H_A * DV_A,
            H_B * DK_B, H_B * DK_B, H_B * DV_B, H_B * DV_B]
CD_SIZES = [H_C * DK_C, H_C * DK_C, H_C * DV_C, H_C * DV_C, H_C, H_C,
            H_D * DK_D, H_D * DK_D, H_D * DV_D, H_D * DV_D, GLA_RANK]
AB_IN = sum(AB_SIZES)
CD_IN = sum(CD_SIZES)
AB_MIX = H_A * DV_A + H_B * DV_B
CD_MIX = H_C * DV_C + H_D * DV_D

kernel_name = 'hybrid_ret_hgrn2_mlstm_gla_step'


def _rmsnorm(x, g):
    xf = x.astype(jnp.float32)
    y = xf * lax.rsqrt(jnp.mean(xf * xf, axis=-1, keepdims=True) + EPS)
    return (y * g.astype(jnp.float32)).astype(x.dtype)


def _head_norm(o, g):
    b, t, h, v = o.shape
    o = o * lax.rsqrt(jnp.mean(o * o, axis=-1, keepdims=True) + EPS)
    return o.reshape(b, t, h * v) * g.astype(jnp.float32)


def _swiglu(x, w1, w3, w2):
    return (jax.nn.silu(x @ w1) * (x @ w3)) @ w2


def _split(z, sizes):
    return jnp.split(z, [int(i) for i in np.cumsum(sizes)[:-1]], axis=-1)


def _rope(x, pos):
    half = x.shape[-1] // 2
    inv = ROPE_BASE ** (-jnp.arange(half, dtype=jnp.float32) / half)
    ang = pos[:, None] * inv[None, :]
    cos = jnp.cos(ang)[None, :, None, :]
    sin = jnp.sin(ang)[None, :, None, :]
    x1, x2 = x[..., :half], x[..., half:]
    return jnp.concatenate([x1 * cos - x2 * sin, x1 * sin + x2 * cos], axis=-1)


def _chunk_len(t):
    return math.gcd(t, CHUNK)


def _to_chunks(a, n):
    b, t = a.shape[:2]
    return jnp.moveaxis(a.reshape((b, t // n, n) + a.shape[2:]), 1, 0)


def _from_chunks(a):
    nc, b, n = a.shape[:3]
    return jnp.moveaxis(a, 0, 1).reshape((b, nc * n) + a.shape[3:])


def _scan_scalar_decay(q, k, v, g, s0):
    n = _chunk_len(q.shape[1])
    mask = jnp.tril(jnp.ones((n, n), dtype=bool))

    def step(s, inp):
        qc, kc, vc, gc = inp
        b = jnp.cumsum(gc, axis=1)
        o_inter = jnp.einsum('bthk,bhkv->bthv', qc * jnp.exp(b)[..., None], s)
        dec = jnp.exp(jnp.where(mask[None, :, :, None], b[:, :, None] - b[:, None, :], -jnp.inf))
        a = jnp.einsum('bthk,bshk->btsh', qc, kc) * dec
        o = o_inter + jnp.einsum('btsh,bshv->bthv', a, vc)
        bl = b[:, -1]
        kw = kc * jnp.exp(bl[:, None] - b)[..., None]
        s = jnp.exp(bl)[..., None, None] * s + jnp.einsum('bshk,bshv->bhkv', kw, vc)
        return s, o

    s, o = lax.scan(step, s0, (_to_chunks(q, n), _to_chunks(k, n), _to_chunks(v, n), _to_chunks(g, n)))
    return _from_chunks(o), s


def _scan_vector_decay(q, k, v, g, s0):
    n = _chunk_len(q.shape[1])
    mask = jnp.tril(jnp.ones((n, n), dtype=bool))

    def step(s, inp):
        qc, kc, vc, gc = inp
        b = jnp.cumsum(gc, axis=1)
        o_inter = jnp.einsum('bthk,bhkv->bthv', qc * jnp.exp(b), s)
        dec = jnp.exp(jnp.where(mask[None, :, :, None, None], b[:, :, None] - b[:, None, :], -jnp.inf))
        a = jnp.einsum('bthk,bshk,btshk->btsh', qc, kc, dec)
        o = o_inter + jnp.einsum('btsh,bshv->bthv', a, vc)
        bl = b[:, -1]
        kw = kc * jnp.exp(bl[:, None] - b)
        s = jnp.exp(bl)[..., None] * s + jnp.einsum('bshk,bshv->bhkv', kw, vc)
        return s, o

    s, o = lax.scan(step, s0, (_to_chunks(q, n), _to_chunks(k, n), _to_chunks(v, n), _to_chunks(g, n)))
    return _from_chunks(o), s


def _mlstm_scan(q, k, v, log_i, log_f, c0, n0, m0):
    n = _chunk_len(q.shape[1])
    mask = jnp.tril(jnp.ones((n, n), dtype=bool))

    def step(carry, inp):
        c, nv, m = carry
        qc, kc, vc, ic, fc = inp
        b = jnp.cumsum(fc, axis=1)
        d = jnp.where(mask[None, :, :, None], b[:, :, None] - b[:, None, :] + ic[:, None, :], -jnp.inf)
        m_inter = b + m[:, None]
        m_t = jnp.maximum(m_inter, jnp.max(d, axis=2))
        w_inter = jnp.exp(m_inter - m_t)
        sc = jnp.einsum('bthk,bshk->btsh', qc, kc) * jnp.exp(d - m_t[:, :, None])
        num = w_inter[..., None] * jnp.einsum('bthk,bhkv->bthv', qc, c) + jnp.einsum('btsh,bshv->bthv', sc, vc)
        den = w_inter * jnp.einsum('bthk,bhk->bth', qc, nv) + jnp.sum(sc, axis=2)
        h = num / jnp.maximum(jnp.abs(den), jnp.exp(-m_t))[..., None]
        m_new = m_t[:, -1]
        w_c = jnp.exp(b[:, -1] + m - m_new)
        w_s = jnp.exp(b[:, -1:] - b + ic - m_new[:, None])
        kw = kc * w_s[..., None]
        c = w_c[..., None, None] * c + jnp.einsum('bshk,bshv->bhkv', kw, vc)
        nv = w_c[..., None] * nv + jnp.sum(kw, axis=1)
        return (c, nv, m_new), h

    (c, nv, m), h = lax.scan(step, (c0, n0, m0), (_to_chunks(q, n), _to_chunks(k, n), _to_chunks(v, n),
                                                  _to_chunks(log_i, n), _to_chunks(log_f, n)))
    return _from_chunks(h), c, nv, m


def _ab_mixer(h, s_ret, s_hgrn, pos, lb, w_in, w_out, g_a, g_b):
    f32 = jnp.float32
    bsz, t, _ = h.shape
    qa, ka, va, ga, qb, fb, ib, gb = _split(h @ w_in, AB_SIZES)
    qa = _rope(qa.astype(f32).reshape(bsz, t, H_A, DK_A), pos)
    ka = _rope(ka.astype(f32).reshape(bsz, t, H_A, DK_A), pos) * (DK_A ** -0.5)
    va = va.astype(f32).reshape(bsz, t, H_A, DV_A)
    log_gamma = jnp.log1p(-jnp.exp2(-5.0 - jnp.arange(H_A, dtype=f32)))
    oa, s_ret_new = _scan_scalar_decay(qa, ka, va, jnp.broadcast_to(log_gamma, (bsz, t, H_A)), s_ret.astype(f32))
    ya = _head_norm(oa, g_a) * jax.nn.silu(ga.astype(f32))
    fb = fb.astype(f32)
    f_gate = lb + (1.0 - lb) * jax.nn.sigmoid(fb)
    kb = ((1.0 - lb) * jax.nn.sigmoid(-fb)).reshape(bsz, t, H_B, DK_B)
    log_f = jnp.log(f_gate).reshape(bsz, t, H_B, DK_B)
    qb = jax.nn.silu(qb.astype(f32)).reshape(bsz, t, H_B, DK_B)
    vb = ib.astype(f32).reshape(bsz, t, H_B, DV_B)
    ob, s_hgrn_new = _scan_vector_decay(qb, kb, vb, log_f, s_hgrn.astype(f32))
    yb = _head_norm(ob, g_b) * jax.nn.silu(gb.astype(f32))
    y = jnp.concatenate([ya, yb], axis=-1).astype(h.dtype) @ w_out
    return y, s_ret_new.astype(s_ret.dtype), s_hgrn_new.astype(s_hgrn.dtype)


def _cd_mixer(h, c0, n0, m0, conv0, s_gla, w_in, w_out, conv_w, i_bias, f_bias, g_c, a_w2, a_b, g_d):
    f32 = jnp.float32
    bsz, t, _ = h.shape
    qc, kc, vc, oc, ic, fc, qd, kd, vd, gd, ad = _split(h @ w_in, CD_SIZES)
    u = jnp.concatenate([qc, kc], axis=-1)
    ext = jnp.concatenate([conv0.astype(u.dtype), u], axis=1)
    conv = ext[:, 0:t] * conv_w[0]
    for j in range(1, CONV_W):
        conv = conv + ext[:, j:j + t] * conv_w[j]
    conv_new = ext[:, t:]
    q_c, k_c = jnp.split(jax.nn.silu(conv.astype(f32)), 2, axis=-1)
    q_c = q_c.reshape(bsz, t, H_C, DK_C)
    k_c = k_c.reshape(bsz, t, H_C, DK_C) * (DK_C ** -0.5)
    v_c = vc.astype(f32).reshape(bsz, t, H_C, DV_C)
    log_i = ic.astype(f32) + i_bias.astype(f32)
    log_fc = jax.nn.log_sigmoid(fc.astype(f32) + f_bias.astype(f32))
    hc, c_new, n_new, m_new = _mlstm_scan(q_c, k_c, v_c, log_i, log_fc,
                                          c0.astype(f32), n0.astype(f32), m0.astype(f32))
    yc = jax.nn.sigmoid(oc.astype(f32)) * _head_norm(hc, g_c)
    q_d = qd.astype(f32).reshape(bsz, t, H_D, DK_D) * (DK_D ** -0.5)
    k_d = kd.astype(f32).reshape(bsz, t, H_D, DK_D)
    v_d = vd.astype(f32).reshape(bsz, t, H_D, DV_D)
    log_a = (jax.nn.log_sigmoid(ad.astype(f32) @ a_w2.astype(f32) + a_b.astype(f32)) / GLA_TAU).reshape(bsz, t, H_D, DK_D)
    od, s_gla_new = _scan_vector_decay(q_d, k_d, v_d, log_a, s_gla.astype(f32))
    yd = _head_norm(od, g_d) * jax.nn.silu(gd.astype(f32))
    y = jnp.concatenate([yc, yd], axis=-1).astype(h.dtype) @ w_out
    return (y, c_new.astype(c0.dtype), n_new.astype(n0.dtype), m_new.astype(m0.dtype),
            conv_new.astype(conv0.dtype), s_gla_new.astype(s_gla.dtype))


def _trunk(x, pos0, s_ret, s_hgrn, s_c, s_n, s_m, s_conv, s_gla, w):
    f32 = jnp.float32
    pos = jnp.arange(x.shape[1], dtype=f32) + float(pos0)
    lb_all = jnp.cumsum(jax.nn.softmax(w['lb_logits'].astype(f32), axis=0), axis=0)
    ret, hg, cs, ns, ms, cvs, gls = [], [], [], [], [], [], []
    for l in range(DEPTH):
        j = l // 2
        x = x + 0.5 * _swiglu(_rmsnorm(x, w['norm_gain'][l, 0]), w['ffn_w1'][l, 0], w['ffn_w3'][l, 0], w['ffn_w2'][l, 0])
        h = _rmsnorm(x, w['norm_gain'][l, 1])
        if l % 2 == 0:
            y, r, g = _ab_mixer(h, s_ret[j], s_hgrn[j], pos, lb_all[l], w['ab_w_in'][j], w['ab_w_out'][j],
                                w['ab_norm_a'][j], w['ab_norm_b'][j])
            ret.append(r)
            hg.append(g)
        else:
            y, c, n, m, cv, gl = _cd_mixer(h, s_c[j], s_n[j], s_m[j], s_conv[j], s_gla[j], w['cd_w_in'][j],
                                           w['cd_w_out'][j], w['cd_conv_w'][j], w['cd_i_bias'][j], w['cd_f_bias'][j],
                                           w['cd_norm_c'][j], w['cd_alpha_w2'][j], w['cd_alpha_b'][j], w['cd_norm_d'][j])
            cs.append(c)
            ns.append(n)
            ms.append(m)
            cvs.append(cv)
            gls.append(gl)
        x = x + y
        x = x + 0.5 * _swiglu(_rmsnorm(x, w['norm_gain'][l, 2]), w['ffn_w1'][l, 1], w['ffn_w3'][l, 1], w['ffn_w2'][l, 1])
    y = _rmsnorm(x, w['final_norm'])
    return (y, jnp.stack(ret), jnp.stack(hg), jnp.stack(cs), jnp.stack(ns), jnp.stack(ms),
            jnp.stack(cvs), jnp.stack(gls))


def setup_inputs(seed: int = 0) -> dict:
    key = jax.random.key(seed)
    keys = iter(jax.random.split(key, 40))

    def nrm(shape, scale):
        return scale * jax.random.normal(next(keys), shape, jnp.float32)

    return {
        'x_prompt': nrm((BATCH, SEQ, D_MODEL), 1.0),
        'x_sample': nrm((DEC_BATCH, DEC_SEQ, D_MODEL), 1.0),
        'state_ret': nrm((N_AB, DEC_BATCH, H_A, DK_A, DV_A), 0.5),
        'state_hgrn': nrm((N_AB, DEC_BATCH, H_B, DK_B, DV_B), 0.5),
        'state_mlstm_c': nrm((N_CD, DEC_BATCH, H_C, DK_C, DV_C), 0.5),
        'state_mlstm_n': nrm((N_CD, DEC_BATCH, H_C, DK_C), 0.5),
        'state_mlstm_m': nrm((N_CD, DEC_BATCH, H_C), 1.0),
        'state_mlstm_conv': nrm((N_CD, DEC_BATCH, CONV_W - 1, 2 * H_C * DK_C), 1.0),
        'state_gla': nrm((N_CD, DEC_BATCH, H_D, DK_D, DV_D), 0.5),
        'norm_gain': 1.0 + nrm((DEPTH, 3, D_MODEL), 0.02),
        'ffn_w1': nrm((DEPTH, 2, D_MODEL, FF_DIM), D_MODEL ** -0.5),
        'ffn_w3': nrm((DEPTH, 2, D_MODEL, FF_DIM), D_MODEL ** -0.5),
        'ffn_w2': nrm((DEPTH, 2, FF_DIM, D_MODEL), FF_DIM ** -0.5),
        'ab_w_in': nrm((N_AB, D_MODEL, AB_IN), D_MODEL ** -0.5),
        'ab_w_out': nrm((N_AB, AB_MIX, D_MODEL), AB_MIX ** -0.5),
        'ab_norm_a': 1.0 + nrm((N_AB, H_A * DV_A), 0.02),
        'ab_norm_b': 1.0 + nrm((N_AB, H_B * DV_B), 0.02),
        'lb_logits': nrm((DEPTH + 1, H_B * DK_B), 0.5),
        'cd_w_in': nrm((N_CD, D_MODEL, CD_IN), D_MODEL ** -0.5),
        'cd_w_out': nrm((N_CD, CD_MIX, D_MODEL), CD_MIX ** -0.5),
        'cd_conv_w': nrm((N_CD, CONV_W, 2 * H_C * DK_C), CONV_W ** -0.5),
        'cd_i_bias': nrm((N_CD, H_C), 0.1),
        'cd_f_bias': jnp.linspace(3.0, 6.0, H_C, dtype=jnp.float32)[None, :] + nrm((N_CD, H_C), 0.1),
        'cd_norm_c': 1.0 + nrm((N_CD, H_C * DV_C), 0.02),
        'cd_alpha_w2': nrm((N_CD, GLA_RANK, H_D * DK_D), GLA_RANK ** -0.5),
        'cd_alpha_b': nrm((N_CD, H_D * DK_D), 0.1),
        'cd_norm_d': 1.0 + nrm((N_CD, H_D * DV_D), 0.02),
        'final_norm': 1.0 + nrm((D_MODEL,), 0.02),
    }


def reference(x_prompt, x_sample, state_ret, state_hgrn, state_mlstm_c, state_mlstm_n, state_mlstm_m,
              state_mlstm_conv, state_gla, norm_gain, ffn_w1, ffn_w3, ffn_w2, ab_w_in, ab_w_out, ab_norm_a,
              ab_norm_b, lb_logits, cd_w_in, cd_w_out, cd_conv_w, cd_i_bias, cd_f_bias, cd_norm_c, cd_alpha_w2,
              cd_alpha_b, cd_norm_d, final_norm):
    w = dict(norm_gain=norm_gain, ffn_w1=ffn_w1, ffn_w3=ffn_w3, ffn_w2=ffn_w2, ab_w_in=ab_w_in,
             ab_w_out=ab_w_out, ab_norm_a=ab_norm_a, ab_norm_b=ab_norm_b, lb_logits=lb_logits,
             cd_w_in=cd_w_in, cd_w_out=cd_w_out, cd_conv_w=cd_conv_w, cd_i_bias=cd_i_bias,
             cd_f_bias=cd_f_bias, cd_norm_c=cd_norm_c, cd_alpha_w2=cd_alpha_w2, cd_alpha_b=cd_alpha_b,
             cd_norm_d=cd_norm_d, final_norm=final_norm)
    bp = x_prompt.shape[0]
    dt = x_prompt.dtype
    z_ret = jnp.zeros((N_AB, bp, H_A, DK_A, DV_A), dt)
    z_hgrn = jnp.zeros((N_AB, bp, H_B, DK_B, DV_B), dt)
    z_c = jnp.zeros((N_CD, bp, H_C, DK_C, DV_C), dt)
    z_n = jnp.zeros((N_CD, bp, H_C, DK_C), dt)
    z_m = jnp.zeros((N_CD, bp, H_C), dt)
    z_conv = jnp.zeros((N_CD, bp, CONV_W - 1, 2 * H_C * DK_C), dt)
    z_gla = jnp.zeros((N_CD, bp, H_D, DK_D, DV_D), dt)
    y_p, ret_p, hgrn_p, c_p, n_p, m_p, conv_p, gla_p = _trunk(
        x_prompt, 0, z_ret, z_hgrn, z_c, z_n, z_m, z_conv, z_gla, w)
    y_s, ret_s, hgrn_s, c_s, n_s, m_s, conv_s, gla_s = _trunk(
        x_sample, PAST_LEN, state_ret, state_hgrn, state_mlstm_c, state_mlstm_n, state_mlstm_m,
        state_mlstm_conv, state_gla, w)
    return (y_p, y_s, ret_p, ret_s, hgrn_p, hgrn_s, c_p, c_s, n_p, n_s, m_p, m_s, conv_p, conv_s, gla_p, gla_s)
```

```python
import functools
import math

import numpy as np
import jax
import jax.numpy as jnp
from jax import lax
from jax.experimental import pallas as pl
from jax.experimental.pallas import tpu as pltpu

F32 = jnp.float32
BF16 = jnp.bfloat16

D_MODEL = 2048
DEPTH = 2
PAST_LEN = 16384
H_A, DK_A, DV_A = 4, 128, 256
ROPE_BASE = 10000.0
H_B, DK_B, DV_B = 8, 128, 128
H_C, DK_C, DV_C = 4, 256, 256
CONV_W = 4
H_D, DK_D, DV_D = 4, 128, 256
GLA_RANK = 16
GLA_TAU = 16.0
FF_DIM = 5632
EPS = 1e-6

AB_IN = 2 * H_A * DK_A + 2 * H_A * DV_A + 2 * H_B * DK_B + 2 * H_B * DV_B
CD_MAIN = 2 * H_C * DK_C + 2 * H_C * DV_C + 2 * H_D * DK_D + 2 * H_D * DV_D
CD_TAIL = 128
AB_MIX = H_A * DV_A + H_B * DV_B
CD_MIX = H_C * DV_C + H_D * DV_D

LANES = 128
SUBLANES = 8
VMEM_LIMIT = 56 * 1024 * 1024

TM = 512
TF = 512
TN = 512
CHUNK = 64
SLAB = 256
SEQ_PER_STEP = 16


def _cparams(sem):
    return pltpu.CompilerParams(dimension_semantics=sem, vmem_limit_bytes=VMEM_LIMIT)


def _rms(x, g):
    return x * lax.rsqrt(jnp.mean(x * x, axis=-1, keepdims=True) + EPS) * g


def _silu(x):
    return x * jax.nn.sigmoid(x)


def _log_sigmoid(x):
    return jnp.minimum(x, 0.0) - jnp.log1p(jnp.exp(-jnp.abs(x)))


def _split3(x):
    hi = x.astype(BF16)
    r1 = x - hi.astype(F32)
    mid = r1.astype(BF16)
    lo = (r1 - mid.astype(F32)).astype(BF16)
    return jnp.concatenate([hi, mid, lo], axis=1)


def _sum3(p, n):
    return p[:, :n] + p[:, n:2 * n] + p[:, 2 * n:3 * n]


def _dot(a, b):
    return jnp.dot(a, b, preferred_element_type=F32)


def _dot_nt(a, b):
    return lax.dot_general(a, b, (((1,), (1,)), ((), ())), preferred_element_type=F32)


def _dot_tn(a, b):
    return lax.dot_general(a, b, (((0,), (0,)), ((), ())), preferred_element_type=F32)


def _prefix8(g):
    row = lax.broadcasted_iota(jnp.int32, g.shape, 0)
    b = jnp.zeros_like(g)
    rev = jnp.zeros_like(g)
    for u in range(SUBLANES):
        gu = jnp.broadcast_to(g[u:u + 1, :], g.shape)
        b = b + jnp.where(row >= u, gu, 0.0)
        rev = rev + jnp.where(row < u, gu, 0.0)
    return b, rev


def _row_bcast(col, c):
    lane = lax.broadcasted_iota(jnp.int32, (c, LANES), 1)
    rm = _split3(jnp.where(lane == 0, jnp.broadcast_to(col, (c, LANES)), 0.0))
    ones = jnp.ones((c, 3 * LANES), BF16)
    return _dot_nt(ones, rm)


def _decay_consts(c):
    levels = []
    m = SUBLANES
    while m < c:
        levels.append(m)
        m *= 2
    t = np.arange(c)[:, None]
    u = np.arange(c)[None, :]
    mats = [u <= t, u > t]
    masks = []
    for m in levels:
        blk = t // (2 * m)
        up = (t % (2 * m)) >= m
        e = blk * 2 * m + m - 1
        mats.append(up & (u > e) & (u <= t))
        mats.append((~up) & (u > t) & (u <= e))
        masks.append(up & ((u % (2 * m)) < m) & (blk == u // (2 * m)))
    mc = jnp.asarray(np.concatenate(mats, 0).astype(np.float32), BF16)
    mk = jnp.asarray(np.stack(masks).astype(np.float32)) if levels else jnp.zeros((1, c, c), F32)
    return tuple(levels), mc, mk


def _diag_blocks(q, k, v, b, c):
    row = lax.broadcasted_iota(jnp.int32, (SUBLANES, 1), 0)
    outs = []
    for n in range(c // SUBLANES):
        r0 = n * SUBLANES
        qb = q[r0:r0 + SUBLANES]
        bb = b[r0:r0 + SUBLANES]
        ob = jnp.zeros((SUBLANES, v.shape[1]), F32)
        for j in range(SUBLANES):
            r = r0 + j
            e = jnp.exp(jnp.minimum(bb - b[r:r + 1], 0.0))
            a = jnp.sum(qb * k[r:r + 1] * e, axis=-1, keepdims=True)
            ob = ob + jnp.where(row >= j, a, 0.0) * v[r:r + 1]
        outs.append(ob)
    return outs[0] if len(outs) == 1 else jnp.concatenate(outs, axis=0)


def _decay_core(q, k, v, g, s, mc_ref, mk_ref, levels):
    c, kd = q.shape
    vd = v.shape[1]
    if c == SUBLANES:
        b, rev = _prefix8(g)
        pre = None
    else:
        pre = _sum3(_dot(mc_ref[...], _split3(g)), kd)
        b, rev = pre[0:c], pre[c:2 * c]
    o = _dot((q * jnp.exp(b)).astype(BF16), s.astype(BF16))
    if levels:
        a = jnp.zeros((c, c), F32)
        for li in range(len(levels)):
            dq = pre[(2 + 2 * li) * c:(3 + 2 * li) * c]
            dk = pre[(3 + 2 * li) * c:(4 + 2 * li) * c]
            ql = (q * jnp.exp(dq)).astype(BF16)
            kl = (k * jnp.exp(dk)).astype(BF16)
            a = a + mk_ref[li] * _dot_nt(ql, kl)
        o = o + _dot(a.astype(BF16), v.astype(BF16))
    o = o + _diag_blocks(q, k, v, b, c)
    kw = k * jnp.exp(rev)
    e_col = jnp.transpose(jnp.broadcast_to(jnp.exp(b[c - 1:c, :]), (kd, kd)))
    if vd > kd:
        e_col = jnp.concatenate([e_col] * (vd // kd), axis=1)
    s_new = e_col * s + _dot_tn(kw.astype(BF16), v.astype(BF16))
    return o, s_new


def _head_out(o, gn, gate_act):
    return o * lax.rsqrt(jnp.mean(o * o, axis=-1, keepdims=True) + EPS) * gn * gate_act


def _ffn_kernel(*refs, nf, final):
    if final:
        x_ref, g_ref, w1_ref, w3_ref, w2_ref, fg_ref, o_ref, xn_ref, acc_ref = refs
    else:
        x_ref, g_ref, w1_ref, w3_ref, w2_ref, o_ref, xn_ref, acc_ref = refs
    f = pl.program_id(1)

    @pl.when(f == 0)
    def _():
        xn_ref[...] = _rms(x_ref[...], g_ref[...]).astype(BF16)
        acc_ref[...] = jnp.zeros_like(acc_ref)

    xn = xn_ref[...]
    h1 = _dot(xn, w1_ref[...])
    h3 = _dot(xn, w3_ref[...])
    acc_ref[...] += _dot((_silu(h1) * h3).astype(BF16), w2_ref[...])

    @pl.when(f == nf - 1)
    def _():
        y = x_ref[...] + 0.5 * acc_ref[...]
        if final:
            y = _rms(y, fg_ref[...])
        o_ref[...] = y


def _ffn(x, gain, w1, w3, w2, final_gain=None):
    m = x.shape[0]
    nf = FF_DIM // TF
    final = final_gain is not None
    in_specs = [
        pl.BlockSpec((TM, D_MODEL), lambda i, f: (i, 0)),
        pl.BlockSpec((1, D_MODEL), lambda i, f: (0, 0)),
        pl.BlockSpec((D_MODEL, TF), lambda i, f: (0, f)),
        pl.BlockSpec((D_MODEL, TF), lambda i, f: (0, f)),
        pl.BlockSpec((TF, D_MODEL), lambda i, f: (f, 0)),
    ]
    args = [x, gain.reshape(1, D_MODEL), w1, w3, w2]
    if final:
        in_specs.append(pl.BlockSpec((1, D_MODEL), lambda i, f: (0, 0)))
        args.append(final_gain.reshape(1, D_MODEL))
    return pl.pallas_call(
        functools.partial(_ffn_kernel, nf=nf, final=final),
        grid=(m // TM, nf),
        in_specs=in_specs,
        out_specs=pl.BlockSpec((TM, D_MODEL), lambda i, f: (i, 0)),
        out_shape=jax.ShapeDtypeStruct((m, D_MODEL), F32),
        scratch_shapes=[pltpu.VMEM((TM, D_MODEL), BF16), pltpu.VMEM((TM, D_MODEL), F32)],
        compiler_params=_cparams(("parallel", "arbitrary")),
        name="ffn_final" if final else "ffn",
    )(*args)


def _inproj_kernel(x_ref, g_ref, w_ref, z_ref, xn_ref):
    @pl.when(pl.program_id(1) == 0)
    def _():
        xn_ref[...] = _rms(x_ref[...], g_ref[...]).astype(BF16)

    z_ref[...] = _dot(xn_ref[...], w_ref[...])


def _inproj(x, gain, w):
    m, n = x.shape[0], w.shape[1]
    return pl.pallas_call(
        _inproj_kernel,
        grid=(m // TM, n // TN),
        in_specs=[
            pl.BlockSpec((TM, D_MODEL), lambda i, j: (i, 0)),
            pl.BlockSpec((1, D_MODEL), lambda i, j: (0, 0)),
            pl.BlockSpec((D_MODEL, TN), lambda i, j: (0, j)),
        ],
        out_specs=pl.BlockSpec((TM, TN), lambda i, j: (i, j)),
        out_shape=jax.ShapeDtypeStruct((m, n), F32),
        scratch_shapes=[pltpu.VMEM((TM, D_MODEL), BF16)],
        compiler_params=_cparams(("parallel", "arbitrary")),
        name="inproj",
    )(x, gain.reshape(1, D_MODEL), w)


def _outproj_kernel(x_ref, y_ref, w_ref, o_ref):
    o_ref[...] = x_ref[...] + _dot(y_ref[...].astype(BF16), w_ref[...])


def _outproj(x, y, w):
    m, kdim = y.shape
    return pl.pallas_call(
        _outproj_kernel,
        grid=(m // TM,),
        in_specs=[
            pl.BlockSpec((TM, D_MODEL), lambda i: (i, 0)),
            pl.BlockSpec((TM, kdim), lambda i: (i, 0)),
            pl.BlockSpec((kdim, D_MODEL), lambda i: (0, 0)),
        ],
        out_specs=pl.BlockSpec((TM, D_MODEL), lambda i: (i, 0)),
        out_shape=jax.ShapeDtypeStruct((m, D_MODEL), F32),
        compiler_params=_cparams(("parallel",)),
        name="outproj",
    )(x, y, w)


def _sub(i, c, per_seq):
    r = pl.multiple_of(i * c, c)
    return r, (0 if per_seq else r), (i if per_seq else 0)


def _ret_kernel(q_ref, k_ref, v_ref, gt_ref, cos_ref, sin_ref, gn_ref, mc_ref, mk_ref, s_in,
                y_ref, s_out, *, c, nsub, per_seq, levels):
    @pl.when(pl.program_id(2) == 0)
    def _():
        s_out[...] = s_in[...]

    hf = jnp.full((1, DK_A), pl.program_id(1), jnp.int32).astype(F32)
    log_gamma = jnp.log1p(-jnp.exp2(-5.0 - hf))

    def body(i, carry):
        r, tr, si = _sub(i, c, per_seq)
        cos = cos_ref[pl.ds(tr, c), :]
        sin = sin_ref[pl.ds(tr, c), :]
        q = q_ref[pl.ds(r, c), :]
        k = k_ref[pl.ds(r, c), :]
        q = q * cos + pltpu.roll(q, DK_A // 2, 1) * sin
        k = (k * cos + pltpu.roll(k, DK_A // 2, 1) * sin) * (DK_A ** -0.5)
        g = jnp.broadcast_to(log_gamma, (c, DK_A))
        o, s_new = _decay_core(q, k, v_ref[pl.ds(r, c), :], g, s_out[si, 0], mc_ref, mk_ref, levels)
        s_out[si, 0] = s_new
        y_ref[pl.ds(r, c), :] = _head_out(o, gn_ref[...], _silu(gt_ref[pl.ds(r, c), :]))
        return carry

    lax.fori_loop(0, nsub, body, 0)


def _hgrn_kernel(q_ref, f_ref, i_ref, gt_ref, lb_ref, gn_ref, mc_ref, mk_ref, s_in,
                 y_ref, s_out, *, c, nsub, per_seq, levels, layer):
    @pl.when(pl.program_id(2) == 0)
    def _():
        s_out[...] = s_in[...]

    lbl = lb_ref[...]
    e = jnp.exp(lbl - jnp.max(lbl, axis=0, keepdims=True))
    lb = jnp.sum(e[0:layer + 1], axis=0, keepdims=True) / jnp.sum(e, axis=0, keepdims=True)

    def body(i, carry):
        r, _, si = _sub(i, c, per_seq)
        fb = f_ref[pl.ds(r, c), :]
        f_gate = lb + (1.0 - lb) * jax.nn.sigmoid(fb)
        k = (1.0 - lb) * jax.nn.sigmoid(-fb)
        q = _silu(q_ref[pl.ds(r, c), :])
        o, s_new = _decay_core(q, k, i_ref[pl.ds(r, c), :], jnp.log(f_gate), s_out[si, 0],
                               mc_ref, mk_ref, levels)
        s_out[si, 0] = s_new
        y_ref[pl.ds(r, c), :] = _head_out(o, gn_ref[...], _silu(gt_ref[pl.ds(r, c), :]))
        return carry

    lax.fori_loop(0, nsub, body, 0)


def _gla_kernel(q_ref, k_ref, v_ref, gt_ref, tail_ref, w2_ref, ab_ref, gn_ref, mc_ref, mk_ref, s_in,
                y_ref, s_out, *, c, nsub, per_seq, levels):
    @pl.when(pl.program_id(2) == 0)
    def _():
        s_out[...] = s_in[...]

    w2s = _split3(w2_ref[...])

    def body(i, carry):
        r, _, si = _sub(i, c, per_seq)
        t3 = _split3(tail_ref[pl.ds(r, c), :])
        th, tm_, tl = t3[:, :LANES], t3[:, LANES:2 * LANES], t3[:, 2 * LANES:]
        wh, wm, wl = w2s[:, :DK_D], w2s[:, DK_D:2 * DK_D], w2s[:, 2 * DK_D:]
        x = (_dot(th, wh) + (_dot(th, wm) + _dot(tm_, wh))
             + (_dot(th, wl) + _dot(tm_, wm) + _dot(tl, wh)))
        g = _log_sigmoid(x + ab_ref[...]) / GLA_TAU
        q = q_ref[pl.ds(r, c), :] * (DK_D ** -0.5)
        o, s_new = _decay_core(q, k_ref[pl.ds(r, c), :], v_ref[pl.ds(r, c), :], g, s_out[si, 0],
                               mc_ref, mk_ref, levels)
        s_out[si, 0] = s_new
        y_ref[pl.ds(r, c), :] = _head_out(o, gn_ref[...], _silu(gt_ref[pl.ds(r, c), :]))
        return carry

    lax.fori_loop(0, nsub, body, 0)


def _mlstm_kernel(q_ref, k_ref, v_ref, og_ref, tail_ref, wq_ref, wk_ref, bias_ref, gn_ref, tri_ref,
                  cq_in, ck_in, c_in, n_in, m_in,
                  y_ref, c_out, n_out, m_out, uq_ref, uk_ref, *, c, nsub, per_seq):
    h = pl.program_id(1)

    @pl.when(pl.program_id(2) == 0)
    def _():
        c_out[...] = c_in[...]
        n_out[...] = n_in[...]
        m_out[...] = m_in[...]

    lane = lax.broadcasted_iota(jnp.int32, (c, LANES), 1)
    tt = lax.broadcasted_iota(jnp.int32, (c, c), 0)
    ss = lax.broadcasted_iota(jnp.int32, (c, c), 1)
    hist = CONV_W - 1

    def conv(u_ref, raw, w_ref, hist_rows):
        if hist_rows is not None:
            u_ref[SUBLANES - hist:SUBLANES, :] = hist_rows
        u_ref[SUBLANES:SUBLANES + c, :] = raw
        acc = u_ref[SUBLANES - hist:SUBLANES - hist + c, :] * w_ref[0:1, :]
        for j in range(1, CONV_W):
            acc = acc + u_ref[SUBLANES - hist + j:SUBLANES - hist + j + c, :] * w_ref[j:j + 1, :]
        u_ref[0:SUBLANES, :] = u_ref[c:c + SUBLANES, :]
        return _silu(acc)

    def body(i, carry):
        r, _, si = _sub(i, c, per_seq)
        first = jnp.logical_and(pl.program_id(2) == 0, i == 0)

        @pl.when(jnp.logical_or(first, per_seq))
        def _():
            uq_ref[SUBLANES - hist:SUBLANES, :] = cq_in[si]
            uk_ref[SUBLANES - hist:SUBLANES, :] = ck_in[si]

        q = conv(uq_ref, q_ref[pl.ds(r, c), :], wq_ref, None)
        k = conv(uk_ref, k_ref[pl.ds(r, c), :], wk_ref, None) * (DK_C ** -0.5)
        v = v_ref[pl.ds(r, c), :]

        tl = tail_ref[pl.ds(r, c), :] + bias_ref[...]
        li = jnp.sum(jnp.where(lane == h, tl, 0.0), axis=-1, keepdims=True)
        lf = jnp.sum(jnp.where(lane == H_C + h, _log_sigmoid(tl), 0.0), axis=-1, keepdims=True)
        if c == SUBLANES:
            b = _prefix8(lf)[0]
        else:
            b = _sum3(_dot(tri_ref[...], _split3(jnp.broadcast_to(lf, (c, LANES)))), LANES)[:, :1]

        cst = c_out[si, 0]
        nrow = n_out[si, 0]
        m_prev = m_out[si, 0][:, :1]

        d = jnp.where(ss <= tt, b + _row_bcast(li - b, c), -jnp.inf)
        m_inter = b + m_prev
        m_t = jnp.maximum(m_inter, jnp.max(d, axis=-1, keepdims=True))
        w_inter = jnp.exp(m_inter - m_t)
        qb = q.astype(BF16)
        sc = _dot_nt(qb, k.astype(BF16)) * jnp.exp(d - m_t)
        num = w_inter * _dot(qb, cst.astype(BF16)) + _dot(sc.astype(BF16), v.astype(BF16))
        den = w_inter * jnp.sum(q * nrow, axis=-1, keepdims=True) + jnp.sum(sc, axis=-1, keepdims=True)
        hh = num / jnp.maximum(jnp.abs(den), jnp.exp(-m_t))
        m_new = m_t[c - 1:c]
        b_last = b[c - 1:c]
        w_c = jnp.exp(b_last + m_prev - m_new)
        kw = k * jnp.exp(b_last - b + li - m_new)
        c_out[si, 0] = w_c * cst + _dot_tn(kw.astype(BF16), v.astype(BF16))
        n_out[si, 0] = w_c * nrow + jnp.sum(kw, axis=0, keepdims=True)
        m_out[si, 0] = jnp.broadcast_to(m_new, (1, LANES))
        y_ref[pl.ds(r, c), :] = _head_out(hh, gn_ref[...], jax.nn.sigmoid(og_ref[pl.ds(r, c), :]))
        return carry

    lax.fori_loop(0, nsub, body, 0)


def _mixer_geometry(n_seq, seq_len, row0):
    if seq_len >= SLAB:
        c, rows, nsub, per_seq, nseq_blk = CHUNK, SLAB, SLAB // CHUNK, False, 1
        grid_seq, grid_slab = n_seq, seq_len // SLAB
    else:
        c, nseq_blk, per_seq = seq_len, SEQ_PER_STEP, True
        rows, nsub = nseq_blk * c, nseq_blk
        grid_seq, grid_slab = n_seq // nseq_blk, 1
    base = row0 // rows
    rowblk = lambda b, s: base + b * grid_slab + s
    return dict(c=c, rows=rows, nsub=nsub, per_seq=per_seq, nseq_blk=nseq_blk,
                grid_seq=grid_seq, grid_slab=grid_slab, rowblk=rowblk)


def _zspec(geo, width, col):
    rb = geo["rowblk"]
    return pl.BlockSpec((geo["rows"], width), lambda b, h, s: (rb(b, s), col + h))


def _const_spec(shape):
    return pl.BlockSpec(shape, lambda b, h, s: (0,) * len(shape))


def _state_spec(geo, kd, vd):
    return pl.BlockSpec((geo["nseq_blk"], 1, kd, vd), lambda b, h, s: (b, h, 0, 0))


def _decay_mixer(kind, z, geo, n_seq, seq_len, state, extra):
    levels, mc, mk = _decay_consts(geo["c"])
    heads, kd, vd = {"ret": (H_A, DK_A, DV_A), "hgrn": (H_B, DK_B, DV_B), "gla": (H_D, DK_D, DV_D)}[kind]
    common = dict(c=geo["c"], nsub=geo["nsub"], per_seq=geo["per_seq"], levels=levels)
    kb, vb = LANES, vd
    if kind == "ret":
        cos, sin = extra["rope"]
        tab = pl.BlockSpec((geo["c"] if geo["per_seq"] else geo["rows"], DK_A),
                           lambda b, h, s: (0 if geo["per_seq"] else s, 0))
        in_specs = [_zspec(geo, kb, 0), _zspec(geo, kb, H_A), _zspec(geo, vb, (2 * H_A * DK_A) // vb),
                    _zspec(geo, vb, (2 * H_A * DK_A) // vb + H_A), tab, tab,
                    pl.BlockSpec((1, vd), lambda b, h, s: (0, h))]
        args = [z, z, z, z, cos, sin, extra["gn"].reshape(1, -1)]
        kern = functools.partial(_ret_kernel, **common)
    elif kind == "hgrn":
        off = (2 * H_A * DK_A + 2 * H_A * DV_A) // LANES
        in_specs = [_zspec(geo, kb, off), _zspec(geo, kb, off + H_B), _zspec(geo, kb, off + 2 * H_B),
                    _zspec(geo, kb, off + 3 * H_B),
                    pl.BlockSpec((DEPTH + 1, DK_B), lambda b, h, s: (0, h)),
                    pl.BlockSpec((1, vd), lambda b, h, s: (0, h))]
        args = [z, z, z, z, extra["lb_logits"], extra["gn"].reshape(1, -1)]
        kern = functools.partial(_hgrn_kernel, layer=extra["layer"], **common)
    else:
        off = (2 * H_C * DK_C + 2 * H_C * DV_C) // LANES
        voff = (2 * H_C * DK_C + 2 * H_C * DV_C + 2 * H_D * DK_D) // vb
        in_specs = [_zspec(geo, kb, off), _zspec(geo, kb, off + H_D), _zspec(geo, vb, voff),
                    _zspec(geo, vb, voff + H_D),
                    pl.BlockSpec((geo["rows"], CD_TAIL), lambda b, h, s: (geo["rowblk"](b, s), CD_MAIN // CD_TAIL)),
                    pl.BlockSpec((CD_TAIL, DK_D), lambda b, h, s: (0, h)),
                    pl.BlockSpec((1, DK_D), lambda b, h, s: (0, h)),
                    pl.BlockSpec((1, vd), lambda b, h, s: (0, h))]
        args = [z, z, z, z, z, extra["w2"], extra["ab"].reshape(1, -1), extra["gn"].reshape(1, -1)]
        kern = functools.partial(_gla_kernel, **common)
    in_specs += [_const_spec(mc.shape), _const_spec(mk.shape), _state_spec(geo, kd, vd)]
    args += [mc, mk, state]
    rows_total = n_seq * seq_len
    y, s_new = pl.pallas_call(
        kern,
        grid=(geo["grid_seq"], heads, geo["grid_slab"]),
        in_specs=in_specs,
        out_specs=[pl.BlockSpec((geo["rows"], vd), lambda b, h, s: (b * geo["grid_slab"] + s, h)),
                   _state_spec(geo, kd, vd)],
        out_shape=[jax.ShapeDtypeStruct((rows_total, heads * vd), F32),
                   jax.ShapeDtypeStruct(state.shape, F32)],
        compiler_params=_cparams(("parallel", "parallel", "arbitrary")),
        name=kind + ("_sample" if geo["per_seq"] else "_prompt"),
    )(*args)
    return y, s_new


def _mlstm_mixer(z, geo, n_seq, seq_len, c0, n0, m0, conv0, w):
    c = geo["c"]
    t = np.arange(c)
    tri = jnp.asarray((t[None, :] <= t[:, None]).astype(np.float32), BF16)
    bias = jnp.zeros((1, LANES), F32).at[0, :H_C].set(w["i_bias"]).at[0, H_C:2 * H_C].set(w["f_bias"])
    nb = geo["nseq_blk"]
    kq = 0
    kk = (H_C * DK_C) // DK_C
    kv = (2 * H_C * DK_C) // DV_C
    ko = kv + H_C
    st = lambda shape: pl.BlockSpec((nb, 1) + shape, lambda b, h, s: (b, h, 0, 0))
    in_specs = [
        _zspec(geo, DK_C, kq), _zspec(geo, DK_C, kk), _zspec(geo, DV_C, kv), _zspec(geo, DV_C, ko),
        pl.BlockSpec((geo["rows"], CD_TAIL), lambda b, h, s: (geo["rowblk"](b, s), CD_MAIN // CD_TAIL)),
        pl.BlockSpec((CONV_W, DK_C), lambda b, h, s: (0, h)),
        pl.BlockSpec((CONV_W, DK_C), lambda b, h, s: (0, H_C + h)),
        _const_spec((1, LANES)),
        pl.BlockSpec((1, DV_C), lambda b, h, s: (0, h)),
        _const_spec(tri.shape),
        pl.BlockSpec((nb, CONV_W - 1, DK_C), lambda b, h, s: (b, 0, h)),
        pl.BlockSpec((nb, CONV_W - 1, DK_C), lambda b, h, s: (b, 0, H_C + h)),
        st((DK_C, DV_C)), st((1, DK_C)), st((1, LANES)),
    ]
    n4 = n0.reshape(n_seq, H_C, 1, DK_C)
    m4 = jnp.broadcast_to(m0[:, :, None, None], (n_seq, H_C, 1, LANES))
    y, c_new, n_new, m_new = pl.pallas_call(
        functools.partial(_mlstm_kernel, c=c, nsub=geo["nsub"], per_seq=geo["per_seq"]),
        grid=(geo["grid_seq"], H_C, geo["grid_slab"]),
        in_specs=in_specs,
        out_specs=[pl.BlockSpec((geo["rows"], DV_C), lambda b, h, s: (b * geo["grid_slab"] + s, h)),
                   st((DK_C, DV_C)), st((1, DK_C)), st((1, LANES))],
        out_shape=[jax.ShapeDtypeStruct((n_seq * seq_len, H_C * DV_C), F32),
                   jax.ShapeDtypeStruct(c0.shape, F32),
                   jax.ShapeDtypeStruct(n4.shape, F32),
                   jax.ShapeDtypeStruct(m4.shape, F32)],
        scratch_shapes=[pltpu.VMEM((SUBLANES + c, DK_C), F32), pltpu.VMEM((SUBLANES + c, DK_C), F32)],
        compiler_params=_cparams(("parallel", "parallel", "arbitrary")),
        name="mlstm" + ("_sample" if geo["per_seq"] else "_prompt"),
    )(z, z, z, z, z, w["conv_w"], w["conv_w"], bias, w["gn_c"].reshape(1, -1), tri,
      conv0, conv0, c0, n4, m4)
    return y, c_new, n_new.reshape(n_seq, H_C, DK_C), m_new[:, :, 0, 0]


def _rope_tables(pos0, n):
    half = DK_A // 2
    inv = ROPE_BASE ** (-jnp.arange(half, dtype=F32) / half)
    ang = (jnp.arange(n, dtype=F32) + float(pos0))[:, None] * inv[None, :]
    cos, sin = jnp.cos(ang), jnp.sin(ang)
    return jnp.concatenate([cos, cos], axis=1), jnp.concatenate([-sin, sin], axis=1)


def kernel(x_prompt, x_sample, state_ret, state_hgrn, state_mlstm_c, state_mlstm_n, state_mlstm_m,
           state_mlstm_conv, state_gla, norm_gain, ffn_w1, ffn_w3, ffn_w2, ab_w_in, ab_w_out, ab_norm_a,
           ab_norm_b, lb_logits, cd_w_in, cd_w_out, cd_conv_w, cd_i_bias, cd_f_bias, cd_norm_c,
           cd_alpha_w2, cd_alpha_b, cd_norm_d, final_norm):
    bp, tp, _ = x_prompt.shape
    bs, ts, _ = x_sample.shape
    n_p, n_s = bp * tp, bs * ts
    x = jnp.concatenate([x_prompt.reshape(n_p, D_MODEL), x_sample.reshape(n_s, D_MODEL)], axis=0)
    geo_p = _mixer_geometry(bp, tp, 0)
    geo_s = _mixer_geometry(bs, ts, n_p)
    rope_p = _rope_tables(0, tp)
    rope_s = _rope_tables(PAST_LEN, ts)
    outs = {}

    for l in range(DEPTH):
        j = l // 2
        x = _ffn(x, norm_gain[l, 0], ffn_w1[l, 0].astype(BF16), ffn_w3[l, 0].astype(BF16),
                 ffn_w2[l, 0].astype(BF16))
        if l % 2 == 0:
            z = _inproj(x, norm_gain[l, 1], ab_w_in[j].astype(BF16))
            ys = []
            for tag, geo, nseq, slen, rope, s_ret, s_hg in (
                    ("p", geo_p, bp, tp, rope_p, jnp.zeros((bp, H_A, DK_A, DV_A), F32),
                     jnp.zeros((bp, H_B, DK_B, DV_B), F32)),
                    ("s", geo_s, bs, ts, rope_s, state_ret[j], state_hgrn[j])):
                ya, r_new = _decay_mixer("ret", z, geo, nseq, slen, s_ret,
                                         dict(rope=rope, gn=ab_norm_a[j]))
                yb, g_new = _decay_mixer("hgrn", z, geo, nseq, slen, s_hg,
                                         dict(lb_logits=lb_logits, gn=ab_norm_b[j], layer=l))
                ys.append(jnp.concatenate([ya, yb], axis=1))
                outs.setdefault("ret_" + tag, []).append(r_new)
                outs.setdefault("hgrn_" + tag, []).append(g_new)
            x = _outproj(x, jnp.concatenate(ys, axis=0), ab_w_out[j].astype(BF16))
        else:
            w_in = cd_w_in[j]
            gates = 2 * H_C
            main = 2 * H_C * DK_C + 2 * H_C * DV_C
            w_perm = jnp.concatenate(
                [w_in[:, :main], w_in[:, main + gates:main + gates + (CD_MAIN - main)],
                 w_in[:, main:main + gates], w_in[:, main + gates + (CD_MAIN - main):]], axis=1)
            n_pad = -(-(CD_MAIN + CD_TAIL) // TN) * TN
            w_perm = jnp.pad(w_perm, ((0, 0), (0, n_pad - w_perm.shape[1]))).astype(BF16)
            z = _inproj(x, norm_gain[l, 1], w_perm)
            w2p = jnp.zeros((CD_TAIL, H_D * DK_D), F32).at[gates:gates + GLA_RANK].set(cd_alpha_w2[j])
            ys = []
            for tag, geo, nseq, slen, row0, c0, n0, m0, cv0, s_gl in (
                    ("p", geo_p, bp, tp, 0, jnp.zeros((bp, H_C, DK_C, DV_C), F32),
                     jnp.zeros((bp, H_C, DK_C), F32), jnp.zeros((bp, H_C), F32),
                     jnp.zeros((bp, CONV_W - 1, 2 * H_C * DK_C), F32), jnp.zeros((bp, H_D, DK_D, DV_D), F32)),
                    ("s", geo_s, bs, ts, n_p, state_mlstm_c[j], state_mlstm_n[j], state_mlstm_m[j],
                     state_mlstm_conv[j], state_gla[j])):
                yc, c_new, n_new, m_new = _mlstm_mixer(
                    z, geo, nseq, slen, c0, n0, m0, cv0,
                    dict(conv_w=cd_conv_w[j], i_bias=cd_i_bias[j], f_bias=cd_f_bias[j], gn_c=cd_norm_c[j]))
                yd, gl_new = _decay_mixer("gla", z, geo, nseq, slen, s_gl,
                                          dict(w2=w2p, ab=cd_alpha_b[j], gn=cd_norm_d[j]))
                ys.append(jnp.concatenate([yc, yd], axis=1))
                u = z[row0:row0 + nseq * slen, :2 * H_C * DK_C].reshape(nseq, slen, 2 * H_C * DK_C)
                outs.setdefault("c_" + tag, []).append(c_new)
                outs.setdefault("n_" + tag, []).append(n_new)
                outs.setdefault("m_" + tag, []).append(m_new)
                outs.setdefault("conv_" + tag, []).append(u[:, slen - (CONV_W - 1):])
                outs.setdefault("gla_" + tag, []).append(gl_new)
            x = _outproj(x, jnp.concatenate(ys, axis=0), cd_w_out[j].astype(BF16))
        x = _ffn(x, norm_gain[l, 2], ffn_w1[l, 1].astype(BF16), ffn_w3[l, 1].astype(BF16),
                 ffn_w2[l, 1].astype(BF16), final_gain=final_norm if l == DEPTH - 1 else None)

    y_p = x[:n_p].reshape(bp, tp, D_MODEL)
    y_s = x[n_p:].reshape(bs, ts, D_MODEL)
    st = lambda name: jnp.stack(outs[name])
    return (y_p, y_s, st("ret_p"), st("ret_s"), st("hgrn_p"), st("hgrn_s"), st("c_p"), st("c_s"),
            st("n_p"), st("n_s"), st("m_p"), st("m_s"), st("conv_p"), st("conv_s"), st("gla_p"), st("gla_s"))
```

```python
import functools
import math

import numpy as np
import jax
import jax.numpy as jnp
from jax import lax
from jax.experimental import pallas as pl
from jax.experimental.pallas import tpu as pltpu

F32 = jnp.float32
BF16 = jnp.bfloat16

D_MODEL = 2048
DEPTH = 2
PAST_LEN = 16384
H_A, DK_A, DV_A = 4, 128, 256
ROPE_BASE = 10000.0
H_B, DK_B, DV_B = 8, 128, 128
H_C, DK_C, DV_C = 4, 256, 256
CONV_W = 4
H_D, DK_D, DV_D = 4, 128, 256
GLA_RANK = 16
GLA_TAU = 16.0
FF_DIM = 5632
EPS = 1e-6

AB_IN = 2 * H_A * DK_A + 2 * H_A * DV_A + 2 * H_B * DK_B + 2 * H_B * DV_B
CD_MAIN = 2 * H_C * DK_C + 2 * H_C * DV_C + 2 * H_D * DK_D + 2 * H_D * DV_D
CD_TAIL = 128
AB_MIX = H_A * DV_A + H_B * DV_B
CD_MIX = H_C * DV_C + H_D * DV_D

LANES = 128
SUBLANES = 8
VMEM_LIMIT = 56 * 1024 * 1024

TM = 512
TF = 512
TN = 512
CHUNK = 64
SLAB = 256
SEQ_PER_STEP = 16


def _cparams(sem):
    return pltpu.CompilerParams(dimension_semantics=sem, vmem_limit_bytes=VMEM_LIMIT)


def _rms(x, g):
    return x * lax.rsqrt(jnp.mean(x * x, axis=-1, keepdims=True) + EPS) * g


def _silu(x):
    return x * jax.nn.sigmoid(x)


def _log_sigmoid(x):
    return jnp.minimum(x, 0.0) - jnp.log1p(jnp.exp(-jnp.abs(x)))


def _pieces3(x):
    hi = x.astype(BF16).astype(F32)
    r1 = x - hi
    mid = r1.astype(BF16).astype(F32)
    lo = (r1 - mid).astype(BF16).astype(F32)
    return hi, mid, lo


def _dot(a, b):
    return jnp.dot(a, b, preferred_element_type=F32)


def _dot_nt(a, b):
    return lax.dot_general(a, b, (((1,), (1,)), ((), ())), preferred_element_type=F32)


def _dot_tn(a, b):
    return lax.dot_general(a, b, (((0,), (0,)), ((), ())), preferred_element_type=F32)


def _prefix(g):
    c, n = g.shape
    row = lax.broadcasted_iota(jnp.int32, (SUBLANES, n), 0)
    outs = []
    off = None
    for r0 in range(0, c, SUBLANES):
        x = g[r0:r0 + SUBLANES]
        for sh in (1, 2, 4):
            x = x + jnp.where(row >= sh, pltpu.roll(x, sh, 0), 0.0)
        if off is not None:
            x = x + off
        off = x[SUBLANES - 1:SUBLANES]
        outs.append(x)
    return outs[0] if len(outs) == 1 else jnp.concatenate(outs, axis=0)


def _row_bcast(col, c):
    lane = lax.broadcasted_iota(jnp.int32, (c, LANES), 1)
    pieces = _pieces3(jnp.where(lane == 0, jnp.broadcast_to(col, (c, LANES)), 0.0))
    rm = jnp.concatenate(pieces, axis=1).astype(BF16)
    ones = jnp.ones((c, 3 * LANES), BF16)
    return _dot_nt(ones, rm)


def _levels(c):
    out = []
    m = SUBLANES
    while m < c:
        out.append(m)
        m *= 2
    return tuple(out)


def _level_masks(c):
    t = np.arange(c)[:, None]
    u = np.arange(c)[None, :]
    masks = [((t % (2 * m)) >= m) & ((u % (2 * m)) < m) & (t // (2 * m) == u // (2 * m))
             for m in _levels(c)]
    return jnp.asarray(np.stack(masks).astype(np.float32))


def _diag_blocks(q, k, v, b, c):
    row = lax.broadcasted_iota(jnp.int32, (SUBLANES, 1), 0)
    outs = []
    for n in range(c // SUBLANES):
        r0 = n * SUBLANES
        qb = q[r0:r0 + SUBLANES]
        bb = b[r0:r0 + SUBLANES]
        ob = jnp.zeros((SUBLANES, v.shape[1]), F32)
        for j in range(SUBLANES):
            r = r0 + j
            e = jnp.exp(jnp.minimum(bb - b[r:r + 1], 0.0))
            a = jnp.sum(qb * k[r:r + 1] * e, axis=-1, keepdims=True)
            ob = ob + jnp.where(row >= j, a, 0.0) * v[r:r + 1]
        outs.append(ob)
    return outs[0] if len(outs) == 1 else jnp.concatenate(outs, axis=0)


def _state_update(kw, v, e_row, s):
    c, kd = kw.shape
    vd = v.shape[1]
    pad = 2 * SUBLANES
    lhs = jnp.concatenate(_pieces3(e_row) + (jnp.zeros((pad - 3, kd), F32), kw), axis=0)
    sel = (lax.broadcasted_iota(jnp.int32, (pad, LANES), 0) < 3).astype(F32)
    rhs = jnp.concatenate(
        [jnp.concatenate([jnp.zeros((pad, vd), F32), sel], axis=1),
         jnp.concatenate([v, jnp.zeros((c, LANES), F32)], axis=1)], axis=0)
    upd = _dot_tn(lhs.astype(BF16), rhs.astype(BF16))
    e_col = upd[:, vd:]
    if vd > LANES:
        e_col = jnp.concatenate([e_col] * (vd // LANES), axis=1)
    return e_col * s + upd[:, :vd]


def _decay_core(q, k, v, g, s, mk_ref):
    c, kd = q.shape
    b = _prefix(g)
    o = _dot((q * jnp.exp(b)).astype(BF16), s.astype(BF16))
    levels = _levels(c)
    if levels:
        a = jnp.zeros((c, c), F32)
        for li, m in enumerate(levels):
            mid = jnp.concatenate([jnp.broadcast_to(b[r0 + m - 1:r0 + m], (2 * m, kd))
                                   for r0 in range(0, c, 2 * m)], axis=0)
            w = jnp.exp(-jnp.abs(b - mid))
            a = a + mk_ref[li] * _dot_nt((q * w).astype(BF16), (k * w).astype(BF16))
        o = o + _dot(a.astype(BF16), v.astype(BF16))
    o = o + _diag_blocks(q, k, v, b, c)
    b_last = b[c - 1:c]
    s_new = _state_update(k * jnp.exp(b_last - b), v, jnp.exp(b_last), s)
    return o, s_new


def _scalar_decay_core(q, k, v, log_decay, s):
    c = q.shape[0]
    tt = lax.broadcasted_iota(jnp.int32, (c, c), 0)
    ss = lax.broadcasted_iota(jnp.int32, (c, c), 1)
    t1 = lax.broadcasted_iota(jnp.int32, (c, 1), 0).astype(F32)
    dec = jnp.where(ss <= tt, jnp.exp(jnp.maximum(tt - ss, 0).astype(F32) * log_decay), 0.0)
    qb = q.astype(BF16)
    a = _dot_nt(qb, k.astype(BF16)) * dec
    o = _dot((q * jnp.exp((t1 + 1.0) * log_decay)).astype(BF16), s.astype(BF16))
    o = o + _dot(a.astype(BF16), v.astype(BF16))
    kw = k * jnp.exp((c - 1.0 - t1) * log_decay)
    s_new = jnp.exp(c * log_decay) * s + _dot_tn(kw.astype(BF16), v.astype(BF16))
    return o, s_new


def _head_out(o, gn, gate_act):
    return o * lax.rsqrt(jnp.mean(o * o, axis=-1, keepdims=True) + EPS) * gn * gate_act


def _ffn_kernel(*refs, nf, final):
    if final:
        x_ref, g_ref, w1_ref, w3_ref, w2_ref, fg_ref, o_ref, xn_ref, acc_ref = refs
    else:
        x_ref, g_ref, w1_ref, w3_ref, w2_ref, o_ref, xn_ref, acc_ref = refs
    f = pl.program_id(1)

    @pl.when(f == 0)
    def _():
        xn_ref[...] = _rms(x_ref[...], g_ref[...]).astype(BF16)
        acc_ref[...] = jnp.zeros_like(acc_ref)

    xn = xn_ref[...]
    h1 = _dot(xn, w1_ref[...])
    h3 = _dot(xn, w3_ref[...])
    acc_ref[...] += _dot((_silu(h1) * h3).astype(BF16), w2_ref[...])

    @pl.when(f == nf - 1)
    def _():
        y = x_ref[...] + 0.5 * acc_ref[...]
        if final:
            y = _rms(y, fg_ref[...])
        o_ref[...] = y


def _ffn(x, gain, w1, w3, w2, final_gain=None):
    m = x.shape[0]
    nf = FF_DIM // TF
    final = final_gain is not None
    in_specs = [
        pl.BlockSpec((TM, D_MODEL), lambda i, f: (i, 0)),
        pl.BlockSpec((1, D_MODEL), lambda i, f: (0, 0)),
        pl.BlockSpec((D_MODEL, TF), lambda i, f: (0, f)),
        pl.BlockSpec((D_MODEL, TF), lambda i, f: (0, f)),
        pl.BlockSpec((TF, D_MODEL), lambda i, f: (f, 0)),
    ]
    args = [x, gain.reshape(1, D_MODEL), w1, w3, w2]
    if final:
        in_specs.append(pl.BlockSpec((1, D_MODEL), lambda i, f: (0, 0)))
        args.append(final_gain.reshape(1, D_MODEL))
    return pl.pallas_call(
        functools.partial(_ffn_kernel, nf=nf, final=final),
        grid=(m // TM, nf),
        in_specs=in_specs,
        out_specs=pl.BlockSpec((TM, D_MODEL), lambda i, f: (i, 0)),
        out_shape=jax.ShapeDtypeStruct((m, D_MODEL), F32),
        scratch_shapes=[pltpu.VMEM((TM, D_MODEL), BF16), pltpu.VMEM((TM, D_MODEL), F32)],
        compiler_params=_cparams(("parallel", "arbitrary")),
        name="ffn_final" if final else "ffn",
    )(*args)


def _inproj_kernel(x_ref, g_ref, w_ref, z_ref, xn_ref):
    @pl.when(pl.program_id(1) == 0)
    def _():
        xn_ref[...] = _rms(x_ref[...], g_ref[...]).astype(BF16)

    z_ref[...] = _dot(xn_ref[...], w_ref[...])


def _inproj(x, gain, w):
    m, n = x.shape[0], w.shape[1]
    return pl.pallas_call(
        _inproj_kernel,
        grid=(m // TM, n // TN),
        in_specs=[
            pl.BlockSpec((TM, D_MODEL), lambda i, j: (i, 0)),
            pl.BlockSpec((1, D_MODEL), lambda i, j: (0, 0)),
            pl.BlockSpec((D_MODEL, TN), lambda i, j: (0, j)),
        ],
        out_specs=pl.BlockSpec((TM, TN), lambda i, j: (i, j)),
        out_shape=jax.ShapeDtypeStruct((m, n), F32),
        scratch_shapes=[pltpu.VMEM((TM, D_MODEL), BF16)],
        compiler_params=_cparams(("parallel", "arbitrary")),
        name="inproj",
    )(x, gain.reshape(1, D_MODEL), w)


def _outproj_kernel(x_ref, y_ref, w_ref, o_ref):
    o_ref[...] = x_ref[...] + _dot(y_ref[...].astype(BF16), w_ref[...])


def _outproj(x, y, w):
    m, kdim = y.shape
    return pl.pallas_call(
        _outproj_kernel,
        grid=(m // TM,),
        in_specs=[
            pl.BlockSpec((TM, D_MODEL), lambda i: (i, 0)),
            pl.BlockSpec((TM, kdim), lambda i: (i, 0)),
            pl.BlockSpec((kdim, D_MODEL), lambda i: (0, 0)),
        ],
        out_specs=pl.BlockSpec((TM, D_MODEL), lambda i: (i, 0)),
        out_shape=jax.ShapeDtypeStruct((m, D_MODEL), F32),
        compiler_params=_cparams(("parallel",)),
        name="outproj",
    )(x, y, w)


def _run_subchunks(c, nsub, per_seq, ins, outs, step):
    if per_seq:
        for i in range(nsub):
            new = step(i, i * c, 0, tuple(r[i, 0] for r in ins))
            for o, val in zip(outs, new):
                o[i, 0] = val
    else:
        @pl.when(pl.program_id(2) == 0)
        def _():
            for o, r in zip(outs, ins):
                o[...] = r[...]

        st = tuple(o[0, 0] for o in outs)
        for i in range(nsub):
            st = step(i, i * c, i * c, st)
        for o, val in zip(outs, st):
            o[0, 0] = val


def _ret_kernel(q_ref, k_ref, v_ref, gt_ref, cos_ref, sin_ref, gn_ref, s_in,
                y_ref, s_out, *, c, nsub, per_seq):
    hf = jnp.full((1, 1), pl.program_id(1), jnp.int32).astype(F32)
    log_gamma = jnp.log1p(-jnp.exp2(-5.0 - hf))

    def step(i, r, tr, st):
        cos = cos_ref[tr:tr + c, :]
        sin = sin_ref[tr:tr + c, :]
        q = q_ref[r:r + c, :]
        k = k_ref[r:r + c, :]
        q = q * cos + pltpu.roll(q, DK_A // 2, 1) * sin
        k = (k * cos + pltpu.roll(k, DK_A // 2, 1) * sin) * (DK_A ** -0.5)
        o, s_new = _scalar_decay_core(q, k, v_ref[r:r + c, :], log_gamma, st[0])
        y_ref[r:r + c, :] = _head_out(o, gn_ref[...], _silu(gt_ref[r:r + c, :]))
        return (s_new,)

    _run_subchunks(c, nsub, per_seq, (s_in,), (s_out,), step)


def _hgrn_kernel(q_ref, f_ref, i_ref, gt_ref, lb_ref, gn_ref, *rest, c, nsub, per_seq, layer):
    mk_ref = rest[0] if _levels(c) else None
    s_in, y_ref, s_out = rest[-3:]
    lbl = lb_ref[...]
    e = jnp.exp(lbl - jnp.max(lbl, axis=0, keepdims=True))
    lb = jnp.sum(e[0:layer + 1], axis=0, keepdims=True) / jnp.sum(e, axis=0, keepdims=True)

    def step(i, r, tr, st):
        fb = f_ref[r:r + c, :]
        f_gate = lb + (1.0 - lb) * jax.nn.sigmoid(fb)
        k = (1.0 - lb) * jax.nn.sigmoid(-fb)
        q = _silu(q_ref[r:r + c, :])
        o, s_new = _decay_core(q, k, i_ref[r:r + c, :], jnp.log(f_gate), st[0], mk_ref)
        y_ref[r:r + c, :] = _head_out(o, gn_ref[...], _silu(gt_ref[r:r + c, :]))
        return (s_new,)

    _run_subchunks(c, nsub, per_seq, (s_in,), (s_out,), step)


def _gla_kernel(q_ref, k_ref, v_ref, gt_ref, tail_ref, w2_ref, ab_ref, gn_ref, *rest, c, nsub, per_seq):
    mk_ref = rest[0] if _levels(c) else None
    s_in, y_ref, s_out = rest[-3:]
    wh, wm, wl = (p.astype(BF16) for p in _pieces3(w2_ref[...]))

    def step(i, r, tr, st):
        th, tm_, tl = (p.astype(BF16) for p in _pieces3(tail_ref[r:r + c, :]))
        x = (_dot(th, wh) + (_dot(th, wm) + _dot(tm_, wh))
             + (_dot(th, wl) + _dot(tm_, wm) + _dot(tl, wh)))
        g = _log_sigmoid(x + ab_ref[...]) / GLA_TAU
        q = q_ref[r:r + c, :] * (DK_D ** -0.5)
        o, s_new = _decay_core(q, k_ref[r:r + c, :], v_ref[r:r + c, :], g, st[0], mk_ref)
        y_ref[r:r + c, :] = _head_out(o, gn_ref[...], _silu(gt_ref[r:r + c, :]))
        return (s_new,)

    _run_subchunks(c, nsub, per_seq, (s_in,), (s_out,), step)


def _mlstm_kernel(q_ref, k_ref, v_ref, og_ref, tail_ref, wq_ref, wk_ref, bias_ref, gn_ref,
                  cq_in, ck_in, c_in, n_in, m_in,
                  y_ref, c_out, n_out, m_out, uq_ref, uk_ref, hq_ref, hk_ref, *, c, nsub, per_seq):
    h = pl.program_id(1)
    lane = lax.broadcasted_iota(jnp.int32, (c, LANES), 1)
    tt = lax.broadcasted_iota(jnp.int32, (c, c), 0)
    ss = lax.broadcasted_iota(jnp.int32, (c, c), 1)
    hist = CONV_W - 1
    lo = SUBLANES - hist

    if not per_seq:
        @pl.when(pl.program_id(2) == 0)
        def _():
            for h_ref, cin in ((hq_ref, cq_in), (hk_ref, ck_in)):
                h_ref[...] = jnp.zeros_like(h_ref)
                h_ref[lo:SUBLANES, :] = cin[0]

    prev_raw = {}

    def conv(name, u_ref, h_ref, cin, raw, w_ref, i):
        if per_seq:
            u_ref[i, lo:SUBLANES, :] = cin[i]
        elif i == 0:
            u_ref[0, 0:SUBLANES, :] = h_ref[...]
        else:
            u_ref[i, 0:SUBLANES, :] = prev_raw[name][c - SUBLANES:c]
        u_ref[i, SUBLANES:SUBLANES + c, :] = raw
        prev_raw[name] = raw
        acc = u_ref[i, lo:lo + c, :] * w_ref[0:1, :]
        for j in range(1, CONV_W):
            acc = acc + u_ref[i, lo + j:lo + j + c, :] * w_ref[j:j + 1, :]
        return _silu(acc)

    def step(i, r, tr, st):
        cst, nrow, mrow = st
        q = conv("q", uq_ref, hq_ref, cq_in, q_ref[r:r + c, :], wq_ref, i)
        k = conv("k", uk_ref, hk_ref, ck_in, k_ref[r:r + c, :], wk_ref, i) * (DK_C ** -0.5)
        v = v_ref[r:r + c, :]

        tl = tail_ref[r:r + c, :] + bias_ref[...]
        li = jnp.sum(jnp.where(lane == h, tl, 0.0), axis=-1, keepdims=True)
        lf = jnp.sum(jnp.where(lane == H_C + h, _log_sigmoid(tl), 0.0), axis=-1, keepdims=True)
        b = _prefix(jnp.broadcast_to(lf, (c, LANES)))[:, :1]
        m_prev = mrow[:, :1]

        d = jnp.where(ss <= tt, b + _row_bcast(li - b, c), -jnp.inf)
        m_inter = b + m_prev
        m_t = jnp.maximum(m_inter, jnp.max(d, axis=-1, keepdims=True))
        w_inter = jnp.exp(m_inter - m_t)
        qb = q.astype(BF16)
        sc = _dot_nt(qb, k.astype(BF16)) * jnp.exp(d - m_t)
        num = w_inter * _dot(qb, cst.astype(BF16)) + _dot(sc.astype(BF16), v.astype(BF16))
        den = w_inter * jnp.sum(q * nrow, axis=-1, keepdims=True) + jnp.sum(sc, axis=-1, keepdims=True)
        hh = num / jnp.maximum(jnp.abs(den), jnp.exp(-m_t))
        m_new = m_t[c - 1:c]
        b_last = b[c - 1:c]
        w_c = jnp.exp(b_last + m_prev - m_new)
        kw = k * jnp.exp(b_last - b + li - m_new)
        y_ref[r:r + c, :] = _head_out(hh, gn_ref[...], jax.nn.sigmoid(og_ref[r:r + c, :]))
        return (w_c * cst + _dot_tn(kw.astype(BF16), v.astype(BF16)),
                w_c * nrow + jnp.sum(kw, axis=0, keepdims=True),
                jnp.broadcast_to(m_new, (1, LANES)))

    _run_subchunks(c, nsub, per_seq, (c_in, n_in, m_in), (c_out, n_out, m_out), step)

    if not per_seq:
        hq_ref[...] = prev_raw["q"][c - SUBLANES:c]
        hk_ref[...] = prev_raw["k"][c - SUBLANES:c]


def _mixer_geometry(n_seq, seq_len, row0):
    if seq_len >= SLAB:
        c, rows, nsub, per_seq, nseq_blk = CHUNK, SLAB, SLAB // CHUNK, False, 1
        grid_seq, grid_slab = n_seq, seq_len // SLAB
    else:
        c, nseq_blk, per_seq = seq_len, SEQ_PER_STEP, True
        rows, nsub = nseq_blk * c, nseq_blk
        grid_seq, grid_slab = n_seq // nseq_blk, 1
    base = row0 // rows
    rowblk = lambda b, s: base + b * grid_slab + s
    return dict(c=c, rows=rows, nsub=nsub, per_seq=per_seq, nseq_blk=nseq_blk,
                grid_seq=grid_seq, grid_slab=grid_slab, rowblk=rowblk)


def _zspec(geo, width, col):
    rb = geo["rowblk"]
    return pl.BlockSpec((geo["rows"], width), lambda b, h, s: (rb(b, s), col + h))


def _const_spec(shape):
    return pl.BlockSpec(shape, lambda b, h, s: (0,) * len(shape))


def _state_spec(geo, kd, vd):
    return pl.BlockSpec((geo["nseq_blk"], 1, kd, vd), lambda b, h, s: (b, h, 0, 0))


def _decay_mixer(kind, z, geo, n_seq, seq_len, state, extra):
    heads, kd, vd = {"ret": (H_A, DK_A, DV_A), "hgrn": (H_B, DK_B, DV_B), "gla": (H_D, DK_D, DV_D)}[kind]
    common = dict(c=geo["c"], nsub=geo["nsub"], per_seq=geo["per_seq"])
    kb, vb = LANES, vd
    if kind == "ret":
        cos, sin = extra["rope"]
        tab = pl.BlockSpec((geo["c"] if geo["per_seq"] else geo["rows"], DK_A),
                           lambda b, h, s: (0 if geo["per_seq"] else s, 0))
        in_specs = [_zspec(geo, kb, 0), _zspec(geo, kb, H_A), _zspec(geo, vb, (2 * H_A * DK_A) // vb),
                    _zspec(geo, vb, (2 * H_A * DK_A) // vb + H_A), tab, tab,
                    pl.BlockSpec((1, vd), lambda b, h, s: (0, h))]
        args = [z, z, z, z, cos, sin, extra["gn"].reshape(1, -1)]
        kern = functools.partial(_ret_kernel, **common)
    elif kind == "hgrn":
        off = (2 * H_A * DK_A + 2 * H_A * DV_A) // LANES
        in_specs = [_zspec(geo, kb, off), _zspec(geo, kb, off + H_B), _zspec(geo, kb, off + 2 * H_B),
                    _zspec(geo, kb, off + 3 * H_B),
                    pl.BlockSpec((DEPTH + 1, DK_B), lambda b, h, s: (0, h)),
                    pl.BlockSpec((1, vd), lambda b, h, s: (0, h))]
        args = [z, z, z, z, extra["lb_logits"], extra["gn"].reshape(1, -1)]
        kern = functools.partial(_hgrn_kernel, layer=extra["layer"], **common)
    else:
        off = (2 * H_C * DK_C + 2 * H_C * DV_C) // LANES
        voff = (2 * H_C * DK_C + 2 * H_C * DV_C + 2 * H_D * DK_D) // vb
        in_specs = [_zspec(geo, kb, off), _zspec(geo, kb, off + H_D), _zspec(geo, vb, voff),
                    _zspec(geo, vb, voff + H_D),
                    pl.BlockSpec((geo["rows"], CD_TAIL), lambda b, h, s: (geo["rowblk"](b, s), CD_MAIN // CD_TAIL)),
                    pl.BlockSpec((CD_TAIL, DK_D), lambda b, h, s: (0, h)),
                    pl.BlockSpec((1, DK_D), lambda b, h, s: (0, h)),
                    pl.BlockSpec((1, vd), lambda b, h, s: (0, h))]
        args = [z, z, z, z, z, extra["w2"], extra["ab"].reshape(1, -1), extra["gn"].reshape(1, -1)]
        kern = functools.partial(_gla_kernel, **common)
    if kind != "ret" and _levels(geo["c"]):
        mk = _level_masks(geo["c"])
        in_specs.append(_const_spec(mk.shape))
        args.append(mk)
    in_specs.append(_state_spec(geo, kd, vd))
    args.append(state)
    rows_total = n_seq * seq_len
    y, s_new = pl.pallas_call(
        kern,
        grid=(geo["grid_seq"], heads, geo["grid_slab"]),
        in_specs=in_specs,
        out_specs=[pl.BlockSpec((geo["rows"], vd), lambda b, h, s: (b * geo["grid_slab"] + s, h)),
                   _state_spec(geo, kd, vd)],
        out_shape=[jax.ShapeDtypeStruct((rows_total, heads * vd), F32),
                   jax.ShapeDtypeStruct(state.shape, F32)],
        compiler_params=_cparams(("parallel", "parallel", "arbitrary")),
        name=kind + ("_sample" if geo["per_seq"] else "_prompt"),
    )(*args)
    return y, s_new


def _mlstm_mixer(z, geo, n_seq, seq_len, c0, n0, m0, conv0, w):
    c = geo["c"]
    bias = jnp.zeros((1, LANES), F32).at[0, :H_C].set(w["i_bias"]).at[0, H_C:2 * H_C].set(w["f_bias"])
    nb = geo["nseq_blk"]
    kq = 0
    kk = (H_C * DK_C) // DK_C
    kv = (2 * H_C * DK_C) // DV_C
    ko = kv + H_C
    st = lambda shape: pl.BlockSpec((nb, 1) + shape, lambda b, h, s: (b, h, 0, 0))
    in_specs = [
        _zspec(geo, DK_C, kq), _zspec(geo, DK_C, kk), _zspec(geo, DV_C, kv), _zspec(geo, DV_C, ko),
        pl.BlockSpec((geo["rows"], CD_TAIL), lambda b, h, s: (geo["rowblk"](b, s), CD_MAIN // CD_TAIL)),
        pl.BlockSpec((CONV_W, DK_C), lambda b, h, s: (0, h)),
        pl.BlockSpec((CONV_W, DK_C), lambda b, h, s: (0, H_C + h)),
        _const_spec((1, LANES)),
        pl.BlockSpec((1, DV_C), lambda b, h, s: (0, h)),
        pl.BlockSpec((nb, CONV_W - 1, DK_C), lambda b, h, s: (b, 0, h)),
        pl.BlockSpec((nb, CONV_W - 1, DK_C), lambda b, h, s: (b, 0, H_C + h)),
        st((DK_C, DV_C)), st((1, DK_C)), st((1, LANES)),
    ]
    n4 = n0.reshape(n_seq, H_C, 1, DK_C)
    m4 = jnp.broadcast_to(m0[:, :, None, None], (n_seq, H_C, 1, LANES))
    y, c_new, n_new, m_new = pl.pallas_call(
        functools.partial(_mlstm_kernel, c=c, nsub=geo["nsub"], per_seq=geo["per_seq"]),
        grid=(geo["grid_seq"], H_C, geo["grid_slab"]),
        in_specs=in_specs,
        out_specs=[pl.BlockSpec((geo["rows"], DV_C), lambda b, h, s: (b * geo["grid_slab"] + s, h)),
                   st((DK_C, DV_C)), st((1, DK_C)), st((1, LANES))],
        out_shape=[jax.ShapeDtypeStruct((n_seq * seq_len, H_C * DV_C), F32),
                   jax.ShapeDtypeStruct(c0.shape, F32),
                   jax.ShapeDtypeStruct(n4.shape, F32),
                   jax.ShapeDtypeStruct(m4.shape, F32)],
        scratch_shapes=[pltpu.VMEM((geo["nsub"], SUBLANES + c, DK_C), F32),
                        pltpu.VMEM((geo["nsub"], SUBLANES + c, DK_C), F32),
                        pltpu.VMEM((SUBLANES, DK_C), F32), pltpu.VMEM((SUBLANES, DK_C), F32)],
        compiler_params=_cparams(("parallel", "parallel", "arbitrary")),
        name="mlstm" + ("_sample" if geo["per_seq"] else "_prompt"),
    )(z, z, z, z, z, w["conv_w"], w["conv_w"], bias, w["gn_c"].reshape(1, -1),
      conv0, conv0, c0, n4, m4)
    return y, c_new, n_new.reshape(n_seq, H_C, DK_C), m_new[:, :, 0, 0]


def _rope_tables(pos0, n):
    half = DK_A // 2
    inv = ROPE_BASE ** (-jnp.arange(half, dtype=F32) / half)
    ang = (jnp.arange(n, dtype=F32) + float(pos0))[:, None] * inv[None, :]
    cos, sin = jnp.cos(ang), jnp.sin(ang)
    return jnp.concatenate([cos, cos], axis=1), jnp.concatenate([-sin, sin], axis=1)


def kernel(x_prompt, x_sample, state_ret, state_hgrn, state_mlstm_c, state_mlstm_n, state_mlstm_m,
           state_mlstm_conv, state_gla, norm_gain, ffn_w1, ffn_w3, ffn_w2, ab_w_in, ab_w_out, ab_norm_a,
           ab_norm_b, lb_logits, cd_w_in, cd_w_out, cd_conv_w, cd_i_bias, cd_f_bias, cd_norm_c,
           cd_alpha_w2, cd_alpha_b, cd_norm_d, final_norm):
    bp, tp, _ = x_prompt.shape
    bs, ts, _ = x_sample.shape
    n_p, n_s = bp * tp, bs * ts
    x = jnp.concatenate([x_prompt.reshape(n_p, D_MODEL), x_sample.reshape(n_s, D_MODEL)], axis=0)
    geo_p = _mixer_geometry(bp, tp, 0)
    geo_s = _mixer_geometry(bs, ts, n_p)
    rope_p = _rope_tables(0, tp)
    rope_s = _rope_tables(PAST_LEN, ts)
    outs = {}

    for l in range(DEPTH):
        j = l // 2
        x = _ffn(x, norm_gain[l, 0], ffn_w1[l, 0].astype(BF16), ffn_w3[l, 0].astype(BF16),
                 ffn_w2[l, 0].astype(BF16))
        if l % 2 == 0:
            z = _inproj(x, norm_gain[l, 1], ab_w_in[j].astype(BF16))
            ys = []
            for tag, geo, nseq, slen, rope, s_ret, s_hg in (
                    ("p", geo_p, bp, tp, rope_p, jnp.zeros((bp, H_A, DK_A, DV_A), F32),
                     jnp.zeros((bp, H_B, DK_B, DV_B), F32)),
                    ("s", geo_s, bs, ts, rope_s, state_ret[j], state_hgrn[j])):
                ya, r_new = _decay_mixer("ret", z, geo, nseq, slen, s_ret,
                                         dict(rope=rope, gn=ab_norm_a[j]))
                yb, g_new = _decay_mixer("hgrn", z, geo, nseq, slen, s_hg,
                                         dict(lb_logits=lb_logits, gn=ab_norm_b[j], layer=l))
                ys.append(jnp.concatenate([ya, yb], axis=1))
                outs.setdefault("ret_" + tag, []).append(r_new)
                outs.setdefault("hgrn_" + tag, []).append(g_new)
            x = _outproj(x, jnp.concatenate(ys, axis=0), ab_w_out[j].astype(BF16))
        else:
            w_in = cd_w_in[j]
            gates = 2 * H_C
            main = 2 * H_C * DK_C + 2 * H_C * DV_C
            w_perm = jnp.concatenate(
                [w_in[:, :main], w_in[:, main + gates:main + gates + (CD_MAIN - main)],
                 w_in[:, main:main + gates], w_in[:, main + gates + (CD_MAIN - main):]], axis=1)
            n_pad = -(-(CD_MAIN + CD_TAIL) // TN) * TN
            w_perm = jnp.pad(w_perm, ((0, 0), (0, n_pad - w_perm.shape[1]))).astype(BF16)
            z = _inproj(x, norm_gain[l, 1], w_perm)
            w2p = jnp.zeros((CD_TAIL, H_D * DK_D), F32).at[gates:gates + GLA_RANK].set(cd_alpha_w2[j])
            ys = []
            for tag, geo, nseq, slen, row0, c0, n0, m0, cv0, s_gl in (
                    ("p", geo_p, bp, tp, 0, jnp.zeros((bp, H_C, DK_C, DV_C), F32),
                     jnp.zeros((bp, H_C, DK_C), F32), jnp.zeros((bp, H_C), F32),
                     jnp.zeros((bp, CONV_W - 1, 2 * H_C * DK_C), F32), jnp.zeros((bp, H_D, DK_D, DV_D), F32)),
                    ("s", geo_s, bs, ts, n_p, state_mlstm_c[j], state_mlstm_n[j], state_mlstm_m[j],
                     state_mlstm_conv[j], state_gla[j])):
                yc, c_new, n_new, m_new = _mlstm_mixer(
                    z, geo, nseq, slen, c0, n0, m0, cv0,
                    dict(conv_w=cd_conv_w[j], i_bias=cd_i_bias[j], f_bias=cd_f_bias[j], gn_c=cd_norm_c[j]))
                yd, gl_new = _decay_mixer("gla", z, geo, nseq, slen, s_gl,
                                          dict(w2=w2p, ab=cd_alpha_b[j], gn=cd_norm_d[j]))
                ys.append(jnp.concatenate([yc, yd], axis=1))
                u = z[row0:row0 + nseq * slen, :2 * H_C * DK_C].reshape(nseq, slen, 2 * H_C * DK_C)
                outs.setdefault("c_" + tag, []).append(c_new)
                outs.setdefault("n_" + tag, []).append(n_new)
                outs.setdefault("m_" + tag, []).append(m_new)
                outs.setdefault("conv_" + tag, []).append(u[:, slen - (CONV_W - 1):])
                outs.setdefault("gla_" + tag, []).append(gl_new)
            x = _outproj(x, jnp.concatenate(ys, axis=0), cd_w_out[j].astype(BF16))
        x = _ffn(x, norm_gain[l, 2], ffn_w1[l, 1].astype(BF16), ffn_w3[l, 1].astype(BF16),
                 ffn_w2[l, 1].astype(BF16), final_gain=final_norm if l == DEPTH - 1 else None)

    y_p = x[:n_p].reshape(bp, tp, D_MODEL)
    y_s = x[n_p:].reshape(bs, ts, D_MODEL)
    st = lambda name: jnp.stack(outs[name])
    return (y_p, y_s, st("ret_p"), st("ret_s"), st("hgrn_p"), st("hgrn_s"), st("c_p"), st("c_s"),
            st("n_p"), st("n_s"), st("m_p"), st("m_s"), st("conv_p"), st("conv_s"), st("gla_p"), st("gla_s"))
```

```python
import functools

import numpy as np
import jax
import jax.numpy as jnp
from jax import lax
from jax.experimental import pallas as pl
from jax.experimental.pallas import tpu as pltpu

F32 = jnp.float32
BF16 = jnp.bfloat16

D_MODEL = 2048
DEPTH = 2
PAST_LEN = 16384
H_A, DK_A, DV_A = 4, 128, 256
ROPE_BASE = 10000.0
H_B, DK_B, DV_B = 8, 128, 128
H_C, DK_C, DV_C = 4, 256, 256
CONV_W = 4
H_D, DK_D, DV_D = 4, 128, 256
GLA_RANK = 16
GLA_TAU = 16.0
FF_DIM = 5632
EPS = 1e-6

CD_MAIN = 2 * H_C * DK_C + 2 * H_C * DV_C + 2 * H_D * DK_D + 2 * H_D * DV_D
CD_TAIL = 128
MIX = H_A * DV_A + H_B * DV_B

LANES = 128
SUBLANES = 8
VMEM_LIMIT = 56 * 1024 * 1024

TM = 768
TM_OUT = 512
TF = 512
N_SPLIT = 4
CHUNK = 64
SLAB = 256
SEQ_PER_STEP = 16


def _cparams(sem):
    return pltpu.CompilerParams(dimension_semantics=sem, vmem_limit_bytes=VMEM_LIMIT)


def _rms(x, g):
    return x * lax.rsqrt(jnp.mean(x * x, axis=-1, keepdims=True) + EPS) * g


def _silu(x):
    return x * jax.nn.sigmoid(x)


def _log_sigmoid(x):
    return jnp.minimum(x, 0.0) - jnp.log1p(jnp.exp(-jnp.abs(x)))


def _pieces3(x):
    hi = x.astype(BF16).astype(F32)
    r1 = x - hi
    mid = r1.astype(BF16).astype(F32)
    lo = (r1 - mid).astype(BF16).astype(F32)
    return hi, mid, lo


def _dot(a, b):
    return jnp.dot(a, b, preferred_element_type=F32)


def _dot_nt(a, b):
    return lax.dot_general(a, b, (((1,), (1,)), ((), ())), preferred_element_type=F32)


def _dot_tn(a, b):
    return lax.dot_general(a, b, (((0,), (0,)), ((), ())), preferred_element_type=F32)


def _cat(parts, axis=0):
    return parts[0] if len(parts) == 1 else jnp.concatenate(parts, axis=axis)


def _prefix(g, blk):
    rows, n = g.shape
    row = lax.broadcasted_iota(jnp.int32, (SUBLANES, n), 0)
    outs = []
    off = None
    for r0 in range(0, rows, SUBLANES):
        x = g[r0:r0 + SUBLANES]
        for sh in (1, 2, 4):
            x = x + jnp.where(row >= sh, pltpu.roll(x, sh, 0), 0.0)
        if r0 % blk:
            x = x + off
        off = x[SUBLANES - 1:SUBLANES]
        outs.append(x)
    return _cat(outs)


def _block_last(x, blk):
    rows, n = x.shape
    return _cat([jnp.broadcast_to(x[r0 + blk - 1:r0 + blk], (blk, n)) for r0 in range(0, rows, blk)])


def _block_mask(rows, blk):
    tt = lax.broadcasted_iota(jnp.int32, (rows, rows), 0)
    ss = lax.broadcasted_iota(jnp.int32, (rows, rows), 1)
    mask = ss <= tt
    if blk < rows:
        sh = blk.bit_length() - 1
        mask = jnp.logical_and(mask, lax.shift_right_logical(tt, sh) == lax.shift_right_logical(ss, sh))
    return mask, tt - ss


def _row_bcast(col):
    c = col.shape[0]
    lane = lax.broadcasted_iota(jnp.int32, (c, LANES), 1)
    pieces = _pieces3(jnp.where(lane == 0, jnp.broadcast_to(col, (c, LANES)), 0.0))
    rm = jnp.concatenate(pieces, axis=1).astype(BF16)
    ones = jnp.ones((c, 3 * LANES), BF16)
    return _dot_nt(ones, rm)


def _levels(c):
    out = []
    m = SUBLANES
    while m < c:
        out.append(m)
        m *= 2
    return tuple(out)


def _level_masks(c):
    t = np.arange(c)[:, None]
    u = np.arange(c)[None, :]
    masks = [((t % (2 * m)) >= m) & ((u % (2 * m)) < m) & (t // (2 * m) == u // (2 * m))
             for m in _levels(c)]
    return jnp.asarray(np.stack(masks).astype(np.float32))


def _diag_blocks(q, k, v, b):
    row = lax.broadcasted_iota(jnp.int32, (SUBLANES, 1), 0)
    outs = []
    for r0 in range(0, q.shape[0], SUBLANES):
        qb = q[r0:r0 + SUBLANES]
        bb = b[r0:r0 + SUBLANES]
        ob = jnp.zeros((SUBLANES, v.shape[1]), F32)
        for j in range(SUBLANES):
            r = r0 + j
            e = jnp.exp(jnp.minimum(bb - b[r:r + 1], 0.0))
            a = jnp.sum(qb * k[r:r + 1] * e, axis=-1, keepdims=True)
            ob = ob + jnp.where(row >= j, a, 0.0) * v[r:r + 1]
        outs.append(ob)
    return _cat(outs)


def _state_update(kw, v, e_row, s):
    c, kd = kw.shape
    vd = v.shape[1]
    pad = 2 * SUBLANES
    lhs = jnp.concatenate(_pieces3(e_row) + (jnp.zeros((pad - 3, kd), F32), kw), axis=0)
    sel = (lax.broadcasted_iota(jnp.int32, (pad, LANES), 0) < 3).astype(F32)
    rhs = jnp.concatenate(
        [jnp.concatenate([jnp.zeros((pad, vd), F32), sel], axis=1),
         jnp.concatenate([v, jnp.zeros((c, LANES), F32)], axis=1)], axis=0)
    upd = _dot_tn(lhs.astype(BF16), rhs.astype(BF16))
    e_col = upd[:, vd:]
    if vd > LANES:
        e_col = jnp.concatenate([e_col] * (vd // LANES), axis=1)
    return e_col * s + upd[:, :vd]


def _head_out(o, gn, gate_act):
    return o * lax.rsqrt(jnp.mean(o * o, axis=-1, keepdims=True) + EPS) * gn * gate_act


def _ffn_kernel(*refs, nf, final):
    if final:
        x_ref, g_ref, w1_ref, w3_ref, w2_ref, fg_ref, o_ref, xn_ref, acc_ref = refs
    else:
        x_ref, g_ref, w1_ref, w3_ref, w2_ref, o_ref, xn_ref, acc_ref = refs
    f = pl.program_id(1)

    @pl.when(f == 0)
    def _():
        xn_ref[...] = _rms(x_ref[...], g_ref[...]).astype(BF16)
        acc_ref[...] = jnp.zeros_like(acc_ref)

    xn = xn_ref[...]
    h1 = _dot(xn, w1_ref[...])
    h3 = _dot(xn, w3_ref[...])
    acc_ref[...] += _dot((_silu(h1) * h3).astype(BF16), w2_ref[...])

    @pl.when(f == nf - 1)
    def _():
        y = x_ref[...] + 0.5 * acc_ref[...]
        if final:
            y = _rms(y, fg_ref[...])
        o_ref[...] = y


def _ffn(x, gain, w1, w3, w2, l, idx, final_gain=None):
    m = x.shape[0]
    nf = FF_DIM // TF
    final = final_gain is not None
    in_specs = [
        pl.BlockSpec((TM, D_MODEL), lambda i, f: (i, 0)),
        pl.BlockSpec((1, D_MODEL), lambda i, f: (0, 0)),
        pl.BlockSpec((None, None, D_MODEL, TF), lambda i, f: (l, idx, 0, f)),
        pl.BlockSpec((None, None, D_MODEL, TF), lambda i, f: (l, idx, 0, f)),
        pl.BlockSpec((None, None, TF, D_MODEL), lambda i, f: (l, idx, f, 0)),
    ]
    args = [x, gain.reshape(1, D_MODEL), w1, w3, w2]
    if final:
        in_specs.append(pl.BlockSpec((1, D_MODEL), lambda i, f: (0, 0)))
        args.append(final_gain.reshape(1, D_MODEL))
    return pl.pallas_call(
        functools.partial(_ffn_kernel, nf=nf, final=final),
        grid=(m // TM, nf),
        in_specs=in_specs,
        out_specs=pl.BlockSpec((TM, D_MODEL), lambda i, f: (i, 0)),
        out_shape=jax.ShapeDtypeStruct((m, D_MODEL), F32),
        scratch_shapes=[pltpu.VMEM((TM, D_MODEL), BF16), pltpu.VMEM((TM, D_MODEL), F32)],
        compiler_params=_cparams(("parallel", "arbitrary")),
        name="ffn_final" if final else "ffn",
    )(*args)


def _inproj_kernel(x_ref, g_ref, w_ref, z_ref, xn_ref):
    @pl.when(pl.program_id(1) == 0)
    def _():
        xn_ref[...] = _rms(x_ref[...], g_ref[...]).astype(BF16)

    z_ref[...] = _dot(xn_ref[...], w_ref[...])


def _inproj(x, gain, w):
    m, n = x.shape[0], w.shape[1]
    tn = n // N_SPLIT
    return pl.pallas_call(
        _inproj_kernel,
        grid=(m // TM, N_SPLIT),
        in_specs=[
            pl.BlockSpec((TM, D_MODEL), lambda i, j: (i, 0)),
            pl.BlockSpec((1, D_MODEL), lambda i, j: (0, 0)),
            pl.BlockSpec((D_MODEL, tn), lambda i, j: (0, j)),
        ],
        out_specs=pl.BlockSpec((TM, tn), lambda i, j: (i, j)),
        out_shape=jax.ShapeDtypeStruct((m, n), F32),
        scratch_shapes=[pltpu.VMEM((TM, D_MODEL), BF16)],
        compiler_params=_cparams(("parallel", "arbitrary")),
        name="inproj",
    )(x, gain.reshape(1, D_MODEL), w)


def _outproj_kernel(x_ref, y_ref, w_ref, o_ref):
    o_ref[...] = x_ref[...] + _dot(y_ref[...].astype(BF16), w_ref[...])


def _outproj(x, y, w):
    m, kdim = y.shape
    return pl.pallas_call(
        _outproj_kernel,
        grid=(m // TM_OUT,),
        in_specs=[
            pl.BlockSpec((TM_OUT, D_MODEL), lambda i: (i, 0)),
            pl.BlockSpec((TM_OUT, kdim), lambda i: (i, 0)),
            pl.BlockSpec((kdim, D_MODEL), lambda i: (0, 0)),
        ],
        out_specs=pl.BlockSpec((TM_OUT, D_MODEL), lambda i: (i, 0)),
        out_shape=jax.ShapeDtypeStruct((m, D_MODEL), F32),
        compiler_params=_cparams(("parallel",)),
        name="outproj",
    )(x, y, w)


def _states(per_seq, nblk, ins, outs):
    if per_seq:
        def get(i):
            return tuple(r[i, 0] for r in ins)

        def put(i, vals):
            for o, val in zip(outs, vals):
                o[i, 0] = val
        return get, put

    @pl.when(pl.program_id(2) == 0)
    def _():
        for o, r in zip(outs, ins):
            o[...] = r[...]

    carried = {"v": tuple(o[0, 0] for o in outs)}

    def get(i):
        return carried["v"]

    def put(i, vals):
        carried["v"] = tuple(vals)
        if i == nblk - 1:
            for o, val in zip(outs, vals):
                o[0, 0] = val
    return get, put


def _ret_kernel(q_ref, k_ref, v_ref, gt_ref, cos_ref, sin_ref, gn_ref, s_in, *rest, blk, nblk, per_seq):
    y_ref, s_out = rest[-2:]
    rows = blk * nblk
    hf = jnp.full((1, 1), pl.program_id(1), jnp.int32).astype(F32)
    lg = jnp.log1p(-jnp.exp2(-5.0 - hf))
    cos = _cat([cos_ref[...]] * (rows // cos_ref.shape[0]))
    sin = _cat([sin_ref[...]] * (rows // sin_ref.shape[0]))
    q = q_ref[...]
    k = k_ref[...]
    q = q * cos + pltpu.roll(q, DK_A // 2, 1) * sin
    k = (k * cos + pltpu.roll(k, DK_A // 2, 1) * sin) * (DK_A ** -0.5)
    vb = v_ref[...].astype(BF16)
    mask, dist = _block_mask(rows, blk)
    dec = jnp.where(mask, jnp.exp(jnp.maximum(dist, 0).astype(F32) * lg), 0.0)
    a = _dot_nt(q.astype(BF16), k.astype(BF16)) * dec
    o = _dot(a.astype(BF16), vb)
    tloc = jnp.bitwise_and(lax.broadcasted_iota(jnp.int32, (rows, 1), 0), blk - 1).astype(F32)
    qt = (q * jnp.exp((tloc + 1.0) * lg)).astype(BF16)
    kw = (k * jnp.exp((blk - 1.0 - tloc) * lg)).astype(BF16)
    e_blk = jnp.exp(blk * lg)
    get, put = _states(per_seq, nblk, (s_in,), (s_out,))
    parts = []
    for i in range(nblk):
        r = slice(i * blk, (i + 1) * blk)
        s, = get(i)
        parts.append(o[r] + _dot(qt[r], s.astype(BF16)))
        put(i, (e_blk * s + _dot_tn(kw[r], vb[r]),))
    y_ref[...] = _head_out(_cat(parts), gn_ref[...], _silu(gt_ref[...]))


def _key_decay_slab(q, k, v, g, mk_ref, blk, nblk, get, put):
    kd = q.shape[1]
    b = _prefix(g, blk)
    b_last = _block_last(b, blk)
    qt = (q * jnp.exp(b)).astype(BF16)
    kw = k * jnp.exp(b_last - b)
    e_last = jnp.exp(b_last)
    vb = v.astype(BF16)
    o = _diag_blocks(q, k, v, b)
    lv = []
    for m in _levels(blk):
        mid = _cat([jnp.broadcast_to(b[r0 + m - 1:r0 + m], (2 * m, kd)) for r0 in range(0, b.shape[0], 2 * m)])
        w = jnp.exp(-jnp.abs(b - mid))
        lv.append(((q * w).astype(BF16), (k * w).astype(BF16)))
    parts = []
    for i in range(nblk):
        r = slice(i * blk, (i + 1) * blk)
        s, = get(i)
        oi = o[r] + _dot(qt[r], s.astype(BF16))
        if lv:
            a = jnp.zeros((blk, blk), F32)
            for li, (ql, kl) in enumerate(lv):
                a = a + mk_ref[li] * _dot_nt(ql[r], kl[r])
            oi = oi + _dot(a.astype(BF16), vb[r])
        parts.append(oi)
        put(i, (_state_update(kw[r], v[r], e_last[i * blk:i * blk + 1], s),))
    return _cat(parts)


def _per_chunk_or_slab(run, blk, nblk, per_seq, get, put):
    if per_seq:
        run(slice(0, blk * nblk), nblk, get, put)
    else:
        for i in range(nblk):
            run(slice(i * blk, (i + 1) * blk), 1, lambda _, i=i: get(i), lambda _, vals, i=i: put(i, vals))


def _hgrn_kernel(q_ref, f_ref, i_ref, gt_ref, lb_ref, gn_ref, *rest, blk, nblk, per_seq, layer):
    mk_ref = rest[0] if _levels(blk) else None
    s_in = rest[1] if _levels(blk) else rest[0]
    y_ref, s_out = rest[-2:]
    lbl = lb_ref[...]
    e = jnp.exp(lbl - jnp.max(lbl, axis=0, keepdims=True))
    lb = jnp.sum(e[0:layer + 1], axis=0, keepdims=True) / jnp.sum(e, axis=0, keepdims=True)
    get, put = _states(per_seq, nblk, (s_in,), (s_out,))

    def run(r, n, get, put):
        fb = f_ref[r, :]
        f_gate = lb + (1.0 - lb) * jax.nn.sigmoid(fb)
        k = (1.0 - lb) * jax.nn.sigmoid(-fb)
        o = _key_decay_slab(_silu(q_ref[r, :]), k, i_ref[r, :], jnp.log(f_gate), mk_ref, blk, n, get, put)
        y_ref[r, :] = _head_out(o, gn_ref[...], _silu(gt_ref[r, :]))

    _per_chunk_or_slab(run, blk, nblk, per_seq, get, put)


def _gla_kernel(q_ref, k_ref, v_ref, gt_ref, tail_ref, w2_ref, ab_ref, gn_ref, *rest, blk, nblk, per_seq):
    mk_ref = rest[0] if _levels(blk) else None
    s_in = rest[1] if _levels(blk) else rest[0]
    y_ref, s_out = rest[-2:]
    wh, wm, wl = (p.astype(BF16) for p in _pieces3(w2_ref[...]))
    get, put = _states(per_seq, nblk, (s_in,), (s_out,))

    def run(r, n, get, put):
        th, tm_, tl = (p.astype(BF16) for p in _pieces3(tail_ref[r, :]))
        x = (_dot(th, wh) + (_dot(th, wm) + _dot(tm_, wh))
             + (_dot(th, wl) + _dot(tm_, wm) + _dot(tl, wh)))
        g = _log_sigmoid(x + ab_ref[...]) / GLA_TAU
        o = _key_decay_slab(q_ref[r, :] * (DK_D ** -0.5), k_ref[r, :], v_ref[r, :], g, mk_ref, blk, n, get, put)
        y_ref[r, :] = _head_out(o, gn_ref[...], _silu(gt_ref[r, :]))

    _per_chunk_or_slab(run, blk, nblk, per_seq, get, put)


def _mlstm_kernel(q_ref, k_ref, v_ref, og_ref, tail_ref, wq_ref, wk_ref, bias_ref, gn_ref,
                  cq_in, ck_in, c_in, n_in, m_in, *rest, blk, nblk, per_seq):
    y_ref, c_out, n_out, m_out, cq_out, ck_out, uq_ref, uk_ref, hq_ref, hk_ref = rest[-10:]
    assert per_seq or nblk == 1
    rows = blk * nblk
    h = pl.program_id(1)
    hist = CONV_W - 1
    lo = SUBLANES - hist

    if not per_seq:
        @pl.when(pl.program_id(2) == 0)
        def _():
            for h_ref, cin in ((hq_ref, cq_in), (hk_ref, ck_in)):
                h_ref[...] = jnp.zeros_like(h_ref)
                h_ref[lo:SUBLANES, :] = cin[0]

    def conv(u_ref, h_ref, cin, cout, raw_ref, w_ref):
        parts = []
        for i in range(nblk):
            raw = raw_ref[i * blk:(i + 1) * blk, :]
            if per_seq:
                u_ref[i, lo:SUBLANES, :] = cin[i]
            else:
                u_ref[i, 0:SUBLANES, :] = h_ref[...]
                h_ref[...] = raw[blk - SUBLANES:blk]
            u_ref[i, SUBLANES:SUBLANES + blk, :] = raw
            cout[i] = raw[blk - hist:blk]
            acc = u_ref[i, lo:lo + blk, :] * w_ref[0:1, :]
            for j in range(1, CONV_W):
                acc = acc + u_ref[i, lo + j:lo + j + blk, :] * w_ref[j:j + 1, :]
            parts.append(acc)
        return _silu(_cat(parts))

    q = conv(uq_ref, hq_ref, cq_in, cq_out, q_ref, wq_ref)
    k = conv(uk_ref, hk_ref, ck_in, ck_out, k_ref, wk_ref) * (DK_C ** -0.5)
    qb = q.astype(BF16)
    vb = v_ref[...].astype(BF16)

    lane = lax.broadcasted_iota(jnp.int32, (rows, LANES), 1)
    tl = tail_ref[...] + bias_ref[...]
    li = jnp.sum(jnp.where(lane == h, tl, 0.0), axis=-1, keepdims=True)
    lf = jnp.sum(jnp.where(lane == H_C + h, _log_sigmoid(tl), 0.0), axis=-1, keepdims=True)
    b = _prefix(jnp.broadcast_to(lf, (rows, LANES)), blk)[:, :1]

    get, put = _states(per_seq, nblk, (c_in, n_in, m_in), (c_out, n_out, m_out))
    states = [get(i) for i in range(nblk)] if per_seq else [get(0)]
    m_prev = _cat([jnp.broadcast_to(st[2][:, :1], (blk, 1)) for st in states])

    mask, _ = _block_mask(rows, blk)
    d = jnp.where(mask, b + _row_bcast(li - b), -jnp.inf)
    m_inter = b + m_prev
    m_t = jnp.maximum(m_inter, jnp.max(d, axis=-1, keepdims=True))
    w_inter = jnp.exp(m_inter - m_t)
    sc = _dot_nt(qb, k.astype(BF16)) * jnp.exp(d - m_t)
    num = _dot(sc.astype(BF16), vb)
    den = jnp.sum(sc, axis=-1, keepdims=True)
    m_new = _block_last(m_t, blk)
    b_last = _block_last(b, blk)
    kw = k * jnp.exp(b_last - b + li - m_new)
    w_c = jnp.exp(b_last + m_prev - m_new)
    nums, dens = [], []
    for i in range(nblk):
        r = slice(i * blk, (i + 1) * blk)
        cst, nrow, _ = states[i]
        nums.append(num[r] + w_inter[r] * _dot(qb[r], cst.astype(BF16)))
        dens.append(den[r] + w_inter[r] * jnp.sum(q[r] * nrow, axis=-1, keepdims=True))
        wc = w_c[i * blk:i * blk + 1]
        put(i, (wc * cst + _dot_tn(kw[r].astype(BF16), vb[r]),
                wc * nrow + jnp.sum(kw[r], axis=0, keepdims=True),
                jnp.broadcast_to(m_new[i * blk:i * blk + 1], (1, LANES))))
    hh = _cat(nums) / jnp.maximum(jnp.abs(_cat(dens)), jnp.exp(-m_t))
    y_ref[...] = _head_out(hh, gn_ref[...], jax.nn.sigmoid(og_ref[...]))


def _geometry(n_seq, seq_len, row0, blk_prompt):
    if seq_len >= SLAB:
        blk, rows, per_seq, nseq_blk = blk_prompt, SLAB, False, 1
        grid_seq, grid_slab = n_seq, seq_len // SLAB
    else:
        blk, nseq_blk, per_seq = seq_len, SEQ_PER_STEP, True
        rows = nseq_blk * blk
        grid_seq, grid_slab = n_seq // nseq_blk, 1
    base = row0 // rows
    return dict(blk=blk, rows=rows, nblk=rows // blk, per_seq=per_seq, nseq_blk=nseq_blk,
                grid_seq=grid_seq, grid_slab=grid_slab, rowblk=lambda b, s: base + b * grid_slab + s)


def _zspec(geo, width, col):
    rb = geo["rowblk"]
    return pl.BlockSpec((geo["rows"], width), lambda b, h, s: (rb(b, s), col + h))


def _const_spec(shape):
    return pl.BlockSpec(shape, lambda b, h, s: (0,) * len(shape))


def _head_spec(width, col=0):
    return pl.BlockSpec((1, width), lambda b, h, s: (0, col + h))


def _state_spec(geo, shape):
    return pl.BlockSpec((geo["nseq_blk"], 1) + shape, lambda b, h, s: (b, h, 0, 0))


def _mixer_call(kern, geo, heads, vd, y_col, y_prev, n_tok, in_specs, args, extra_specs, extra_shapes,
                scratch, name):
    if y_prev is not None:
        in_specs = in_specs + [pl.BlockSpec(memory_space=pl.ANY)]
        args = args + [y_prev]
    rb = geo["rowblk"]
    return pl.pallas_call(
        kern,
        grid=(geo["grid_seq"], heads, geo["grid_slab"]),
        in_specs=in_specs,
        out_specs=[pl.BlockSpec((geo["rows"], vd), lambda b, h, s: (rb(b, s), y_col // vd + h))] + extra_specs,
        out_shape=[jax.ShapeDtypeStruct((n_tok, MIX), F32)] + extra_shapes,
        input_output_aliases={} if y_prev is None else {len(args) - 1: 0},
        scratch_shapes=scratch,
        compiler_params=_cparams(("parallel", "parallel", "arbitrary")),
        name=name + ("_sample" if geo["per_seq"] else "_prompt"),
    )(*args)


def _decay_mixer(kind, z, geo, state, extra, y_prev, y_col):
    heads, kd, vd = {"ret": (H_A, DK_A, DV_A), "hgrn": (H_B, DK_B, DV_B), "gla": (H_D, DK_D, DV_D)}[kind]
    common = dict(blk=geo["blk"], nblk=geo["nblk"], per_seq=geo["per_seq"])
    kb, vb = LANES, vd
    if kind == "ret":
        cos, sin = extra["rope"]
        tab = pl.BlockSpec((geo["blk"] if geo["per_seq"] else geo["rows"], DK_A),
                           lambda b, h, s: (0 if geo["per_seq"] else s, 0))
        in_specs = [_zspec(geo, kb, 0), _zspec(geo, kb, H_A), _zspec(geo, vb, (2 * H_A * DK_A) // vb),
                    _zspec(geo, vb, (2 * H_A * DK_A) // vb + H_A), tab, tab, _head_spec(vd)]
        args = [z, z, z, z, cos, sin, extra["gn"].reshape(1, -1)]
        kern = functools.partial(_ret_kernel, **common)
    elif kind == "hgrn":
        off = (2 * H_A * DK_A + 2 * H_A * DV_A) // LANES
        in_specs = [_zspec(geo, kb, off), _zspec(geo, kb, off + H_B), _zspec(geo, kb, off + 2 * H_B),
                    _zspec(geo, kb, off + 3 * H_B),
                    pl.BlockSpec((DEPTH + 1, DK_B), lambda b, h, s: (0, h)), _head_spec(vd)]
        args = [z, z, z, z, extra["lb_logits"], extra["gn"].reshape(1, -1)]
        kern = functools.partial(_hgrn_kernel, layer=extra["layer"], **common)
    else:
        off = (2 * H_C * DK_C + 2 * H_C * DV_C) // LANES
        voff = (2 * H_C * DK_C + 2 * H_C * DV_C + 2 * H_D * DK_D) // vb
        in_specs = [_zspec(geo, kb, off), _zspec(geo, kb, off + H_D), _zspec(geo, vb, voff),
                    _zspec(geo, vb, voff + H_D),
                    pl.BlockSpec((geo["rows"], CD_TAIL), lambda b, h, s: (geo["rowblk"](b, s), CD_MAIN // CD_TAIL)),
                    pl.BlockSpec((CD_TAIL, DK_D), lambda b, h, s: (0, h)),
                    _head_spec(DK_D), _head_spec(vd)]
        args = [z, z, z, z, z, extra["w2"], extra["ab"].reshape(1, -1), extra["gn"].reshape(1, -1)]
        kern = functools.partial(_gla_kernel, **common)
    if kind != "ret" and _levels(geo["blk"]):
        mk = _level_masks(geo["blk"])
        in_specs.append(_const_spec(mk.shape))
        args.append(mk)
    in_specs.append(_state_spec(geo, (kd, vd)))
    args.append(state)
    return _mixer_call(kern, geo, heads, vd, y_col, y_prev, z.shape[0], in_specs, args,
                       [_state_spec(geo, (kd, vd))], [jax.ShapeDtypeStruct(state.shape, F32)], [], kind)


def _mlstm_mixer(z, geo, c0, n0, m0, conv0, w, y_prev, y_col):
    n_seq = c0.shape[0]
    blk, nb = geo["blk"], geo["nseq_blk"]
    bias = jnp.zeros((1, LANES), F32).at[0, :H_C].set(w["i_bias"]).at[0, H_C:2 * H_C].set(w["f_bias"])
    kk = (H_C * DK_C) // DK_C
    kv = (2 * H_C * DK_C) // DV_C
    conv_spec = lambda col: pl.BlockSpec((nb, CONV_W - 1, DK_C), lambda b, h, s: (b, 0, col + h))
    in_specs = [
        _zspec(geo, DK_C, 0), _zspec(geo, DK_C, kk), _zspec(geo, DV_C, kv), _zspec(geo, DV_C, kv + H_C),
        pl.BlockSpec((geo["rows"], CD_TAIL), lambda b, h, s: (geo["rowblk"](b, s), CD_MAIN // CD_TAIL)),
        pl.BlockSpec((CONV_W, DK_C), lambda b, h, s: (0, h)),
        pl.BlockSpec((CONV_W, DK_C), lambda b, h, s: (0, H_C + h)),
        _const_spec((1, LANES)), _head_spec(DV_C),
        conv_spec(0), conv_spec(H_C),
        _state_spec(geo, (DK_C, DV_C)), _state_spec(geo, (1, DK_C)), _state_spec(geo, (1, LANES)),
    ]
    n4 = n0.reshape(n_seq, H_C, 1, DK_C)
    m4 = jnp.broadcast_to(m0[:, :, None, None], (n_seq, H_C, 1, LANES))
    args = [z, z, z, z, z, w["conv_w"], w["conv_w"], bias, w["gn_c"].reshape(1, -1),
            conv0, conv0, c0, n4, m4]
    conv_shape = jax.ShapeDtypeStruct((n_seq, CONV_W - 1, H_C * DK_C), F32)
    y, c_new, n_new, m_new, cq, ck = _mixer_call(
        functools.partial(_mlstm_kernel, blk=blk, nblk=geo["nblk"], per_seq=geo["per_seq"]),
        geo, H_C, DV_C, y_col, y_prev, z.shape[0], in_specs, args,
        [_state_spec(geo, (DK_C, DV_C)), _state_spec(geo, (1, DK_C)), _state_spec(geo, (1, LANES)),
         conv_spec(0), conv_spec(0)],
        [jax.ShapeDtypeStruct(c0.shape, F32), jax.ShapeDtypeStruct(n4.shape, F32),
         jax.ShapeDtypeStruct(m4.shape, F32), conv_shape, conv_shape],
        [pltpu.VMEM((geo["nblk"], SUBLANES + blk, DK_C), F32), pltpu.VMEM((geo["nblk"], SUBLANES + blk, DK_C), F32),
         pltpu.VMEM((SUBLANES, DK_C), F32), pltpu.VMEM((SUBLANES, DK_C), F32)],
        "mlstm")
    return (y, c_new, n_new.reshape(n_seq, H_C, DK_C), m_new[:, :, 0, 0],
            jnp.concatenate([cq, ck], axis=-1))


def _rope_tables(pos0, n):
    half = DK_A // 2
    inv = ROPE_BASE ** (-jnp.arange(half, dtype=F32) / half)
    ang = (jnp.arange(n, dtype=F32) + float(pos0))[:, None] * inv[None, :]
    cos, sin = jnp.cos(ang), jnp.sin(ang)
    return jnp.concatenate([cos, cos], axis=1), jnp.concatenate([-sin, sin], axis=1)


def kernel(x_prompt, x_sample, state_ret, state_hgrn, state_mlstm_c, state_mlstm_n, state_mlstm_m,
           state_mlstm_conv, state_gla, norm_gain, ffn_w1, ffn_w3, ffn_w2, ab_w_in, ab_w_out, ab_norm_a,
           ab_norm_b, lb_logits, cd_w_in, cd_w_out, cd_conv_w, cd_i_bias, cd_f_bias, cd_norm_c,
           cd_alpha_w2, cd_alpha_b, cd_norm_d, final_norm):
    bp, tp, _ = x_prompt.shape
    bs, ts, _ = x_sample.shape
    n_p, n_s = bp * tp, bs * ts
    x = jnp.concatenate([x_prompt.reshape(n_p, D_MODEL), x_sample.reshape(n_s, D_MODEL)], axis=0)
    w1, w3, w2 = ffn_w1.astype(BF16), ffn_w3.astype(BF16), ffn_w2.astype(BF16)
    groups = (("p", bp, _geometry(bp, tp, 0, SLAB), _geometry(bp, tp, 0, CHUNK)),
              ("s", bs, _geometry(bs, ts, n_p, SLAB), _geometry(bs, ts, n_p, CHUNK)))
    rope = {"p": _rope_tables(0, tp), "s": _rope_tables(PAST_LEN, ts)}
    zeros = lambda *shape: jnp.zeros(shape, F32)
    outs = {}

    def keep(name, tag, val):
        outs.setdefault(name + "_" + tag, []).append(val)

    for l in range(DEPTH):
        j = l // 2
        x = _ffn(x, norm_gain[l, 0], w1, w3, w2, l, 0)
        y = None
        if l % 2 == 0:
            z = _inproj(x, norm_gain[l, 1], ab_w_in[j].astype(BF16))
            for tag, nseq, geo_h, geo_k in groups:
                s_ret = zeros(nseq, H_A, DK_A, DV_A) if tag == "p" else state_ret[j]
                s_hg = zeros(nseq, H_B, DK_B, DV_B) if tag == "p" else state_hgrn[j]
                y, r_new = _decay_mixer("ret", z, geo_h, s_ret, dict(rope=rope[tag], gn=ab_norm_a[j]), y, 0)
                y, g_new = _decay_mixer("hgrn", z, geo_k, s_hg,
                                        dict(lb_logits=lb_logits, gn=ab_norm_b[j], layer=l), y, H_A * DV_A)
                keep("ret", tag, r_new)
                keep("hgrn", tag, g_new)
            x = _outproj(x, y, ab_w_out[j].astype(BF16))
        else:
            w_in = cd_w_in[j]
            gates = 2 * H_C
            main = 2 * H_C * DK_C + 2 * H_C * DV_C
            rest = CD_MAIN - main
            n_pad = -(-(CD_MAIN + CD_TAIL) // (N_SPLIT * LANES)) * (N_SPLIT * LANES)
            w_perm = jnp.concatenate(
                [w_in[:, :main].astype(BF16), w_in[:, main + gates:main + gates + rest].astype(BF16),
                 w_in[:, main:main + gates].astype(BF16), w_in[:, main + gates + rest:].astype(BF16),
                 jnp.zeros((D_MODEL, n_pad - w_in.shape[1]), BF16)], axis=1)
            z = _inproj(x, norm_gain[l, 1], w_perm)
            w2p = jnp.zeros((CD_TAIL, H_D * DK_D), F32).at[gates:gates + GLA_RANK].set(cd_alpha_w2[j])
            wts = dict(conv_w=cd_conv_w[j], i_bias=cd_i_bias[j], f_bias=cd_f_bias[j], gn_c=cd_norm_c[j])
            for tag, nseq, geo_h, geo_k in groups:
                if tag == "p":
                    c0, n0, m0 = zeros(nseq, H_C, DK_C, DV_C), zeros(nseq, H_C, DK_C), zeros(nseq, H_C)
                    cv0, s_gl = zeros(nseq, CONV_W - 1, 2 * H_C * DK_C), zeros(nseq, H_D, DK_D, DV_D)
                else:
                    c0, n0, m0 = state_mlstm_c[j], state_mlstm_n[j], state_mlstm_m[j]
                    cv0, s_gl = state_mlstm_conv[j], state_gla[j]
                y, c_new, n_new, m_new, cv_new = _mlstm_mixer(z, geo_h, c0, n0, m0, cv0, wts, y, 0)
                y, gl_new = _decay_mixer("gla", z, geo_k, s_gl,
                                         dict(w2=w2p, ab=cd_alpha_b[j], gn=cd_norm_d[j]), y, H_C * DV_C)
                for name, val in (("c", c_new), ("n", n_new), ("m", m_new), ("conv", cv_new), ("gla", gl_new)):
                    keep(name, tag, val)
            x = _outproj(x, y, cd_w_out[j].astype(BF16))
        x = _ffn(x, norm_gain[l, 2], w1, w3, w2, l, 1, final_gain=final_norm if l == DEPTH - 1 else None)

    y_p = x[:n_p].reshape(bp, tp, D_MODEL)
    y_s = x[n_p:].reshape(bs, ts, D_MODEL)
    st = lambda name: jnp.stack(outs[name])
    return (y_p, y_s, st("ret_p"), st("ret_s"), st("hgrn_p"), st("hgrn_s"), st("c_p"), st("c_s"),
            st("n_p"), st("n_s"), st("m_p"), st("m_s"), st("conv_p"), st("conv_s"), st("gla_p"), st("gla_s"))
```

```python
import functools

import numpy as np
import jax
import jax.numpy as jnp
from jax import lax
from jax.experimental import pallas as pl
from jax.experimental.pallas import tpu as pltpu

F32 = jnp.float32
BF16 = jnp.bfloat16

D_MODEL = 2048
DEPTH = 2
PAST_LEN = 16384
H_A, DK_A, DV_A = 4, 128, 256
ROPE_BASE = 10000.0
H_B, DK_B, DV_B = 8, 128, 128
H_C, DK_C, DV_C = 4, 256, 256
CONV_W = 4
H_D, DK_D, DV_D = 4, 128, 256
GLA_RANK = 16
GLA_TAU = 16.0
FF_DIM = 5632
EPS = 1e-6

CD_MAIN = 2 * H_C * DK_C + 2 * H_C * DV_C + 2 * H_D * DK_D + 2 * H_D * DV_D
CD_TAIL = 128
MIX = H_A * DV_A + H_B * DV_B

LANES = 128
SUBLANES = 8
VMEM_LIMIT = 56 * 1024 * 1024

TM = 768
TM_OUT = 512
TF = 512
N_SPLIT = 4
CHUNK = 256
SLAB = 256
SEQ_PER_STEP = 16
PAIR_TILE = 128
LOG2E = 1.4426950408889634


def _cparams(sem):
    return pltpu.CompilerParams(dimension_semantics=sem, vmem_limit_bytes=VMEM_LIMIT)


def _rms(x, g):
    return x * lax.rsqrt(jnp.mean(x * x, axis=-1, keepdims=True) + EPS) * g


def _silu(x):
    return x * jax.nn.sigmoid(x)


def _log_sigmoid(x):
    return jnp.minimum(x, 0.0) - jnp.log1p(jnp.exp(-jnp.abs(x)))


def _pieces3(x):
    hi = x.astype(BF16).astype(F32)
    r1 = x - hi
    mid = r1.astype(BF16).astype(F32)
    lo = (r1 - mid).astype(BF16).astype(F32)
    return hi, mid, lo


def _dot(a, b):
    return jnp.dot(a, b, preferred_element_type=F32)


def _dot_nt(a, b):
    return lax.dot_general(a, b, (((1,), (1,)), ((), ())), preferred_element_type=F32)


def _dot_tn(a, b):
    return lax.dot_general(a, b, (((0,), (0,)), ((), ())), preferred_element_type=F32)


def _cat(parts, axis=0):
    return parts[0] if len(parts) == 1 else jnp.concatenate(parts, axis=axis)


def _prefix(g, blk):
    rows, n = g.shape
    row = lax.broadcasted_iota(jnp.int32, (SUBLANES, n), 0)
    outs = []
    off = None
    for r0 in range(0, rows, SUBLANES):
        x = g[r0:r0 + SUBLANES]
        for sh in (1, 2, 4):
            x = x + jnp.where(row >= sh, pltpu.roll(x, sh, 0), 0.0)
        if r0 % blk:
            x = x + off
        off = x[SUBLANES - 1:SUBLANES]
        outs.append(x)
    return _cat(outs)


def _block_last(x, blk):
    rows, n = x.shape
    return _cat([jnp.broadcast_to(x[r0 + blk - 1:r0 + blk], (blk, n)) for r0 in range(0, rows, blk)])


def _block_mask(rows, blk):
    tt = lax.broadcasted_iota(jnp.int32, (rows, rows), 0)
    ss = lax.broadcasted_iota(jnp.int32, (rows, rows), 1)
    mask = ss <= tt
    if blk < rows:
        sh = blk.bit_length() - 1
        mask = jnp.logical_and(mask, lax.shift_right_logical(tt, sh) == lax.shift_right_logical(ss, sh))
    return mask, tt - ss


def _row_bcast(col):
    c = col.shape[0]
    lane = lax.broadcasted_iota(jnp.int32, (c, LANES), 1)
    pieces = _pieces3(jnp.where(lane == 0, jnp.broadcast_to(col, (c, LANES)), 0.0))
    rm = jnp.concatenate(pieces, axis=1).astype(BF16)
    ones = jnp.ones((c, 3 * LANES), BF16)
    return _dot_nt(ones, rm)


def _levels(blk):
    out = []
    m = 1
    while m < blk:
        out.append(m)
        m *= 2
    return tuple(out)


def _pair_masks(rows, blk):
    t = np.arange(rows)[:, None]
    u = np.arange(rows)[None, :]
    masks = [t == u] + [((t % (2 * m)) >= m) & ((u % (2 * m)) < m) & (t // (2 * m) == u // (2 * m))
                        for m in _levels(blk)]
    return jnp.asarray(np.stack(masks).astype(np.float32))


def _level_exponent(b, g, m):
    rows, n = b.shape
    if m >= SUBLANES:
        mid = _cat([jnp.broadcast_to(b[r0 + m - 1:r0 + m], (2 * m, n)) for r0 in range(0, rows, 2 * m)])
        return -jnp.abs(b - mid)
    row = lax.broadcasted_iota(jnp.int32, (SUBLANES, n), 0)
    outs = []
    for r0 in range(0, rows, SUBLANES):
        x = b[r0:r0 + SUBLANES]
        if m == 1:
            outs.append(jnp.where(jnp.bitwise_and(row, 1) == 1, g[r0:r0 + SUBLANES], 0.0))
            continue
        if m == 2:
            mid = jnp.where(row < 4, jnp.broadcast_to(x[1:2], x.shape), jnp.broadcast_to(x[5:6], x.shape))
        else:
            mid = jnp.broadcast_to(x[3:4], x.shape)
        outs.append(-jnp.abs(x - mid))
    return _cat(outs)


def _state_update(kw, v, e_row, s):
    c, kd = kw.shape
    vd = v.shape[1]
    pad = 2 * SUBLANES
    lhs = jnp.concatenate(_pieces3(e_row) + (jnp.zeros((pad - 3, kd), F32), kw), axis=0)
    sel = (lax.broadcasted_iota(jnp.int32, (pad, LANES), 0) < 3).astype(F32)
    rhs = jnp.concatenate(
        [jnp.concatenate([jnp.zeros((pad, vd), F32), sel], axis=1),
         jnp.concatenate([v, jnp.zeros((c, LANES), F32)], axis=1)], axis=0)
    upd = _dot_tn(lhs.astype(BF16), rhs.astype(BF16))
    e_col = upd[:, vd:]
    if vd > LANES:
        e_col = jnp.concatenate([e_col] * (vd // LANES), axis=1)
    return e_col * s + upd[:, :vd]


def _head_out(o, gn, gate_act):
    return o * lax.rsqrt(jnp.mean(o * o, axis=-1, keepdims=True) + EPS) * gn * gate_act


def _ffn_kernel(*refs, nf, final):
    if final:
        x_ref, g_ref, w1_ref, w3_ref, w2_ref, fg_ref, o_ref, xn_ref, acc_ref = refs
    else:
        x_ref, g_ref, w1_ref, w3_ref, w2_ref, o_ref, xn_ref, acc_ref = refs
    f = pl.program_id(1)

    @pl.when(f == 0)
    def _():
        xn_ref[...] = _rms(x_ref[...], g_ref[...]).astype(BF16)
        acc_ref[...] = jnp.zeros_like(acc_ref)

    xn = xn_ref[...]
    h1 = _dot(xn, w1_ref[...])
    h3 = _dot(xn, w3_ref[...])
    acc_ref[...] += _dot((_silu(h1) * h3).astype(BF16), w2_ref[...])

    @pl.when(f == nf - 1)
    def _():
        y = x_ref[...] + 0.5 * acc_ref[...]
        if final:
            y = _rms(y, fg_ref[...])
        o_ref[...] = y


def _ffn(x, gain, w1, w3, w2, l, idx, final_gain=None):
    m = x.shape[0]
    nf = FF_DIM // TF
    final = final_gain is not None
    in_specs = [
        pl.BlockSpec((TM, D_MODEL), lambda i, f: (i, 0)),
        pl.BlockSpec((1, D_MODEL), lambda i, f: (0, 0)),
        pl.BlockSpec((None, None, D_MODEL, TF), lambda i, f: (l, idx, 0, f)),
        pl.BlockSpec((None, None, D_MODEL, TF), lambda i, f: (l, idx, 0, f)),
        pl.BlockSpec((None, None, TF, D_MODEL), lambda i, f: (l, idx, f, 0)),
    ]
    args = [x, gain.reshape(1, D_MODEL), w1, w3, w2]
    if final:
        in_specs.append(pl.BlockSpec((1, D_MODEL), lambda i, f: (0, 0)))
        args.append(final_gain.reshape(1, D_MODEL))
    return pl.pallas_call(
        functools.partial(_ffn_kernel, nf=nf, final=final),
        grid=(m // TM, nf),
        in_specs=in_specs,
        out_specs=pl.BlockSpec((TM, D_MODEL), lambda i, f: (i, 0)),
        out_shape=jax.ShapeDtypeStruct((m, D_MODEL), F32),
        scratch_shapes=[pltpu.VMEM((TM, D_MODEL), BF16), pltpu.VMEM((TM, D_MODEL), F32)],
        compiler_params=_cparams(("parallel", "arbitrary")),
        name="ffn_final" if final else "ffn",
    )(*args)


def _inproj_kernel(x_ref, g_ref, w_ref, z_ref, xn_ref):
    @pl.when(pl.program_id(1) == 0)
    def _():
        xn_ref[...] = _rms(x_ref[...], g_ref[...]).astype(BF16)

    z_ref[...] = _dot(xn_ref[...], w_ref[...])


def _inproj(x, gain, w):
    m, n = x.shape[0], w.shape[1]
    tn = n // N_SPLIT
    return pl.pallas_call(
        _inproj_kernel,
        grid=(m // TM, N_SPLIT),
        in_specs=[
            pl.BlockSpec((TM, D_MODEL), lambda i, j: (i, 0)),
            pl.BlockSpec((1, D_MODEL), lambda i, j: (0, 0)),
            pl.BlockSpec((D_MODEL, tn), lambda i, j: (0, j)),
        ],
        out_specs=pl.BlockSpec((TM, tn), lambda i, j: (i, j)),
        out_shape=jax.ShapeDtypeStruct((m, n), F32),
        scratch_shapes=[pltpu.VMEM((TM, D_MODEL), BF16)],
        compiler_params=_cparams(("parallel", "arbitrary")),
        name="inproj",
    )(x, gain.reshape(1, D_MODEL), w)


def _outproj_kernel(x_ref, y_ref, w_ref, o_ref):
    o_ref[...] = x_ref[...] + _dot(y_ref[...].astype(BF16), w_ref[...])


def _outproj(x, y, w):
    m, kdim = y.shape
    return pl.pallas_call(
        _outproj_kernel,
        grid=(m // TM_OUT,),
        in_specs=[
            pl.BlockSpec((TM_OUT, D_MODEL), lambda i: (i, 0)),
            pl.BlockSpec((TM_OUT, kdim), lambda i: (i, 0)),
            pl.BlockSpec((kdim, D_MODEL), lambda i: (0, 0)),
        ],
        out_specs=pl.BlockSpec((TM_OUT, D_MODEL), lambda i: (i, 0)),
        out_shape=jax.ShapeDtypeStruct((m, D_MODEL), F32),
        compiler_params=_cparams(("parallel",)),
        name="outproj",
    )(x, y, w)


def _states(per_seq, nblk, ins, outs):
    if per_seq:
        def get(i):
            return tuple(r[i, 0] for r in ins)

        def put(i, vals):
            for o, val in zip(outs, vals):
                o[i, 0] = val
        return get, put

    @pl.when(pl.program_id(2) == 0)
    def _():
        for o, r in zip(outs, ins):
            o[...] = r[...]

    carried = {"v": tuple(o[0, 0] for o in outs)}

    def get(i):
        return carried["v"]

    def put(i, vals):
        carried["v"] = tuple(vals)
        if i == nblk - 1:
            for o, val in zip(outs, vals):
                o[0, 0] = val
    return get, put


def _ret_kernel(q_ref, k_ref, v_ref, gt_ref, cos_ref, sin_ref, gn_ref, s_in, *rest, blk, nblk, per_seq):
    y_ref, s_out = rest[-2:]
    rows = blk * nblk
    hf = jnp.full((1, 1), pl.program_id(1), jnp.int32).astype(F32)
    lg = jnp.log1p(-jnp.exp2(-5.0 - hf))
    cos = _cat([cos_ref[...]] * (rows // cos_ref.shape[0]))
    sin = _cat([sin_ref[...]] * (rows // sin_ref.shape[0]))
    q = q_ref[...]
    k = k_ref[...]
    q = q * cos + pltpu.roll(q, DK_A // 2, 1) * sin
    k = (k * cos + pltpu.roll(k, DK_A // 2, 1) * sin) * (DK_A ** -0.5)
    vb = v_ref[...].astype(BF16)
    mask, dist = _block_mask(rows, blk)
    dec = jnp.where(mask, jnp.exp(jnp.maximum(dist, 0).astype(F32) * lg), 0.0)
    a = _dot_nt(q.astype(BF16), k.astype(BF16)) * dec
    o = _dot(a.astype(BF16), vb)
    tloc = jnp.bitwise_and(lax.broadcasted_iota(jnp.int32, (rows, 1), 0), blk - 1).astype(F32)
    qt = (q * jnp.exp((tloc + 1.0) * lg)).astype(BF16)
    kw = (k * jnp.exp((blk - 1.0 - tloc) * lg)).astype(BF16)
    e_blk = jnp.exp(blk * lg)
    get, put = _states(per_seq, nblk, (s_in,), (s_out,))
    parts = []
    for i in range(nblk):
        r = slice(i * blk, (i + 1) * blk)
        s, = get(i)
        parts.append(o[r] + _dot(qt[r], s.astype(BF16)))
        put(i, (e_blk * s + _dot_tn(kw[r], vb[r]),))
    y_ref[...] = _head_out(_cat(parts), gn_ref[...], _silu(gt_ref[...]))


def _key_decay_slab(q, k, v, g, mk_ref, blk, nblk, get, put):
    rows = q.shape[0]
    tile = min(rows, PAIR_TILE)
    tiles = [slice(i * tile, (i + 1) * tile) for i in range(rows // tile)]
    g2 = g * LOG2E
    b = _prefix(g2, blk)
    b_last = _block_last(b, blk)
    qt = (q * jnp.exp2(b)).astype(BF16)
    kw = k * jnp.exp2(b_last - b)
    e_last = jnp.exp2(b_last)
    vb = v.astype(BF16)
    qb, kb = q.astype(BF16), k.astype(BF16)
    a = {(i, i): mk_ref[0] * _dot_nt(qb[t], kb[t]) for i, t in enumerate(tiles)}
    for li, m in enumerate(_levels(blk)):
        w = jnp.exp2(_level_exponent(b, g2, m))
        ql, kl = (q * w).astype(BF16), (k * w).astype(BF16)
        if m < tile:
            for i, t in enumerate(tiles):
                a[i, i] = a[i, i] + mk_ref[li + 1] * _dot_nt(ql[t], kl[t])
        else:
            for r0 in range(0, rows, 2 * m):
                for i in range((r0 + m) // tile, (r0 + 2 * m) // tile):
                    for j in range(r0 // tile, (r0 + m) // tile):
                        a[i, j] = _dot_nt(ql[tiles[i]], kl[tiles[j]])
    o_tiles = []
    for i in range(len(tiles)):
        terms = [_dot(a[i, j].astype(BF16), vb[tiles[j]]) for j in range(len(tiles)) if (i, j) in a]
        o_tiles.append(functools.reduce(lambda x, y: x + y, terms))
    o = _cat(o_tiles)
    parts = []
    for i in range(nblk):
        r = slice(i * blk, (i + 1) * blk)
        s, = get(i)
        parts.append(o[r] + _dot(qt[r], s.astype(BF16)))
        put(i, (_state_update(kw[r], v[r], e_last[i * blk:i * blk + 1], s),))
    return _cat(parts)


def _hgrn_kernel(q_ref, f_ref, i_ref, gt_ref, lb_ref, gn_ref, *rest, blk, nblk, per_seq, layer):
    mk_ref, s_in = rest[:2]
    y_ref, s_out = rest[-2:]
    lbl = lb_ref[...]
    e = jnp.exp(lbl - jnp.max(lbl, axis=0, keepdims=True))
    lb = jnp.sum(e[0:layer + 1], axis=0, keepdims=True) / jnp.sum(e, axis=0, keepdims=True)
    get, put = _states(per_seq, nblk, (s_in,), (s_out,))
    k = (1.0 - lb) * (1.0 - jax.nn.sigmoid(f_ref[...]))
    o = _key_decay_slab(_silu(q_ref[...]), k, i_ref[...], jnp.log(1.0 - k), mk_ref, blk, nblk, get, put)
    y_ref[...] = _head_out(o, gn_ref[...], _silu(gt_ref[...]))


def _gla_kernel(q_ref, k_ref, v_ref, gt_ref, tail_ref, w2_ref, ab_ref, gn_ref, *rest, blk, nblk, per_seq):
    mk_ref, s_in = rest[:2]
    y_ref, s_out = rest[-2:]
    wh, wm, wl = (p.astype(BF16) for p in _pieces3(w2_ref[...]))
    get, put = _states(per_seq, nblk, (s_in,), (s_out,))
    th, tm_, tl = (p.astype(BF16) for p in _pieces3(tail_ref[...]))
    x = (_dot(th, wh) + (_dot(th, wm) + _dot(tm_, wh))
         + (_dot(th, wl) + _dot(tm_, wm) + _dot(tl, wh)))
    g = _log_sigmoid(x + ab_ref[...]) / GLA_TAU
    o = _key_decay_slab(q_ref[...] * (DK_D ** -0.5), k_ref[...], v_ref[...], g, mk_ref, blk, nblk, get, put)
    y_ref[...] = _head_out(o, gn_ref[...], _silu(gt_ref[...]))


def _mlstm_kernel(q_ref, k_ref, v_ref, og_ref, tail_ref, wq_ref, wk_ref, bias_ref, gn_ref,
                  cq_in, ck_in, c_in, n_in, m_in, *rest, blk, nblk, per_seq):
    y_ref, c_out, n_out, m_out, cq_out, ck_out, uq_ref, uk_ref, hq_ref, hk_ref = rest[-10:]
    assert per_seq or nblk == 1
    rows = blk * nblk
    h = pl.program_id(1)
    hist = CONV_W - 1
    lo = SUBLANES - hist

    if not per_seq:
        @pl.when(pl.program_id(2) == 0)
        def _():
            for h_ref, cin in ((hq_ref, cq_in), (hk_ref, ck_in)):
                h_ref[...] = jnp.zeros_like(h_ref)
                h_ref[lo:SUBLANES, :] = cin[0]

    def conv(u_ref, h_ref, cin, cout, raw_ref, w_ref):
        parts = []
        for i in range(nblk):
            raw = raw_ref[i * blk:(i + 1) * blk, :]
            if per_seq:
                u_ref[i, lo:SUBLANES, :] = cin[i]
            else:
                u_ref[i, 0:SUBLANES, :] = h_ref[...]
                h_ref[...] = raw[blk - SUBLANES:blk]
            u_ref[i, SUBLANES:SUBLANES + blk, :] = raw
            cout[i] = raw[blk - hist:blk]
            acc = u_ref[i, lo:lo + blk, :] * w_ref[0:1, :]
            for j in range(1, CONV_W):
                acc = acc + u_ref[i, lo + j:lo + j + blk, :] * w_ref[j:j + 1, :]
            parts.append(acc)
        return _silu(_cat(parts))

    q = conv(uq_ref, hq_ref, cq_in, cq_out, q_ref, wq_ref)
    k = conv(uk_ref, hk_ref, ck_in, ck_out, k_ref, wk_ref) * (DK_C ** -0.5)
    qb = q.astype(BF16)
    vb = v_ref[...].astype(BF16)

    lane = lax.broadcasted_iota(jnp.int32, (rows, LANES), 1)
    tl = tail_ref[...] + bias_ref[...]
    li = jnp.sum(jnp.where(lane == h, tl, 0.0), axis=-1, keepdims=True)
    lf = jnp.sum(jnp.where(lane == H_C + h, _log_sigmoid(tl), 0.0), axis=-1, keepdims=True)
    b = _prefix(jnp.broadcast_to(lf, (rows, LANES)), blk)[:, :1]

    get, put = _states(per_seq, nblk, (c_in, n_in, m_in), (c_out, n_out, m_out))
    states = [get(i) for i in range(nblk)] if per_seq else [get(0)]
    m_prev = _cat([jnp.broadcast_to(st[2][:, :1], (blk, 1)) for st in states])

    mask, _ = _block_mask(rows, blk)
    d = jnp.where(mask, b + _row_bcast(li - b), -jnp.inf)
    m_inter = b + m_prev
    m_t = jnp.maximum(m_inter, jnp.max(d, axis=-1, keepdims=True))
    w_inter = jnp.exp(m_inter - m_t)
    sc = _dot_nt(qb, k.astype(BF16)) * jnp.exp(d - m_t)
    num = _dot(sc.astype(BF16), vb)
    den = jnp.sum(sc, axis=-1, keepdims=True)
    m_new = _block_last(m_t, blk)
    b_last = _block_last(b, blk)
    kw = k * jnp.exp(b_last - b + li - m_new)
    w_c = jnp.exp(b_last + m_prev - m_new)
    nums, dens = [], []
    for i in range(nblk):
        r = slice(i * blk, (i + 1) * blk)
        cst, nrow, _ = states[i]
        nums.append(num[r] + w_inter[r] * _dot(qb[r], cst.astype(BF16)))
        dens.append(den[r] + w_inter[r] * jnp.sum(q[r] * nrow, axis=-1, keepdims=True))
        wc = w_c[i * blk:i * blk + 1]
        put(i, (wc * cst + _dot_tn(kw[r].astype(BF16), vb[r]),
                wc * nrow + jnp.sum(kw[r], axis=0, keepdims=True),
                jnp.broadcast_to(m_new[i * blk:i * blk + 1], (1, LANES))))
    hh = _cat(nums) / jnp.maximum(jnp.abs(_cat(dens)), jnp.exp(-m_t))
    y_ref[...] = _head_out(hh, gn_ref[...], jax.nn.sigmoid(og_ref[...]))


def _geometry(n_seq, seq_len, row0, blk_prompt):
    if seq_len >= SLAB:
        blk, rows, per_seq, nseq_blk = blk_prompt, SLAB, False, 1
        grid_seq, grid_slab = n_seq, seq_len // SLAB
    else:
        blk, nseq_blk, per_seq = seq_len, SEQ_PER_STEP, True
        rows = nseq_blk * blk
        grid_seq, grid_slab = n_seq // nseq_blk, 1
    base = row0 // rows
    return dict(blk=blk, rows=rows, nblk=rows // blk, per_seq=per_seq, nseq_blk=nseq_blk,
                grid_seq=grid_seq, grid_slab=grid_slab, rowblk=lambda b, s: base + b * grid_slab + s)


def _zspec(geo, width, col):
    rb = geo["rowblk"]
    return pl.BlockSpec((geo["rows"], width), lambda b, h, s: (rb(b, s), col + h))


def _const_spec(shape):
    return pl.BlockSpec(shape, lambda b, h, s: (0,) * len(shape))


def _head_spec(width, col=0):
    return pl.BlockSpec((1, width), lambda b, h, s: (0, col + h))


def _state_spec(geo, shape):
    return pl.BlockSpec((geo["nseq_blk"], 1) + shape, lambda b, h, s: (b, h, 0, 0))


def _mixer_call(kern, geo, heads, vd, y_col, y_prev, n_tok, in_specs, args, extra_specs, extra_shapes,
                scratch, name):
    if y_prev is not None:
        in_specs = in_specs + [pl.BlockSpec(memory_space=pl.ANY)]
        args = args + [y_prev]
    rb = geo["rowblk"]
    return pl.pallas_call(
        kern,
        grid=(geo["grid_seq"], heads, geo["grid_slab"]),
        in_specs=in_specs,
        out_specs=[pl.BlockSpec((geo["rows"], vd), lambda b, h, s: (rb(b, s), y_col // vd + h))] + extra_specs,
        out_shape=[jax.ShapeDtypeStruct((n_tok, MIX), F32)] + extra_shapes,
        input_output_aliases={} if y_prev is None else {len(args) - 1: 0},
        scratch_shapes=scratch,
        compiler_params=_cparams(("parallel", "parallel", "arbitrary")),
        name=name + ("_sample" if geo["per_seq"] else "_prompt"),
    )(*args)


def _decay_mixer(kind, z, geo, state, extra, y_prev, y_col):
    heads, kd, vd = {"ret": (H_A, DK_A, DV_A), "hgrn": (H_B, DK_B, DV_B), "gla": (H_D, DK_D, DV_D)}[kind]
    common = dict(blk=geo["blk"], nblk=geo["nblk"], per_seq=geo["per_seq"])
    kb, vb = LANES, vd
    if kind == "ret":
        cos, sin = extra["rope"]
        tab = pl.BlockSpec((geo["blk"] if geo["per_seq"] else geo["rows"], DK_A),
                           lambda b, h, s: (0 if geo["per_seq"] else s, 0))
        in_specs = [_zspec(geo, kb, 0), _zspec(geo, kb, H_A), _zspec(geo, vb, (2 * H_A * DK_A) // vb),
                    _zspec(geo, vb, (2 * H_A * DK_A) // vb + H_A), tab, tab, _head_spec(vd)]
        args = [z, z, z, z, cos, sin, extra["gn"].reshape(1, -1)]
        kern = functools.partial(_ret_kernel, **common)
    elif kind == "hgrn":
        off = (2 * H_A * DK_A + 2 * H_A * DV_A) // LANES
        in_specs = [_zspec(geo, kb, off), _zspec(geo, kb, off + H_B), _zspec(geo, kb, off + 2 * H_B),
                    _zspec(geo, kb, off + 3 * H_B),
                    pl.BlockSpec((DEPTH + 1, DK_B), lambda b, h, s: (0, h)), _head_spec(vd)]
        args = [z, z, z, z, extra["lb_logits"], extra["gn"].reshape(1, -1)]
        kern = functools.partial(_hgrn_kernel, layer=extra["layer"], **common)
    else:
        off = (2 * H_C * DK_C + 2 * H_C * DV_C) // LANES
        voff = (2 * H_C * DK_C + 2 * H_C * DV_C + 2 * H_D * DK_D) // vb
        in_specs = [_zspec(geo, kb, off), _zspec(geo, kb, off + H_D), _zspec(geo, vb, voff),
                    _zspec(geo, vb, voff + H_D),
                    pl.BlockSpec((geo["rows"], CD_TAIL), lambda b, h, s: (geo["rowblk"](b, s), CD_MAIN // CD_TAIL)),
                    pl.BlockSpec((CD_TAIL, DK_D), lambda b, h, s: (0, h)),
                    _head_spec(DK_D), _head_spec(vd)]
        args = [z, z, z, z, z, extra["w2"], extra["ab"].reshape(1, -1), extra["gn"].reshape(1, -1)]
        kern = functools.partial(_gla_kernel, **common)
    if kind != "ret":
        tile = min(geo["rows"], PAIR_TILE)
        mk = _pair_masks(tile, min(geo["blk"], tile))
        in_specs.append(_const_spec(mk.shape))
        args.append(mk)
    in_specs.append(_state_spec(geo, (kd, vd)))
    args.append(state)
    return _mixer_call(kern, geo, heads, vd, y_col, y_prev, z.shape[0], in_specs, args,
                       [_state_spec(geo, (kd, vd))], [jax.ShapeDtypeStruct(state.shape, F32)], [], kind)


def _mlstm_mixer(z, geo, c0, n0, m0, conv0, w, y_prev, y_col):
    n_seq = c0.shape[0]
    blk, nb = geo["blk"], geo["nseq_blk"]
    bias = jnp.zeros((1, LANES), F32).at[0, :H_C].set(w["i_bias"]).at[0, H_C:2 * H_C].set(w["f_bias"])
    kk = (H_C * DK_C) // DK_C
    kv = (2 * H_C * DK_C) // DV_C
    conv_spec = lambda col: pl.BlockSpec((nb, CONV_W - 1, DK_C), lambda b, h, s: (b, 0, col + h))
    in_specs = [
        _zspec(geo, DK_C, 0), _zspec(geo, DK_C, kk), _zspec(geo, DV_C, kv), _zspec(geo, DV_C, kv + H_C),
        pl.BlockSpec((geo["rows"], CD_TAIL), lambda b, h, s: (geo["rowblk"](b, s), CD_MAIN // CD_TAIL)),
        pl.BlockSpec((CONV_W, DK_C), lambda b, h, s: (0, h)),
        pl.BlockSpec((CONV_W, DK_C), lambda b, h, s: (0, H_C + h)),
        _const_spec((1, LANES)), _head_spec(DV_C),
        conv_spec(0), conv_spec(H_C),
        _state_spec(geo, (DK_C, DV_C)), _state_spec(geo, (1, DK_C)), _state_spec(geo, (1, LANES)),
    ]
    n4 = n0.reshape(n_seq, H_C, 1, DK_C)
    m4 = jnp.broadcast_to(m0[:, :, None, None], (n_seq, H_C, 1, LANES))
    args = [z, z, z, z, z, w["conv_w"], w["conv_w"], bias, w["gn_c"].reshape(1, -1),
            conv0, conv0, c0, n4, m4]
    conv_shape = jax.ShapeDtypeStruct((n_seq, CONV_W - 1, H_C * DK_C), F32)
    y, c_new, n_new, m_new, cq, ck = _mixer_call(
        functools.partial(_mlstm_kernel, blk=blk, nblk=geo["nblk"], per_seq=geo["per_seq"]),
        geo, H_C, DV_C, y_col, y_prev, z.shape[0], in_specs, args,
        [_state_spec(geo, (DK_C, DV_C)), _state_spec(geo, (1, DK_C)), _state_spec(geo, (1, LANES)),
         conv_spec(0), conv_spec(0)],
        [jax.ShapeDtypeStruct(c0.shape, F32), jax.ShapeDtypeStruct(n4.shape, F32),
         jax.ShapeDtypeStruct(m4.shape, F32), conv_shape, conv_shape],
        [pltpu.VMEM((geo["nblk"], SUBLANES + blk, DK_C), F32), pltpu.VMEM((geo["nblk"], SUBLANES + blk, DK_C), F32),
         pltpu.VMEM((SUBLANES, DK_C), F32), pltpu.VMEM((SUBLANES, DK_C), F32)],
        "mlstm")
    return (y, c_new, n_new.reshape(n_seq, H_C, DK_C), m_new[:, :, 0, 0],
            jnp.concatenate([cq, ck], axis=-1))


def _rope_tables(pos0, n):
    half = DK_A // 2
    inv = ROPE_BASE ** (-jnp.arange(half, dtype=F32) / half)
    ang = (jnp.arange(n, dtype=F32) + float(pos0))[:, None] * inv[None, :]
    cos, sin = jnp.cos(ang), jnp.sin(ang)
    return jnp.concatenate([cos, cos], axis=1), jnp.concatenate([-sin, sin], axis=1)


def kernel(x_prompt, x_sample, state_ret, state_hgrn, state_mlstm_c, state_mlstm_n, state_mlstm_m,
           state_mlstm_conv, state_gla, norm_gain, ffn_w1, ffn_w3, ffn_w2, ab_w_in, ab_w_out, ab_norm_a,
           ab_norm_b, lb_logits, cd_w_in, cd_w_out, cd_conv_w, cd_i_bias, cd_f_bias, cd_norm_c,
           cd_alpha_w2, cd_alpha_b, cd_norm_d, final_norm):
    bp, tp, _ = x_prompt.shape
    bs, ts, _ = x_sample.shape
    n_p, n_s = bp * tp, bs * ts
    x = jnp.concatenate([x_prompt.reshape(n_p, D_MODEL), x_sample.reshape(n_s, D_MODEL)], axis=0)
    w1, w3, w2 = ffn_w1.astype(BF16), ffn_w3.astype(BF16), ffn_w2.astype(BF16)
    groups = (("p", bp, _geometry(bp, tp, 0, SLAB), _geometry(bp, tp, 0, CHUNK)),
              ("s", bs, _geometry(bs, ts, n_p, SLAB), _geometry(bs, ts, n_p, CHUNK)))
    rope = {"p": _rope_tables(0, tp), "s": _rope_tables(PAST_LEN, ts)}
    zeros = lambda *shape: jnp.zeros(shape, F32)
    outs = {}

    def keep(name, tag, val):
        outs.setdefault(name + "_" + tag, []).append(val)

    for l in range(DEPTH):
        j = l // 2
        x = _ffn(x, norm_gain[l, 0], w1, w3, w2, l, 0)
        y = None
        if l % 2 == 0:
            z = _inproj(x, norm_gain[l, 1], ab_w_in[j].astype(BF16))
            for tag, nseq, geo_h, geo_k in groups:
                s_ret = zeros(nseq, H_A, DK_A, DV_A) if tag == "p" else state_ret[j]
                s_hg = zeros(nseq, H_B, DK_B, DV_B) if tag == "p" else state_hgrn[j]
                y, r_new = _decay_mixer("ret", z, geo_h, s_ret, dict(rope=rope[tag], gn=ab_norm_a[j]), y, 0)
                y, g_new = _decay_mixer("hgrn", z, geo_k, s_hg,
                                        dict(lb_logits=lb_logits, gn=ab_norm_b[j], layer=l), y, H_A * DV_A)
                keep("ret", tag, r_new)
                keep("hgrn", tag, g_new)
            x = _outproj(x, y, ab_w_out[j].astype(BF16))
        else:
            w_in = cd_w_in[j]
            gates = 2 * H_C
            main = 2 * H_C * DK_C + 2 * H_C * DV_C
            rest = CD_MAIN - main
            n_pad = -(-(CD_MAIN + CD_TAIL) // (N_SPLIT * LANES)) * (N_SPLIT * LANES)
            pieces = ((w_in[:, :main], 0), (w_in[:, main + gates:main + gates + rest], main),
                      (w_in[:, main:main + gates], CD_MAIN), (w_in[:, main + gates + rest:], CD_MAIN + gates))
            w_perm = functools.reduce(lambda a, b: a + b, [
                jnp.pad(p, ((0, 0), (off, n_pad - off - p.shape[1]))) for p, off in pieces]).astype(BF16)
            z = _inproj(x, norm_gain[l, 1], w_perm)
            w2p = jnp.zeros((CD_TAIL, H_D * DK_D), F32).at[gates:gates + GLA_RANK].set(cd_alpha_w2[j])
            wts = dict(conv_w=cd_conv_w[j], i_bias=cd_i_bias[j], f_bias=cd_f_bias[j], gn_c=cd_norm_c[j])
            for tag, nseq, geo_h, geo_k in groups:
                if tag == "p":
                    c0, n0, m0 = zeros(nseq, H_C, DK_C, DV_C), zeros(nseq, H_C, DK_C), zeros(nseq, H_C)
                    cv0, s_gl = zeros(nseq, CONV_W - 1, 2 * H_C * DK_C), zeros(nseq, H_D, DK_D, DV_D)
                else:
                    c0, n0, m0 = state_mlstm_c[j], state_mlstm_n[j], state_mlstm_m[j]
                    cv0, s_gl = state_mlstm_conv[j], state_gla[j]
                y, c_new, n_new, m_new, cv_new = _mlstm_mixer(z, geo_h, c0, n0, m0, cv0, wts, y, 0)
                y, gl_new = _decay_mixer("gla", z, geo_k, s_gl,
                                         dict(w2=w2p, ab=cd_alpha_b[j], gn=cd_norm_d[j]), y, H_C * DV_C)
                for name, val in (("c", c_new), ("n", n_new), ("m", m_new), ("conv", cv_new), ("gla", gl_new)):
                    keep(name, tag, val)
            x = _outproj(x, y, cd_w_out[j].astype(BF16))
        x = _ffn(x, norm_gain[l, 2], w1, w3, w2, l, 1, final_gain=final_norm if l == DEPTH - 1 else None)

    y_p = x[:n_p].reshape(bp, tp, D_MODEL)
    y_s = x[n_p:].reshape(bs, ts, D_MODEL)
    st = lambda name: jnp.stack(outs[name])
    return (y_p, y_s, st("ret_p"), st("ret_s"), st("hgrn_p"), st("hgrn_s"), st("c_p"), st("c_s"),
            st("n_p"), st("n_s"), st("m_p"), st("m_s"), st("conv_p"), st("conv_s"), st("gla_p"), st("gla_s"))
```

```python
import functools

import numpy as np
import jax
import jax.numpy as jnp
from jax import lax
from jax.experimental import pallas as pl
from jax.experimental.pallas import tpu as pltpu

F32 = jnp.float32
BF16 = jnp.bfloat16

D_MODEL = 2048
DEPTH = 2
PAST_LEN = 16384
H_A, DK_A, DV_A = 4, 128, 256
ROPE_BASE = 10000.0
H_B, DK_B, DV_B = 8, 128, 128
H_C, DK_C, DV_C = 4, 256, 256
CONV_W = 4
H_D, DK_D, DV_D = 4, 128, 256
GLA_RANK = 16
GLA_TAU = 16.0
FF_DIM = 5632
EPS = 1e-6

CD_MAIN = 2 * H_C * DK_C + 2 * H_C * DV_C + 2 * H_D * DK_D + 2 * H_D * DV_D
CD_TAIL = 128
MIX = H_A * DV_A + H_B * DV_B

LANES = 128
SUBLANES = 8
VMEM_LIMIT = 56 * 1024 * 1024

TM = 768
TM_IN = 1152
TM_OUT = 512
TF = 512
N_SPLIT = 4
CHUNK = 256
SLAB = 256
SEQ_PER_STEP = 16
PAIR_TILE = 128
LOG2E = 1.4426950408889634


def _cparams(sem):
    return pltpu.CompilerParams(dimension_semantics=sem, vmem_limit_bytes=VMEM_LIMIT)


def _rms(x, g):
    return x * lax.rsqrt(jnp.mean(x * x, axis=-1, keepdims=True) + EPS) * g


def _silu(x):
    return x * jax.nn.sigmoid(x)


def _log_sigmoid(x):
    return jnp.minimum(x, 0.0) - jnp.log1p(jnp.exp(-jnp.abs(x)))


def _pieces3(x):
    hi = x.astype(BF16).astype(F32)
    r1 = x - hi
    mid = r1.astype(BF16).astype(F32)
    lo = (r1 - mid).astype(BF16).astype(F32)
    return hi, mid, lo


def _dot(a, b):
    return jnp.dot(a, b, preferred_element_type=F32)


def _dot_nt(a, b):
    return lax.dot_general(a, b, (((1,), (1,)), ((), ())), preferred_element_type=F32)


def _dot_tn(a, b):
    return lax.dot_general(a, b, (((0,), (0,)), ((), ())), preferred_element_type=F32)


def _cat(parts, axis=0):
    return parts[0] if len(parts) == 1 else jnp.concatenate(parts, axis=axis)


def _prefix(g, blk):
    rows, n = g.shape
    row = lax.broadcasted_iota(jnp.int32, (SUBLANES, n), 0)
    outs = []
    off = None
    for r0 in range(0, rows, SUBLANES):
        x = g[r0:r0 + SUBLANES]
        for sh in (1, 2, 4):
            x = x + jnp.where(row >= sh, pltpu.roll(x, sh, 0), 0.0)
        if r0 % blk:
            x = x + off
        off = x[SUBLANES - 1:SUBLANES]
        outs.append(x)
    return _cat(outs)


def _block_last(x, blk):
    rows, n = x.shape
    return _cat([jnp.broadcast_to(x[r0 + blk - 1:r0 + blk], (blk, n)) for r0 in range(0, rows, blk)])


def _block_mask(rows, blk):
    tt = lax.broadcasted_iota(jnp.int32, (rows, rows), 0)
    ss = lax.broadcasted_iota(jnp.int32, (rows, rows), 1)
    mask = ss <= tt
    if blk < rows:
        sh = blk.bit_length() - 1
        mask = jnp.logical_and(mask, lax.shift_right_logical(tt, sh) == lax.shift_right_logical(ss, sh))
    return mask, tt - ss


def _row_bcast(col):
    c = col.shape[0]
    lane = lax.broadcasted_iota(jnp.int32, (c, LANES), 1)
    pieces = _pieces3(jnp.where(lane == 0, jnp.broadcast_to(col, (c, LANES)), 0.0))
    rm = jnp.concatenate(pieces, axis=1).astype(BF16)
    ones = jnp.ones((c, 3 * LANES), BF16)
    return _dot_nt(ones, rm)


def _levels(blk):
    out = []
    m = 1
    while m < blk:
        out.append(m)
        m *= 2
    return tuple(out)


def _pair_masks(rows, blk):
    t = np.arange(rows)[:, None]
    u = np.arange(rows)[None, :]
    masks = [t == u] + [((t % (2 * m)) >= m) & ((u % (2 * m)) < m) & (t // (2 * m) == u // (2 * m))
                        for m in _levels(blk)]
    return jnp.asarray(np.stack(masks).astype(np.float32))


def _level_exponent(b, g, m):
    rows, n = b.shape
    if m >= SUBLANES:
        mid = _cat([jnp.broadcast_to(b[r0 + m - 1:r0 + m], (2 * m, n)) for r0 in range(0, rows, 2 * m)])
        return -jnp.abs(b - mid)
    row = lax.broadcasted_iota(jnp.int32, (SUBLANES, n), 0)
    outs = []
    for r0 in range(0, rows, SUBLANES):
        x = b[r0:r0 + SUBLANES]
        if m == 1:
            outs.append(jnp.where(jnp.bitwise_and(row, 1) == 1, g[r0:r0 + SUBLANES], 0.0))
            continue
        if m == 2:
            mid = jnp.where(row < 4, jnp.broadcast_to(x[1:2], x.shape), jnp.broadcast_to(x[5:6], x.shape))
        else:
            mid = jnp.broadcast_to(x[3:4], x.shape)
        outs.append(-jnp.abs(x - mid))
    return _cat(outs)


def _state_update(kw, v, e_row, s):
    c, kd = kw.shape
    vd = v.shape[1]
    pad = 2 * SUBLANES
    lhs = jnp.concatenate(_pieces3(e_row) + (jnp.zeros((pad - 3, kd), F32), kw), axis=0)
    sel = (lax.broadcasted_iota(jnp.int32, (pad, LANES), 0) < 3).astype(F32)
    rhs = jnp.concatenate(
        [jnp.concatenate([jnp.zeros((pad, vd), F32), sel], axis=1),
         jnp.concatenate([v, jnp.zeros((c, LANES), F32)], axis=1)], axis=0)
    upd = _dot_tn(lhs.astype(BF16), rhs.astype(BF16))
    e_col = upd[:, vd:]
    if vd > LANES:
        e_col = jnp.concatenate([e_col] * (vd // LANES), axis=1)
    return e_col * s + upd[:, :vd]


def _head_out(o, gn, gate_act):
    return o * lax.rsqrt(jnp.mean(o * o, axis=-1, keepdims=True) + EPS) * gn * gate_act


def _ffn_kernel(*refs, nf, post):
    x_ref, g_ref, w1_ref, w3_ref, w2_ref = refs[:5]
    pg_ref = refs[5] if post else None
    outs = refs[6 if post else 5:-1]
    xn_ref = refs[-1]
    acc_ref = outs[0]
    f = pl.program_id(1)

    @pl.when(f == 0)
    def _():
        xn_ref[...] = _rms(x_ref[...], g_ref[...]).astype(BF16)
        acc_ref[...] = jnp.zeros_like(acc_ref)

    xn = xn_ref[...]
    h1 = _dot(xn, w1_ref[...])
    h3 = _dot(xn, w3_ref[...])
    acc_ref[...] += _dot((_silu(h1) * h3).astype(BF16), w2_ref[...])

    @pl.when(f == nf - 1)
    def _():
        y = x_ref[...] + 0.5 * acc_ref[...]
        if post == "final":
            outs[0][...] = _rms(y, pg_ref[...])
        else:
            outs[0][...] = y
            if post == "norm_out":
                outs[1][...] = _rms(y, pg_ref[...]).astype(BF16)


def _ffn(x, gain, w1, w3, w2, l, idx, post=None, post_gain=None):
    m = x.shape[0]
    nf = FF_DIM // TF
    in_specs = [
        pl.BlockSpec((TM, D_MODEL), lambda i, f: (i, 0)),
        pl.BlockSpec((1, D_MODEL), lambda i, f: (0, 0)),
        pl.BlockSpec((None, None, D_MODEL, TF), lambda i, f: (l, idx, 0, f)),
        pl.BlockSpec((None, None, D_MODEL, TF), lambda i, f: (l, idx, 0, f)),
        pl.BlockSpec((None, None, TF, D_MODEL), lambda i, f: (l, idx, f, 0)),
    ]
    args = [x, gain.reshape(1, D_MODEL), w1, w3, w2]
    if post:
        in_specs.append(pl.BlockSpec((1, D_MODEL), lambda i, f: (0, 0)))
        args.append(post_gain.reshape(1, D_MODEL))
    row_spec = pl.BlockSpec((TM, D_MODEL), lambda i, f: (i, 0))
    out_specs, out_shape = [row_spec], [jax.ShapeDtypeStruct((m, D_MODEL), F32)]
    if post == "norm_out":
        out_specs.append(row_spec)
        out_shape.append(jax.ShapeDtypeStruct((m, D_MODEL), BF16))
    res = pl.pallas_call(
        functools.partial(_ffn_kernel, nf=nf, post=post),
        grid=(m // TM, nf),
        in_specs=in_specs,
        out_specs=out_specs,
        out_shape=out_shape,
        scratch_shapes=[pltpu.VMEM((TM, D_MODEL), BF16)],
        compiler_params=_cparams(("parallel", "arbitrary")),
        name="ffn_" + post if post else "ffn",
    )(*args)
    return res if post == "norm_out" else res[0]


def _permute_cols_kernel(w_ref, o_ref, *, pieces):
    x = w_ref[...]
    parts = [x[:, a:b] for a, b in pieces]
    width = sum(b - a for a, b in pieces)
    parts.append(jnp.zeros((x.shape[0], o_ref.shape[1] - width), x.dtype))
    o_ref[...] = jnp.concatenate(parts, axis=1).astype(o_ref.dtype)


def _permute_cols(w, pieces, n_out, rows_per_step=256):
    k = w.shape[0]
    return pl.pallas_call(
        functools.partial(_permute_cols_kernel, pieces=pieces),
        grid=(k // rows_per_step,),
        in_specs=[pl.BlockSpec((rows_per_step, w.shape[1]), lambda i: (i, 0))],
        out_specs=pl.BlockSpec((rows_per_step, n_out), lambda i: (i, 0)),
        out_shape=jax.ShapeDtypeStruct((k, n_out), BF16),
        compiler_params=_cparams(("parallel",)),
        name="permute_cols",
    )(w)


def _inproj_kernel(h_ref, w_ref, z_ref):
    z_ref[...] = _dot(h_ref[...], w_ref[...])


def _inproj(h, w):
    m, n = h.shape[0], w.shape[1]
    tn = n // N_SPLIT
    return pl.pallas_call(
        _inproj_kernel,
        grid=(N_SPLIT, m // TM_IN),
        in_specs=[
            pl.BlockSpec((TM_IN, D_MODEL), lambda j, i: (i, 0)),
            pl.BlockSpec((D_MODEL, tn), lambda j, i: (0, j)),
        ],
        out_specs=pl.BlockSpec((TM_IN, tn), lambda j, i: (i, j)),
        out_shape=jax.ShapeDtypeStruct((m, n), F32),
        compiler_params=_cparams(("parallel", "parallel")),
        name="inproj",
    )(h, w)


def _outproj_kernel(x_ref, y_ref, w_ref, o_ref):
    o_ref[...] = x_ref[...] + _dot(y_ref[...].astype(BF16), w_ref[...])


def _outproj(x, y, w):
    m, kdim = y.shape
    return pl.pallas_call(
        _outproj_kernel,
        grid=(m // TM_OUT,),
        in_specs=[
            pl.BlockSpec((TM_OUT, D_MODEL), lambda i: (i, 0)),
            pl.BlockSpec((TM_OUT, kdim), lambda i: (i, 0)),
            pl.BlockSpec((kdim, D_MODEL), lambda i: (0, 0)),
        ],
        out_specs=pl.BlockSpec((TM_OUT, D_MODEL), lambda i: (i, 0)),
        out_shape=jax.ShapeDtypeStruct((m, D_MODEL), F32),
        compiler_params=_cparams(("parallel",)),
        name="outproj",
    )(x, y, w)


def _states(per_seq, nblk, ins, outs):
    if per_seq:
        def get(i):
            return tuple(r[i, 0] for r in ins)

        def put(i, vals):
            for o, val in zip(outs, vals):
                o[i, 0] = val
        return get, put

    @pl.when(pl.program_id(2) == 0)
    def _():
        for o, r in zip(outs, ins):
            o[...] = r[...]

    carried = {"v": tuple(o[0, 0] for o in outs)}

    def get(i):
        return carried["v"]

    def put(i, vals):
        carried["v"] = tuple(vals)
        if i == nblk - 1:
            for o, val in zip(outs, vals):
                o[0, 0] = val
    return get, put


def _ret_kernel(q_ref, k_ref, v_ref, gt_ref, cos_ref, sin_ref, gn_ref, s_in, *rest, blk, nblk, per_seq):
    y_ref, s_out = rest[-2:]
    rows = blk * nblk
    hf = jnp.full((1, 1), pl.program_id(1), jnp.int32).astype(F32)
    lg = jnp.log1p(-jnp.exp2(-5.0 - hf))
    cos = _cat([cos_ref[...]] * (rows // cos_ref.shape[0]))
    sin = _cat([sin_ref[...]] * (rows // sin_ref.shape[0]))
    q = q_ref[...]
    k = k_ref[...]
    q = q * cos + pltpu.roll(q, DK_A // 2, 1) * sin
    k = (k * cos + pltpu.roll(k, DK_A // 2, 1) * sin) * (DK_A ** -0.5)
    vb = v_ref[...].astype(BF16)
    mask, dist = _block_mask(rows, blk)
    dec = jnp.where(mask, jnp.exp(jnp.maximum(dist, 0).astype(F32) * lg), 0.0)
    a = _dot_nt(q.astype(BF16), k.astype(BF16)) * dec
    o = _dot(a.astype(BF16), vb)
    tloc = jnp.bitwise_and(lax.broadcasted_iota(jnp.int32, (rows, 1), 0), blk - 1).astype(F32)
    qt = (q * jnp.exp((tloc + 1.0) * lg)).astype(BF16)
    kw = (k * jnp.exp((blk - 1.0 - tloc) * lg)).astype(BF16)
    e_blk = jnp.exp(blk * lg)
    get, put = _states(per_seq, nblk, (s_in,), (s_out,))
    parts = []
    for i in range(nblk):
        r = slice(i * blk, (i + 1) * blk)
        s, = get(i)
        parts.append(o[r] + _dot(qt[r], s.astype(BF16)))
        put(i, (e_blk * s + _dot_tn(kw[r], vb[r]),))
    y_ref[...] = _head_out(_cat(parts), gn_ref[...], _silu(gt_ref[...]))


def _key_decay_slab(q, k, v, g, mk_ref, blk, nblk, get, put):
    rows = q.shape[0]
    tile = min(rows, PAIR_TILE)
    tiles = [slice(i * tile, (i + 1) * tile) for i in range(rows // tile)]
    g2 = g * LOG2E
    b = _prefix(g2, blk)
    b_last = _block_last(b, blk)
    qt = (q * jnp.exp2(b)).astype(BF16)
    kw = k * jnp.exp2(b_last - b)
    e_last = jnp.exp2(b_last)
    vb = v.astype(BF16)
    qb, kb = q.astype(BF16), k.astype(BF16)
    a = {(i, i): mk_ref[0] * _dot_nt(qb[t], kb[t]) for i, t in enumerate(tiles)}
    for li, m in enumerate(_levels(blk)):
        w = jnp.exp2(_level_exponent(b, g2, m))
        ql, kl = (q * w).astype(BF16), (k * w).astype(BF16)
        if m < tile:
            for i, t in enumerate(tiles):
                a[i, i] = a[i, i] + mk_ref[li + 1] * _dot_nt(ql[t], kl[t])
        else:
            for r0 in range(0, rows, 2 * m):
                for i in range((r0 + m) // tile, (r0 + 2 * m) // tile):
                    for j in range(r0 // tile, (r0 + m) // tile):
                        a[i, j] = _dot_nt(ql[tiles[i]], kl[tiles[j]])
    o_tiles = []
    for i in range(len(tiles)):
        terms = [_dot(a[i, j].astype(BF16), vb[tiles[j]]) for j in range(len(tiles)) if (i, j) in a]
        o_tiles.append(functools.reduce(lambda x, y: x + y, terms))
    o = _cat(o_tiles)
    parts = []
    for i in range(nblk):
        r = slice(i * blk, (i + 1) * blk)
        s, = get(i)
        parts.append(o[r] + _dot(qt[r], s.astype(BF16)))
        put(i, (_state_update(kw[r], v[r], e_last[i * blk:i * blk + 1], s),))
    return _cat(parts)


def _hgrn_kernel(q_ref, f_ref, i_ref, gt_ref, lb_ref, gn_ref, *rest, blk, nblk, per_seq, layer):
    mk_ref, s_in = rest[:2]
    y_ref, s_out = rest[-2:]
    lbl = lb_ref[...]
    e = jnp.exp(lbl - jnp.max(lbl, axis=0, keepdims=True))
    lb = jnp.sum(e[0:layer + 1], axis=0, keepdims=True) / jnp.sum(e, axis=0, keepdims=True)
    get, put = _states(per_seq, nblk, (s_in,), (s_out,))
    k = (1.0 - lb) * (1.0 - jax.nn.sigmoid(f_ref[...]))
    o = _key_decay_slab(_silu(q_ref[...]), k, i_ref[...], jnp.log(1.0 - k), mk_ref, blk, nblk, get, put)
    y_ref[...] = _head_out(o, gn_ref[...], _silu(gt_ref[...]))


def _gla_kernel(q_ref, k_ref, v_ref, gt_ref, tail_ref, w2_ref, ab_ref, gn_ref, *rest, blk, nblk, per_seq):
    mk_ref, s_in = rest[:2]
    y_ref, s_out = rest[-2:]
    wh, wm, wl = (p.astype(BF16) for p in _pieces3(w2_ref[...]))
    get, put = _states(per_seq, nblk, (s_in,), (s_out,))
    th, tm_, tl = (p.astype(BF16) for p in _pieces3(tail_ref[...]))
    x = (_dot(th, wh) + (_dot(th, wm) + _dot(tm_, wh))
         + (_dot(th, wl) + _dot(tm_, wm) + _dot(tl, wh)))
    g = _log_sigmoid(x + ab_ref[...]) / GLA_TAU
    o = _key_decay_slab(q_ref[...] * (DK_D ** -0.5), k_ref[...], v_ref[...], g, mk_ref, blk, nblk, get, put)
    y_ref[...] = _head_out(o, gn_ref[...], _silu(gt_ref[...]))


def _mlstm_kernel(q_ref, k_ref, v_ref, og_ref, tail_ref, wq_ref, wk_ref, bias_ref, gn_ref,
                  cq_in, ck_in, c_in, n_in, m_in, *rest, blk, nblk, per_seq):
    y_ref, c_out, n_out, m_out, cq_out, ck_out, uq_ref, uk_ref, hq_ref, hk_ref = rest[-10:]
    assert per_seq or nblk == 1
    rows = blk * nblk
    h = pl.program_id(1)
    hist = CONV_W - 1
    lo = SUBLANES - hist

    if not per_seq:
        @pl.when(pl.program_id(2) == 0)
        def _():
            for h_ref, cin in ((hq_ref, cq_in), (hk_ref, ck_in)):
                h_ref[...] = jnp.zeros_like(h_ref)
                h_ref[lo:SUBLANES, :] = cin[0]

    def conv(u_ref, h_ref, cin, cout, raw_ref, w_ref):
        parts = []
        for i in range(nblk):
            raw = raw_ref[i * blk:(i + 1) * blk, :]
            if per_seq:
                u_ref[i, lo:SUBLANES, :] = cin[i]
            else:
                u_ref[i, 0:SUBLANES, :] = h_ref[...]
                h_ref[...] = raw[blk - SUBLANES:blk]
            u_ref[i, SUBLANES:SUBLANES + blk, :] = raw
            cout[i] = raw[blk - hist:blk]
            acc = u_ref[i, lo:lo + blk, :] * w_ref[0:1, :]
            for j in range(1, CONV_W):
                acc = acc + u_ref[i, lo + j:lo + j + blk, :] * w_ref[j:j + 1, :]
            parts.append(acc)
        return _silu(_cat(parts))

    q = conv(uq_ref, hq_ref, cq_in, cq_out, q_ref, wq_ref)
    k = conv(uk_ref, hk_ref, ck_in, ck_out, k_ref, wk_ref) * (DK_C ** -0.5)
    qb = q.astype(BF16)
    vb = v_ref[...].astype(BF16)

    lane = lax.broadcasted_iota(jnp.int32, (rows, LANES), 1)
    tl = tail_ref[...] + bias_ref[...]
    li = jnp.sum(jnp.where(lane == h, tl, 0.0), axis=-1, keepdims=True)
    lf = jnp.sum(jnp.where(lane == H_C + h, _log_sigmoid(tl), 0.0), axis=-1, keepdims=True)
    b = _prefix(jnp.broadcast_to(lf, (rows, LANES)), blk)[:, :1]

    get, put = _states(per_seq, nblk, (c_in, n_in, m_in), (c_out, n_out, m_out))
    states = [get(i) for i in range(nblk)] if per_seq else [get(0)]
    m_prev = _cat([jnp.broadcast_to(st[2][:, :1], (blk, 1)) for st in states])

    mask, _ = _block_mask(rows, blk)
    d = jnp.where(mask, b + _row_bcast(li - b), -jnp.inf)
    m_inter = b + m_prev
    m_t = jnp.maximum(m_inter, jnp.max(d, axis=-1, keepdims=True))
    w_inter = jnp.exp(m_inter - m_t)
    sc = _dot_nt(qb, k.astype(BF16)) * jnp.exp(d - m_t)
    num = _dot(sc.astype(BF16), vb)
    den = jnp.sum(sc, axis=-1, keepdims=True)
    m_new = _block_last(m_t, blk)
    b_last = _block_last(b, blk)
    kw = k * jnp.exp(b_last - b + li - m_new)
    w_c = jnp.exp(b_last + m_prev - m_new)
    nums, dens = [], []
    for i in range(nblk):
        r = slice(i * blk, (i + 1) * blk)
        cst, nrow, _ = states[i]
        nums.append(num[r] + w_inter[r] * _dot(qb[r], cst.astype(BF16)))
        dens.append(den[r] + w_inter[r] * jnp.sum(q[r] * nrow, axis=-1, keepdims=True))
        wc = w_c[i * blk:i * blk + 1]
        put(i, (wc * cst + _dot_tn(kw[r].astype(BF16), vb[r]),
                wc * nrow + jnp.sum(kw[r], axis=0, keepdims=True),
                jnp.broadcast_to(m_new[i * blk:i * blk + 1], (1, LANES))))
    hh = _cat(nums) / jnp.maximum(jnp.abs(_cat(dens)), jnp.exp(-m_t))
    y_ref[...] = _head_out(hh, gn_ref[...], jax.nn.sigmoid(og_ref[...]))


def _geometry(n_seq, seq_len, row0, blk_prompt):
    if seq_len >= SLAB:
        blk, rows, per_seq, nseq_blk = blk_prompt, SLAB, False, 1
        grid_seq, grid_slab = n_seq, seq_len // SLAB
    else:
        blk, nseq_blk, per_seq = seq_len, SEQ_PER_STEP, True
        rows = nseq_blk * blk
        grid_seq, grid_slab = n_seq // nseq_blk, 1
    base = row0 // rows
    return dict(blk=blk, rows=rows, nblk=rows // blk, per_seq=per_seq, nseq_blk=nseq_blk,
                grid_seq=grid_seq, grid_slab=grid_slab, rowblk=lambda b, s: base + b * grid_slab + s)


def _zspec(geo, width, col):
    rb = geo["rowblk"]
    return pl.BlockSpec((geo["rows"], width), lambda b, h, s: (rb(b, s), col + h))


def _const_spec(shape):
    return pl.BlockSpec(shape, lambda b, h, s: (0,) * len(shape))


def _head_spec(width, col=0):
    return pl.BlockSpec((1, width), lambda b, h, s: (0, col + h))


def _state_spec(geo, shape):
    return pl.BlockSpec((geo["nseq_blk"], 1) + shape, lambda b, h, s: (b, h, 0, 0))


def _mixer_call(kern, geo, heads, vd, y_col, y_prev, n_tok, in_specs, args, extra_specs, extra_shapes,
                scratch, name):
    if y_prev is not None:
        in_specs = in_specs + [pl.BlockSpec(memory_space=pl.ANY)]
        args = args + [y_prev]
    rb = geo["rowblk"]
    return pl.pallas_call(
        kern,
        grid=(geo["grid_seq"], heads, geo["grid_slab"]),
        in_specs=in_specs,
        out_specs=[pl.BlockSpec((geo["rows"], vd), lambda b, h, s: (rb(b, s), y_col // vd + h))] + extra_specs,
        out_shape=[jax.ShapeDtypeStruct((n_tok, MIX), F32)] + extra_shapes,
        input_output_aliases={} if y_prev is None else {len(args) - 1: 0},
        scratch_shapes=scratch,
        compiler_params=_cparams(("parallel", "parallel", "arbitrary")),
        name=name + ("_sample" if geo["per_seq"] else "_prompt"),
    )(*args)


def _decay_mixer(kind, z, geo, state, extra, y_prev, y_col):
    heads, kd, vd = {"ret": (H_A, DK_A, DV_A), "hgrn": (H_B, DK_B, DV_B), "gla": (H_D, DK_D, DV_D)}[kind]
    common = dict(blk=geo["blk"], nblk=geo["nblk"], per_seq=geo["per_seq"])
    kb, vb = LANES, vd
    if kind == "ret":
        cos, sin = extra["rope"]
        tab = pl.BlockSpec((geo["blk"] if geo["per_seq"] else geo["rows"], DK_A),
                           lambda b, h, s: (0 if geo["per_seq"] else s, 0))
        in_specs = [_zspec(geo, kb, 0), _zspec(geo, kb, H_A), _zspec(geo, vb, (2 * H_A * DK_A) // vb),
                    _zspec(geo, vb, (2 * H_A * DK_A) // vb + H_A), tab, tab, _head_spec(vd)]
        args = [z, z, z, z, cos, sin, extra["gn"].reshape(1, -1)]
        kern = functools.partial(_ret_kernel, **common)
    elif kind == "hgrn":
        off = (2 * H_A * DK_A + 2 * H_A * DV_A) // LANES
        in_specs = [_zspec(geo, kb, off), _zspec(geo, kb, off + H_B), _zspec(geo, kb, off + 2 * H_B),
                    _zspec(geo, kb, off + 3 * H_B),
                    pl.BlockSpec((DEPTH + 1, DK_B), lambda b, h, s: (0, h)), _head_spec(vd)]
        args = [z, z, z, z, extra["lb_logits"], extra["gn"].reshape(1, -1)]
        kern = functools.partial(_hgrn_kernel, layer=extra["layer"], **common)
    else:
        off = (2 * H_C * DK_C + 2 * H_C * DV_C) // LANES
        voff = (2 * H_C * DK_C + 2 * H_C * DV_C + 2 * H_D * DK_D) // vb
        in_specs = [_zspec(geo, kb, off), _zspec(geo, kb, off + H_D), _zspec(geo, vb, voff),
                    _zspec(geo, vb, voff + H_D),
                    pl.BlockSpec((geo["rows"], CD_TAIL), lambda b, h, s: (geo["rowblk"](b, s), CD_MAIN // CD_TAIL)),
                    pl.BlockSpec((CD_TAIL, DK_D), lambda b, h, s: (0, h)),
                    _head_spec(DK_D), _head_spec(vd)]
        args = [z, z, z, z, z, extra["w2"], extra["ab"].reshape(1, -1), extra["gn"].reshape(1, -1)]
        kern = functools.partial(_gla_kernel, **common)
    if kind != "ret":
        tile = min(geo["rows"], PAIR_TILE)
        mk = _pair_masks(tile, min(geo["blk"], tile))
        in_specs.append(_const_spec(mk.shape))
        args.append(mk)
    in_specs.append(_state_spec(geo, (kd, vd)))
    args.append(state)
    return _mixer_call(kern, geo, heads, vd, y_col, y_prev, z.shape[0], in_specs, args,
                       [_state_spec(geo, (kd, vd))], [jax.ShapeDtypeStruct(state.shape, F32)], [], kind)


def _mlstm_mixer(z, geo, c0, n0, m0, conv0, w, y_prev, y_col):
    n_seq = c0.shape[0]
    blk, nb = geo["blk"], geo["nseq_blk"]
    bias = jnp.zeros((1, LANES), F32).at[0, :H_C].set(w["i_bias"]).at[0, H_C:2 * H_C].set(w["f_bias"])
    kk = (H_C * DK_C) // DK_C
    kv = (2 * H_C * DK_C) // DV_C
    conv_spec = lambda col: pl.BlockSpec((nb, CONV_W - 1, DK_C), lambda b, h, s: (b, 0, col + h))
    in_specs = [
        _zspec(geo, DK_C, 0), _zspec(geo, DK_C, kk), _zspec(geo, DV_C, kv), _zspec(geo, DV_C, kv + H_C),
        pl.BlockSpec((geo["rows"], CD_TAIL), lambda b, h, s: (geo["rowblk"](b, s), CD_MAIN // CD_TAIL)),
        pl.BlockSpec((CONV_W, DK_C), lambda b, h, s: (0, h)),
        pl.BlockSpec((CONV_W, DK_C), lambda b, h, s: (0, H_C + h)),
        _const_spec((1, LANES)), _head_spec(DV_C),
        conv_spec(0), conv_spec(H_C),
        _state_spec(geo, (DK_C, DV_C)), _state_spec(geo, (1, DK_C)), _state_spec(geo, (1, LANES)),
    ]
    n4 = n0.reshape(n_seq, H_C, 1, DK_C)
    m4 = jnp.broadcast_to(m0[:, :, None, None], (n_seq, H_C, 1, LANES))
    args = [z, z, z, z, z, w["conv_w"], w["conv_w"], bias, w["gn_c"].reshape(1, -1),
            conv0, conv0, c0, n4, m4]
    conv_shape = jax.ShapeDtypeStruct((n_seq, CONV_W - 1, H_C * DK_C), F32)
    y, c_new, n_new, m_new, cq, ck = _mixer_call(
        functools.partial(_mlstm_kernel, blk=blk, nblk=geo["nblk"], per_seq=geo["per_seq"]),
        geo, H_C, DV_C, y_col, y_prev, z.shape[0], in_specs, args,
        [_state_spec(geo, (DK_C, DV_C)), _state_spec(geo, (1, DK_C)), _state_spec(geo, (1, LANES)),
         conv_spec(0), conv_spec(0)],
        [jax.ShapeDtypeStruct(c0.shape, F32), jax.ShapeDtypeStruct(n4.shape, F32),
         jax.ShapeDtypeStruct(m4.shape, F32), conv_shape, conv_shape],
        [pltpu.VMEM((geo["nblk"], SUBLANES + blk, DK_C), F32), pltpu.VMEM((geo["nblk"], SUBLANES + blk, DK_C), F32),
         pltpu.VMEM((SUBLANES, DK_C), F32), pltpu.VMEM((SUBLANES, DK_C), F32)],
        "mlstm")
    return (y, c_new, n_new.reshape(n_seq, H_C, DK_C), m_new[:, :, 0, 0],
            jnp.concatenate([cq, ck], axis=-1))


def _rope_tables(pos0, n):
    half = DK_A // 2
    inv = ROPE_BASE ** (-jnp.arange(half, dtype=F32) / half)
    ang = (jnp.arange(n, dtype=F32) + float(pos0))[:, None] * inv[None, :]
    cos, sin = jnp.cos(ang), jnp.sin(ang)
    return jnp.concatenate([cos, cos], axis=1), jnp.concatenate([-sin, sin], axis=1)


def kernel(x_prompt, x_sample, state_ret, state_hgrn, state_mlstm_c, state_mlstm_n, state_mlstm_m,
           state_mlstm_conv, state_gla, norm_gain, ffn_w1, ffn_w3, ffn_w2, ab_w_in, ab_w_out, ab_norm_a,
           ab_norm_b, lb_logits, cd_w_in, cd_w_out, cd_conv_w, cd_i_bias, cd_f_bias, cd_norm_c,
           cd_alpha_w2, cd_alpha_b, cd_norm_d, final_norm):
    bp, tp, _ = x_prompt.shape
    bs, ts, _ = x_sample.shape
    n_p, n_s = bp * tp, bs * ts
    x = jnp.concatenate([x_prompt.reshape(n_p, D_MODEL), x_sample.reshape(n_s, D_MODEL)], axis=0)
    w1, w3, w2 = ffn_w1.astype(BF16), ffn_w3.astype(BF16), ffn_w2.astype(BF16)
    groups = (("p", bp, _geometry(bp, tp, 0, SLAB), _geometry(bp, tp, 0, CHUNK)),
              ("s", bs, _geometry(bs, ts, n_p, SLAB), _geometry(bs, ts, n_p, CHUNK)))
    rope = {"p": _rope_tables(0, tp), "s": _rope_tables(PAST_LEN, ts)}
    zeros = lambda *shape: jnp.zeros(shape, F32)
    outs = {}

    def keep(name, tag, val):
        outs.setdefault(name + "_" + tag, []).append(val)

    for l in range(DEPTH):
        j = l // 2
        x, hn = _ffn(x, norm_gain[l, 0], w1, w3, w2, l, 0, post="norm_out", post_gain=norm_gain[l, 1])
        y = None
        if l % 2 == 0:
            z = _inproj(hn, ab_w_in[j].astype(BF16))
            for tag, nseq, geo_h, geo_k in groups:
                s_ret = zeros(nseq, H_A, DK_A, DV_A) if tag == "p" else state_ret[j]
                s_hg = zeros(nseq, H_B, DK_B, DV_B) if tag == "p" else state_hgrn[j]
                y, r_new = _decay_mixer("ret", z, geo_h, s_ret, dict(rope=rope[tag], gn=ab_norm_a[j]), y, 0)
                y, g_new = _decay_mixer("hgrn", z, geo_k, s_hg,
                                        dict(lb_logits=lb_logits, gn=ab_norm_b[j], layer=l), y, H_A * DV_A)
                keep("ret", tag, r_new)
                keep("hgrn", tag, g_new)
            x = _outproj(x, y, ab_w_out[j].astype(BF16))
        else:
            w_in = cd_w_in[j]
            gates = 2 * H_C
            main = 2 * H_C * DK_C + 2 * H_C * DV_C
            rest = CD_MAIN - main
            n_pad = -(-(CD_MAIN + CD_TAIL) // (N_SPLIT * LANES)) * (N_SPLIT * LANES)
            w_perm = _permute_cols(w_in, ((0, main), (main + gates, main + gates + rest),
                                          (main, main + gates), (main + gates + rest, w_in.shape[1])), n_pad)
            z = _inproj(hn, w_perm)
            w2p = jnp.zeros((CD_TAIL, H_D * DK_D), F32).at[gates:gates + GLA_RANK].set(cd_alpha_w2[j])
            wts = dict(conv_w=cd_conv_w[j], i_bias=cd_i_bias[j], f_bias=cd_f_bias[j], gn_c=cd_norm_c[j])
            for tag, nseq, geo_h, geo_k in groups:
                if tag == "p":
                    c0, n0, m0 = zeros(nseq, H_C, DK_C, DV_C), zeros(nseq, H_C, DK_C), zeros(nseq, H_C)
                    cv0, s_gl = zeros(nseq, CONV_W - 1, 2 * H_C * DK_C), zeros(nseq, H_D, DK_D, DV_D)
                else:
                    c0, n0, m0 = state_mlstm_c[j], state_mlstm_n[j], state_mlstm_m[j]
                    cv0, s_gl = state_mlstm_conv[j], state_gla[j]
                y, c_new, n_new, m_new, cv_new = _mlstm_mixer(z, geo_h, c0, n0, m0, cv0, wts, y, 0)
                y, gl_new = _decay_mixer("gla", z, geo_k, s_gl,
                                         dict(w2=w2p, ab=cd_alpha_b[j], gn=cd_norm_d[j]), y, H_C * DV_C)
                for name, val in (("c", c_new), ("n", n_new), ("m", m_new), ("conv", cv_new), ("gla", gl_new)):
                    keep(name, tag, val)
            x = _outproj(x, y, cd_w_out[j].astype(BF16))
        last = l == DEPTH - 1
        x = _ffn(x, norm_gain[l, 2], w1, w3, w2, l, 1, post="final" if last else None,
                 post_gain=final_norm if last else None)

    y_p = x[:n_p].reshape(bp, tp, D_MODEL)
    y_s = x[n_p:].reshape(bs, ts, D_MODEL)
    st = lambda name: jnp.stack(outs[name])
    return (y_p, y_s, st("ret_p"), st("ret_s"), st("hgrn_p"), st("hgrn_s"), st("c_p"), st("c_s"),
            st("n_p"), st("n_s"), st("m_p"), st("m_s"), st("conv_p"), st("conv_s"), st("gla_p"), st("gla_s"))
```

```python
import functools

import numpy as np
import jax
import jax.numpy as jnp
from jax import lax
from jax.experimental import pallas as pl
from jax.experimental.pallas import tpu as pltpu

F32 = jnp.float32
BF16 = jnp.bfloat16

D_MODEL = 2048
DEPTH = 2
PAST_LEN = 16384
H_A, DK_A, DV_A = 4, 128, 256
ROPE_BASE = 10000.0
H_B, DK_B, DV_B = 8, 128, 128
H_C, DK_C, DV_C = 4, 256, 256
CONV_W = 4
H_D, DK_D, DV_D = 4, 128, 256
GLA_RANK = 16
GLA_TAU = 16.0
FF_DIM = 5632
EPS = 1e-6

CD_MAIN = 2 * H_C * DK_C + 2 * H_C * DV_C + 2 * H_D * DK_D + 2 * H_D * DV_D
CD_TAIL = 128
MIX = H_A * DV_A + H_B * DV_B

LANES = 128
SUBLANES = 8
VMEM_LIMIT = 56 * 1024 * 1024

TM = 768
TM_WIDE = 1024
TM_IN = 1152
TM_OUT = 512
TF = 512
N_SPLIT = 4
CHUNK = 256
SLAB = 256
SEQ_PER_STEP = 32
PAIR_TILE = 128
LOG2E = 1.4426950408889634


def _cparams(sem):
    return pltpu.CompilerParams(dimension_semantics=sem, vmem_limit_bytes=VMEM_LIMIT)


def _rms(x, g):
    return x * lax.rsqrt(jnp.mean(x * x, axis=-1, keepdims=True) + EPS) * g


def _silu(x):
    return x * jax.nn.sigmoid(x)


def _log_sigmoid(x):
    return jnp.minimum(x, 0.0) - jnp.log1p(jnp.exp(-jnp.abs(x)))


def _pieces3(x):
    hi = x.astype(BF16).astype(F32)
    r1 = x - hi
    mid = r1.astype(BF16).astype(F32)
    lo = (r1 - mid).astype(BF16).astype(F32)
    return hi, mid, lo


def _dot(a, b):
    return jnp.dot(a, b, preferred_element_type=F32)


def _dot_nt(a, b):
    return lax.dot_general(a, b, (((1,), (1,)), ((), ())), preferred_element_type=F32)


def _dot_tn(a, b):
    return lax.dot_general(a, b, (((0,), (0,)), ((), ())), preferred_element_type=F32)


def _cat(parts, axis=0):
    return parts[0] if len(parts) == 1 else jnp.concatenate(parts, axis=axis)


def _prefix(g, blk):
    rows, n = g.shape
    row = lax.broadcasted_iota(jnp.int32, (SUBLANES, n), 0)
    outs = []
    off = None
    for r0 in range(0, rows, SUBLANES):
        x = g[r0:r0 + SUBLANES]
        for sh in (1, 2, 4):
            x = x + jnp.where(row >= sh, pltpu.roll(x, sh, 0), 0.0)
        if r0 % blk:
            x = x + off
        off = x[SUBLANES - 1:SUBLANES]
        outs.append(x)
    return _cat(outs)


def _block_last(x, blk):
    rows, n = x.shape
    return _cat([jnp.broadcast_to(x[r0 + blk - 1:r0 + blk], (blk, n)) for r0 in range(0, rows, blk)])


def _block_mask(rows, blk):
    tt = lax.broadcasted_iota(jnp.int32, (rows, rows), 0)
    ss = lax.broadcasted_iota(jnp.int32, (rows, rows), 1)
    mask = ss <= tt
    if blk < rows:
        sh = blk.bit_length() - 1
        mask = jnp.logical_and(mask, lax.shift_right_logical(tt, sh) == lax.shift_right_logical(ss, sh))
    return mask, tt - ss


def _row_bcast(col):
    c = col.shape[0]
    lane = lax.broadcasted_iota(jnp.int32, (c, LANES), 1)
    pieces = _pieces3(jnp.where(lane == 0, jnp.broadcast_to(col, (c, LANES)), 0.0))
    rm = jnp.concatenate(pieces, axis=1).astype(BF16)
    ones = jnp.ones((c, 3 * LANES), BF16)
    return _dot_nt(ones, rm)


def _levels(blk):
    out = []
    m = 1
    while m < blk:
        out.append(m)
        m *= 2
    return tuple(out)


def _pair_masks(rows, blk):
    t = np.arange(rows)[:, None]
    u = np.arange(rows)[None, :]
    masks = [t == u] + [((t % (2 * m)) >= m) & ((u % (2 * m)) < m) & (t // (2 * m) == u // (2 * m))
                        for m in _levels(blk)]
    return jnp.asarray(np.stack(masks).astype(np.float32))


def _level_exponent(b, g, m):
    rows, n = b.shape
    if m >= SUBLANES:
        mid = _cat([jnp.broadcast_to(b[r0 + m - 1:r0 + m], (2 * m, n)) for r0 in range(0, rows, 2 * m)])
        return -jnp.abs(b - mid)
    row = lax.broadcasted_iota(jnp.int32, (SUBLANES, n), 0)
    outs = []
    for r0 in range(0, rows, SUBLANES):
        x = b[r0:r0 + SUBLANES]
        if m == 1:
            outs.append(jnp.where(jnp.bitwise_and(row, 1) == 1, g[r0:r0 + SUBLANES], 0.0))
            continue
        if m == 2:
            mid = jnp.where(row < 4, jnp.broadcast_to(x[1:2], x.shape), jnp.broadcast_to(x[5:6], x.shape))
        else:
            mid = jnp.broadcast_to(x[3:4], x.shape)
        outs.append(-jnp.abs(x - mid))
    return _cat(outs)


def _state_update(kw, v, e_row, s):
    c, kd = kw.shape
    vd = v.shape[1]
    pad = 2 * SUBLANES
    lhs = jnp.concatenate(_pieces3(e_row) + (jnp.zeros((pad - 3, kd), F32), kw), axis=0)
    sel = (lax.broadcasted_iota(jnp.int32, (pad, LANES), 0) < 3).astype(F32)
    rhs = jnp.concatenate(
        [jnp.concatenate([jnp.zeros((pad, vd), F32), sel], axis=1),
         jnp.concatenate([v, jnp.zeros((c, LANES), F32)], axis=1)], axis=0)
    upd = _dot_tn(lhs.astype(BF16), rhs.astype(BF16))
    e_col = upd[:, vd:]
    if vd > LANES:
        e_col = jnp.concatenate([e_col] * (vd // LANES), axis=1)
    return e_col * s + upd[:, :vd]


def _head_out(o, gn, gate_act):
    return o * lax.rsqrt(jnp.mean(o * o, axis=-1, keepdims=True) + EPS) * gn * gate_act


def _ffn_kernel(*refs, nf, post):
    x_ref, g_ref, w1_ref, w3_ref, w2_ref = refs[:5]
    pg_ref = refs[5] if post else None
    outs = refs[6 if post else 5:-1]
    xn_ref = refs[-1]
    acc_ref = outs[0]
    f = pl.program_id(1)

    @pl.when(f == 0)
    def _():
        xn_ref[...] = _rms(x_ref[...], g_ref[...]).astype(BF16)
        acc_ref[...] = jnp.zeros_like(acc_ref)

    xn = xn_ref[...]
    h1 = _dot(xn, w1_ref[...])
    h3 = _dot(xn, w3_ref[...])
    acc_ref[...] += _dot((_silu(h1) * h3).astype(BF16), w2_ref[...])

    @pl.when(f == nf - 1)
    def _():
        y = x_ref[...] + 0.5 * acc_ref[...]
        if post == "final":
            outs[0][...] = _rms(y, pg_ref[...])
        else:
            outs[0][...] = y
            if post == "norm_out":
                outs[1][...] = _rms(y, pg_ref[...]).astype(BF16)


def _ffn(x, gain, w1, w3, w2, l, idx, post=None, post_gain=None):
    m = x.shape[0]
    nf = FF_DIM // TF
    tm = TM if post else TM_WIDE
    in_specs = [
        pl.BlockSpec((tm, D_MODEL), lambda i, f: (i, 0)),
        pl.BlockSpec((1, D_MODEL), lambda i, f: (0, 0)),
        pl.BlockSpec((None, None, D_MODEL, TF), lambda i, f: (l, idx, 0, f)),
        pl.BlockSpec((None, None, D_MODEL, TF), lambda i, f: (l, idx, 0, f)),
        pl.BlockSpec((None, None, TF, D_MODEL), lambda i, f: (l, idx, f, 0)),
    ]
    args = [x, gain.reshape(1, D_MODEL), w1, w3, w2]
    if post:
        in_specs.append(pl.BlockSpec((1, D_MODEL), lambda i, f: (0, 0)))
        args.append(post_gain.reshape(1, D_MODEL))
    row_spec = pl.BlockSpec((tm, D_MODEL), lambda i, f: (i, 0))
    out_specs, out_shape = [row_spec], [jax.ShapeDtypeStruct((m, D_MODEL), F32)]
    if post == "norm_out":
        out_specs.append(row_spec)
        out_shape.append(jax.ShapeDtypeStruct((m, D_MODEL), BF16))
    res = pl.pallas_call(
        functools.partial(_ffn_kernel, nf=nf, post=post),
        grid=(m // tm, nf),
        in_specs=in_specs,
        out_specs=out_specs,
        out_shape=out_shape,
        scratch_shapes=[pltpu.VMEM((tm, D_MODEL), BF16)],
        compiler_params=_cparams(("parallel", "arbitrary")),
        name="ffn_" + post if post else "ffn",
    )(*args)
    return res if post == "norm_out" else res[0]


def _permute_cols_kernel(w_ref, o_ref, *, pieces):
    x = w_ref[...]
    parts = [x[:, a:b] for a, b in pieces]
    width = sum(b - a for a, b in pieces)
    parts.append(jnp.zeros((x.shape[0], o_ref.shape[1] - width), x.dtype))
    o_ref[...] = jnp.concatenate(parts, axis=1).astype(o_ref.dtype)


def _permute_cols(w, j, pieces, n_out, rows_per_step=256):
    k = w.shape[1]
    return pl.pallas_call(
        functools.partial(_permute_cols_kernel, pieces=pieces),
        grid=(k // rows_per_step,),
        in_specs=[pl.BlockSpec((None, rows_per_step, w.shape[2]), lambda i: (j, i, 0))],
        out_specs=pl.BlockSpec((rows_per_step, n_out), lambda i: (i, 0)),
        out_shape=jax.ShapeDtypeStruct((k, n_out), BF16),
        compiler_params=_cparams(("parallel",)),
        name="permute_cols",
    )(w)


def _inproj_kernel(h_ref, w_ref, z_ref):
    z_ref[...] = _dot(h_ref[...], w_ref[...])


def _inproj(h, w):
    m, n = h.shape[0], w.shape[1]
    tn = n // N_SPLIT
    return pl.pallas_call(
        _inproj_kernel,
        grid=(N_SPLIT, m // TM_IN),
        in_specs=[
            pl.BlockSpec((TM_IN, D_MODEL), lambda j, i: (i, 0)),
            pl.BlockSpec((D_MODEL, tn), lambda j, i: (0, j)),
        ],
        out_specs=pl.BlockSpec((TM_IN, tn), lambda j, i: (i, j)),
        out_shape=jax.ShapeDtypeStruct((m, n), F32),
        compiler_params=_cparams(("parallel", "parallel")),
        name="inproj",
    )(h, w)


def _outproj_kernel(x_ref, y_ref, w_ref, o_ref):
    o_ref[...] = x_ref[...] + _dot(y_ref[...].astype(BF16), w_ref[...])


def _outproj(x, y, w):
    m, kdim = y.shape
    return pl.pallas_call(
        _outproj_kernel,
        grid=(m // TM_OUT,),
        in_specs=[
            pl.BlockSpec((TM_OUT, D_MODEL), lambda i: (i, 0)),
            pl.BlockSpec((TM_OUT, kdim), lambda i: (i, 0)),
            pl.BlockSpec((kdim, D_MODEL), lambda i: (0, 0)),
        ],
        out_specs=pl.BlockSpec((TM_OUT, D_MODEL), lambda i: (i, 0)),
        out_shape=jax.ShapeDtypeStruct((m, D_MODEL), F32),
        compiler_params=_cparams(("parallel",)),
        name="outproj",
    )(x, y, w)


def _states(per_seq, nblk, ins, outs):
    if per_seq:
        def get(i):
            return tuple(r[i, 0] for r in ins)

        def put(i, vals):
            for o, val in zip(outs, vals):
                o[i, 0] = val
        return get, put

    @pl.when(pl.program_id(2) == 0)
    def _():
        for o, r in zip(outs, ins):
            o[...] = r[...]

    carried = {"v": tuple(o[0, 0] for o in outs)}

    def get(i):
        return carried["v"]

    def put(i, vals):
        carried["v"] = tuple(vals)
        if i == nblk - 1:
            for o, val in zip(outs, vals):
                o[0, 0] = val
    return get, put


def _ret_kernel(q_ref, k_ref, v_ref, gt_ref, cos_ref, sin_ref, gn_ref, s_in, *rest, blk, nblk, per_seq):
    y_ref, s_out = rest[-2:]
    rows = blk * nblk
    hf = jnp.full((1, 1), pl.program_id(1), jnp.int32).astype(F32)
    lg = jnp.log1p(-jnp.exp2(-5.0 - hf))
    cos = _cat([cos_ref[...]] * (rows // cos_ref.shape[0]))
    sin = _cat([sin_ref[...]] * (rows // sin_ref.shape[0]))
    q = q_ref[...]
    k = k_ref[...]
    q = q * cos + pltpu.roll(q, DK_A // 2, 1) * sin
    k = (k * cos + pltpu.roll(k, DK_A // 2, 1) * sin) * (DK_A ** -0.5)
    vb = v_ref[...].astype(BF16)
    mask, dist = _block_mask(rows, blk)
    dec = jnp.where(mask, jnp.exp(jnp.maximum(dist, 0).astype(F32) * lg), 0.0)
    a = _dot_nt(q.astype(BF16), k.astype(BF16)) * dec
    o = _dot(a.astype(BF16), vb)
    tloc = jnp.bitwise_and(lax.broadcasted_iota(jnp.int32, (rows, 1), 0), blk - 1).astype(F32)
    qt = (q * jnp.exp((tloc + 1.0) * lg)).astype(BF16)
    kw = (k * jnp.exp((blk - 1.0 - tloc) * lg)).astype(BF16)
    e_blk = jnp.exp(blk * lg)
    get, put = _states(per_seq, nblk, (s_in,), (s_out,))
    parts = []
    for i in range(nblk):
        r = slice(i * blk, (i + 1) * blk)
        s, = get(i)
        parts.append(o[r] + _dot(qt[r], s.astype(BF16)))
        put(i, (e_blk * s + _dot_tn(kw[r], vb[r]),))
    y_ref[...] = _head_out(_cat(parts), gn_ref[...], _silu(gt_ref[...]))


def _key_decay_slab(q, k, v, g, mk_ref, blk, nblk, get, put):
    rows = q.shape[0]
    tile = min(rows, PAIR_TILE)
    tiles = [slice(i * tile, (i + 1) * tile) for i in range(rows // tile)]
    g2 = g * LOG2E
    b = _prefix(g2, blk)
    b_last = _block_last(b, blk)
    qt = (q * jnp.exp2(b)).astype(BF16)
    kw = k * jnp.exp2(b_last - b)
    e_last = jnp.exp2(b_last)
    vb = v.astype(BF16)
    qb, kb = q.astype(BF16), k.astype(BF16)
    a = {(i, i): mk_ref[0] * _dot_nt(qb[t], kb[t]) for i, t in enumerate(tiles)}
    for li, m in enumerate(_levels(blk)):
        w = jnp.exp2(_level_exponent(b, g2, m))
        ql, kl = (q * w).astype(BF16), (k * w).astype(BF16)
        if m < tile:
            for i, t in enumerate(tiles):
                a[i, i] = a[i, i] + mk_ref[li + 1] * _dot_nt(ql[t], kl[t])
        else:
            for r0 in range(0, rows, 2 * m):
                for i in range((r0 + m) // tile, (r0 + 2 * m) // tile):
                    for j in range(r0 // tile, (r0 + m) // tile):
                        a[i, j] = _dot_nt(ql[tiles[i]], kl[tiles[j]])
    o_tiles = []
    for i in range(len(tiles)):
        terms = [_dot(a[i, j].astype(BF16), vb[tiles[j]]) for j in range(len(tiles)) if (i, j) in a]
        o_tiles.append(functools.reduce(lambda x, y: x + y, terms))
    o = _cat(o_tiles)
    parts = []
    for i in range(nblk):
        r = slice(i * blk, (i + 1) * blk)
        s, = get(i)
        parts.append(o[r] + _dot(qt[r], s.astype(BF16)))
        put(i, (_state_update(kw[r], v[r], e_last[i * blk:i * blk + 1], s),))
    return _cat(parts)


def _hgrn_kernel(q_ref, f_ref, i_ref, gt_ref, lb_ref, gn_ref, *rest, blk, nblk, per_seq, layer):
    mk_ref, s_in = rest[:2]
    y_ref, s_out = rest[-2:]
    lbl = lb_ref[...]
    e = jnp.exp(lbl - jnp.max(lbl, axis=0, keepdims=True))
    lb = jnp.sum(e[0:layer + 1], axis=0, keepdims=True) / jnp.sum(e, axis=0, keepdims=True)
    get, put = _states(per_seq, nblk, (s_in,), (s_out,))
    k = (1.0 - lb) * (1.0 - jax.nn.sigmoid(f_ref[...]))
    o = _key_decay_slab(_silu(q_ref[...]), k, i_ref[...], jnp.log(1.0 - k), mk_ref, blk, nblk, get, put)
    y_ref[...] = _head_out(o, gn_ref[...], _silu(gt_ref[...]))


def _gla_kernel(q_ref, k_ref, v_ref, gt_ref, tail_ref, w2_ref, ab_ref, gn_ref, *rest, blk, nblk, per_seq):
    mk_ref, s_in = rest[:2]
    y_ref, s_out = rest[-2:]
    wh, wm, wl = (p.astype(BF16) for p in _pieces3(w2_ref[...]))
    get, put = _states(per_seq, nblk, (s_in,), (s_out,))
    th, tm_, tl = (p.astype(BF16) for p in _pieces3(tail_ref[...]))
    x = (_dot(th, wh) + (_dot(th, wm) + _dot(tm_, wh))
         + (_dot(th, wl) + _dot(tm_, wm) + _dot(tl, wh)))
    g = _log_sigmoid(x + ab_ref[...]) / GLA_TAU
    o = _key_decay_slab(q_ref[...] * (DK_D ** -0.5), k_ref[...], v_ref[...], g, mk_ref, blk, nblk, get, put)
    y_ref[...] = _head_out(o, gn_ref[...], _silu(gt_ref[...]))


def _mlstm_kernel(q_ref, k_ref, v_ref, og_ref, tail_ref, wq_ref, wk_ref, bias_ref, gn_ref,
                  cq_in, ck_in, c_in, n_in, m_in, *rest, blk, nblk, per_seq):
    y_ref, c_out, n_out, m_out, cq_out, ck_out, uq_ref, uk_ref, hq_ref, hk_ref = rest[-10:]
    assert per_seq or nblk == 1
    rows = blk * nblk
    h = pl.program_id(1)
    hist = CONV_W - 1
    lo = SUBLANES - hist

    if not per_seq:
        @pl.when(pl.program_id(2) == 0)
        def _():
            for h_ref, cin in ((hq_ref, cq_in), (hk_ref, ck_in)):
                h_ref[...] = jnp.zeros_like(h_ref)
                h_ref[lo:SUBLANES, :] = cin[0]

    def conv(u_ref, h_ref, cin, cout, raw_ref, w_ref):
        parts = []
        for i in range(nblk):
            raw = raw_ref[i * blk:(i + 1) * blk, :]
            if per_seq:
                u_ref[i, lo:SUBLANES, :] = cin[i]
            else:
                u_ref[i, 0:SUBLANES, :] = h_ref[...]
                h_ref[...] = raw[blk - SUBLANES:blk]
            u_ref[i, SUBLANES:SUBLANES + blk, :] = raw
            cout[i] = raw[blk - hist:blk]
            acc = u_ref[i, lo:lo + blk, :] * w_ref[0:1, :]
            for j in range(1, CONV_W):
                acc = acc + u_ref[i, lo + j:lo + j + blk, :] * w_ref[j:j + 1, :]
            parts.append(acc)
        return _silu(_cat(parts))

    q = conv(uq_ref, hq_ref, cq_in, cq_out, q_ref, wq_ref)
    k = conv(uk_ref, hk_ref, ck_in, ck_out, k_ref, wk_ref) * (DK_C ** -0.5)
    qb = q.astype(BF16)
    vb = v_ref[...].astype(BF16)

    lane = lax.broadcasted_iota(jnp.int32, (rows, LANES), 1)
    tl = tail_ref[...] + bias_ref[...]
    li = jnp.sum(jnp.where(lane == h, tl, 0.0), axis=-1, keepdims=True)
    lf = jnp.sum(jnp.where(lane == H_C + h, _log_sigmoid(tl), 0.0), axis=-1, keepdims=True)
    b = _prefix(jnp.broadcast_to(lf, (rows, LANES)), blk)[:, :1]

    get, put = _states(per_seq, nblk, (c_in, n_in, m_in), (c_out, n_out, m_out))
    states = [get(i) for i in range(nblk)] if per_seq else [get(0)]
    m_prev = _cat([jnp.broadcast_to(st[2][:, :1], (blk, 1)) for st in states])

    mask, _ = _block_mask(rows, blk)
    d = jnp.where(mask, b + _row_bcast(li - b), -jnp.inf)
    m_inter = b + m_prev
    m_t = jnp.maximum(m_inter, jnp.max(d, axis=-1, keepdims=True))
    w_inter = jnp.exp(m_inter - m_t)
    sc = _dot_nt(qb, k.astype(BF16)) * jnp.exp(d - m_t)
    num = _dot(sc.astype(BF16), vb)
    den = jnp.sum(sc, axis=-1, keepdims=True)
    m_new = _block_last(m_t, blk)
    b_last = _block_last(b, blk)
    kw = k * jnp.exp(b_last - b + li - m_new)
    w_c = jnp.exp(b_last + m_prev - m_new)
    nums, dens = [], []
    for i in range(nblk):
        r = slice(i * blk, (i + 1) * blk)
        cst, nrow, _ = states[i]
        nums.append(num[r] + w_inter[r] * _dot(qb[r], cst.astype(BF16)))
        dens.append(den[r] + w_inter[r] * jnp.sum(q[r] * nrow, axis=-1, keepdims=True))
        wc = w_c[i * blk:i * blk + 1]
        put(i, (wc * cst + _dot_tn(kw[r].astype(BF16), vb[r]),
                wc * nrow + jnp.sum(kw[r], axis=0, keepdims=True),
                jnp.broadcast_to(m_new[i * blk:i * blk + 1], (1, LANES))))
    hh = _cat(nums) / jnp.maximum(jnp.abs(_cat(dens)), jnp.exp(-m_t))
    y_ref[...] = _head_out(hh, gn_ref[...], jax.nn.sigmoid(og_ref[...]))


def _geometry(n_seq, seq_len, row0, blk_prompt):
    if seq_len >= SLAB:
        blk, rows, per_seq, nseq_blk = blk_prompt, SLAB, False, 1
        grid_seq, grid_slab = n_seq, seq_len // SLAB
    else:
        blk, nseq_blk, per_seq = seq_len, SEQ_PER_STEP, True
        rows = nseq_blk * blk
        grid_seq, grid_slab = n_seq // nseq_blk, 1
    base = row0 // rows
    return dict(blk=blk, rows=rows, nblk=rows // blk, per_seq=per_seq, nseq_blk=nseq_blk,
                grid_seq=grid_seq, grid_slab=grid_slab, rowblk=lambda b, s: base + b * grid_slab + s)


def _zspec(geo, width, col):
    rb = geo["rowblk"]
    return pl.BlockSpec((geo["rows"], width), lambda b, h, s: (rb(b, s), col + h))


def _const_spec(shape):
    return pl.BlockSpec(shape, lambda b, h, s: (0,) * len(shape))


def _head_spec(width, col=0):
    return pl.BlockSpec((1, width), lambda b, h, s: (0, col + h))


def _state_spec(geo, shape):
    return pl.BlockSpec((geo["nseq_blk"], 1) + shape, lambda b, h, s: (b, h, 0, 0))


def _mixer_call(kern, geo, heads, vd, y_col, y_prev, n_tok, in_specs, args, extra_specs, extra_shapes,
                scratch, name):
    if y_prev is not None:
        in_specs = in_specs + [pl.BlockSpec(memory_space=pl.ANY)]
        args = args + [y_prev]
    rb = geo["rowblk"]
    return pl.pallas_call(
        kern,
        grid=(geo["grid_seq"], heads, geo["grid_slab"]),
        in_specs=in_specs,
        out_specs=[pl.BlockSpec((geo["rows"], vd), lambda b, h, s: (rb(b, s), y_col // vd + h))] + extra_specs,
        out_shape=[jax.ShapeDtypeStruct((n_tok, MIX), F32)] + extra_shapes,
        input_output_aliases={} if y_prev is None else {len(args) - 1: 0},
        scratch_shapes=scratch,
        compiler_params=_cparams(("parallel", "parallel", "arbitrary")),
        name=name + ("_sample" if geo["per_seq"] else "_prompt"),
    )(*args)


def _decay_mixer(kind, z, geo, state, extra, y_prev, y_col):
    heads, kd, vd = {"ret": (H_A, DK_A, DV_A), "hgrn": (H_B, DK_B, DV_B), "gla": (H_D, DK_D, DV_D)}[kind]
    common = dict(blk=geo["blk"], nblk=geo["nblk"], per_seq=geo["per_seq"])
    kb, vb = LANES, vd
    if kind == "ret":
        cos, sin = extra["rope"]
        tab = pl.BlockSpec((geo["blk"] if geo["per_seq"] else geo["rows"], DK_A),
                           lambda b, h, s: (0 if geo["per_seq"] else s, 0))
        in_specs = [_zspec(geo, kb, 0), _zspec(geo, kb, H_A), _zspec(geo, vb, (2 * H_A * DK_A) // vb),
                    _zspec(geo, vb, (2 * H_A * DK_A) // vb + H_A), tab, tab, _head_spec(vd)]
        args = [z, z, z, z, cos, sin, extra["gn"].reshape(1, -1)]
        kern = functools.partial(_ret_kernel, **common)
    elif kind == "hgrn":
        off = (2 * H_A * DK_A + 2 * H_A * DV_A) // LANES
        in_specs = [_zspec(geo, kb, off), _zspec(geo, kb, off + H_B), _zspec(geo, kb, off + 2 * H_B),
                    _zspec(geo, kb, off + 3 * H_B),
                    pl.BlockSpec((DEPTH + 1, DK_B), lambda b, h, s: (0, h)), _head_spec(vd)]
        args = [z, z, z, z, extra["lb_logits"], extra["gn"].reshape(1, -1)]
        kern = functools.partial(_hgrn_kernel, layer=extra["layer"], **common)
    else:
        off = (2 * H_C * DK_C + 2 * H_C * DV_C) // LANES
        voff = (2 * H_C * DK_C + 2 * H_C * DV_C + 2 * H_D * DK_D) // vb
        in_specs = [_zspec(geo, kb, off), _zspec(geo, kb, off + H_D), _zspec(geo, vb, voff),
                    _zspec(geo, vb, voff + H_D),
                    pl.BlockSpec((geo["rows"], CD_TAIL), lambda b, h, s: (geo["rowblk"](b, s), CD_MAIN // CD_TAIL)),
                    pl.BlockSpec((CD_TAIL, DK_D), lambda b, h, s: (0, h)),
                    _head_spec(DK_D), _head_spec(vd)]
        args = [z, z, z, z, z, extra["w2"], extra["ab"].reshape(1, -1), extra["gn"].reshape(1, -1)]
        kern = functools.partial(_gla_kernel, **common)
    if kind != "ret":
        tile = min(geo["rows"], PAIR_TILE)
        mk = _pair_masks(tile, min(geo["blk"], tile))
        in_specs.append(_const_spec(mk.shape))
        args.append(mk)
    in_specs.append(_state_spec(geo, (kd, vd)))
    args.append(state)
    return _mixer_call(kern, geo, heads, vd, y_col, y_prev, z.shape[0], in_specs, args,
                       [_state_spec(geo, (kd, vd))], [jax.ShapeDtypeStruct(state.shape, F32)], [], kind)


def _mlstm_mixer(z, geo, c0, n0, m0, conv0, w, y_prev, y_col):
    n_seq = c0.shape[0]
    blk, nb = geo["blk"], geo["nseq_blk"]
    bias = jnp.zeros((1, LANES), F32).at[0, :H_C].set(w["i_bias"]).at[0, H_C:2 * H_C].set(w["f_bias"])
    kk = (H_C * DK_C) // DK_C
    kv = (2 * H_C * DK_C) // DV_C
    conv_spec = lambda col: pl.BlockSpec((nb, CONV_W - 1, DK_C), lambda b, h, s: (b, 0, col + h))
    in_specs = [
        _zspec(geo, DK_C, 0), _zspec(geo, DK_C, kk), _zspec(geo, DV_C, kv), _zspec(geo, DV_C, kv + H_C),
        pl.BlockSpec((geo["rows"], CD_TAIL), lambda b, h, s: (geo["rowblk"](b, s), CD_MAIN // CD_TAIL)),
        pl.BlockSpec((CONV_W, DK_C), lambda b, h, s: (0, h)),
        pl.BlockSpec((CONV_W, DK_C), lambda b, h, s: (0, H_C + h)),
        _const_spec((1, LANES)), _head_spec(DV_C),
        conv_spec(0), conv_spec(H_C),
        _state_spec(geo, (DK_C, DV_C)), _state_spec(geo, (1, DK_C)), _state_spec(geo, (1, LANES)),
    ]
    n4 = n0.reshape(n_seq, H_C, 1, DK_C)
    m4 = jnp.broadcast_to(m0[:, :, None, None], (n_seq, H_C, 1, LANES))
    args = [z, z, z, z, z, w["conv_w"], w["conv_w"], bias, w["gn_c"].reshape(1, -1),
            conv0, conv0, c0, n4, m4]
    conv_shape = jax.ShapeDtypeStruct((n_seq, CONV_W - 1, H_C * DK_C), F32)
    y, c_new, n_new, m_new, cq, ck = _mixer_call(
        functools.partial(_mlstm_kernel, blk=blk, nblk=geo["nblk"], per_seq=geo["per_seq"]),
        geo, H_C, DV_C, y_col, y_prev, z.shape[0], in_specs, args,
        [_state_spec(geo, (DK_C, DV_C)), _state_spec(geo, (1, DK_C)), _state_spec(geo, (1, LANES)),
         conv_spec(0), conv_spec(0)],
        [jax.ShapeDtypeStruct(c0.shape, F32), jax.ShapeDtypeStruct(n4.shape, F32),
         jax.ShapeDtypeStruct(m4.shape, F32), conv_shape, conv_shape],
        [pltpu.VMEM((geo["nblk"], SUBLANES + blk, DK_C), F32), pltpu.VMEM((geo["nblk"], SUBLANES + blk, DK_C), F32),
         pltpu.VMEM((SUBLANES, DK_C), F32), pltpu.VMEM((SUBLANES, DK_C), F32)],
        "mlstm")
    return (y, c_new, n_new.reshape(n_seq, H_C, DK_C), m_new[:, :, 0, 0],
            jnp.concatenate([cq, ck], axis=-1))


def _rope_tables(pos0, n):
    half = DK_A // 2
    inv = ROPE_BASE ** (-jnp.arange(half, dtype=F32) / half)
    ang = (jnp.arange(n, dtype=F32) + float(pos0))[:, None] * inv[None, :]
    cos, sin = jnp.cos(ang), jnp.sin(ang)
    return jnp.concatenate([cos, cos], axis=1), jnp.concatenate([-sin, sin], axis=1)


def kernel(x_prompt, x_sample, state_ret, state_hgrn, state_mlstm_c, state_mlstm_n, state_mlstm_m,
           state_mlstm_conv, state_gla, norm_gain, ffn_w1, ffn_w3, ffn_w2, ab_w_in, ab_w_out, ab_norm_a,
           ab_norm_b, lb_logits, cd_w_in, cd_w_out, cd_conv_w, cd_i_bias, cd_f_bias, cd_norm_c,
           cd_alpha_w2, cd_alpha_b, cd_norm_d, final_norm):
    bp, tp, _ = x_prompt.shape
    bs, ts, _ = x_sample.shape
    n_p, n_s = bp * tp, bs * ts
    x = jnp.concatenate([x_prompt.reshape(n_p, D_MODEL), x_sample.reshape(n_s, D_MODEL)], axis=0)
    w1, w3, w2 = ffn_w1.astype(BF16), ffn_w3.astype(BF16), ffn_w2.astype(BF16)
    groups = (("p", bp, _geometry(bp, tp, 0, SLAB), _geometry(bp, tp, 0, CHUNK)),
              ("s", bs, _geometry(bs, ts, n_p, SLAB), _geometry(bs, ts, n_p, CHUNK)))
    rope = {"p": _rope_tables(0, tp), "s": _rope_tables(PAST_LEN, ts)}
    zeros = lambda *shape: jnp.zeros(shape, F32)
    outs = {}

    def keep(name, tag, val):
        outs.setdefault(name + "_" + tag, []).append(val)

    for l in range(DEPTH):
        j = l // 2
        x, hn = _ffn(x, norm_gain[l, 0], w1, w3, w2, l, 0, post="norm_out", post_gain=norm_gain[l, 1])
        y = None
        if l % 2 == 0:
            z = _inproj(hn, ab_w_in[j].astype(BF16))
            for tag, nseq, geo_h, geo_k in groups:
                s_ret = zeros(nseq, H_A, DK_A, DV_A) if tag == "p" else state_ret[j]
                s_hg = zeros(nseq, H_B, DK_B, DV_B) if tag == "p" else state_hgrn[j]
                y, r_new = _decay_mixer("ret", z, geo_h, s_ret, dict(rope=rope[tag], gn=ab_norm_a[j]), y, 0)
                y, g_new = _decay_mixer("hgrn", z, geo_k, s_hg,
                                        dict(lb_logits=lb_logits, gn=ab_norm_b[j], layer=l), y, H_A * DV_A)
                keep("ret", tag, r_new)
                keep("hgrn", tag, g_new)
            x = _outproj(x, y, ab_w_out[j].astype(BF16))
        else:
            gates = 2 * H_C
            main = 2 * H_C * DK_C + 2 * H_C * DV_C
            rest = CD_MAIN - main
            n_pad = -(-(CD_MAIN + CD_TAIL) // (N_SPLIT * LANES)) * (N_SPLIT * LANES)
            w_perm = _permute_cols(cd_w_in, j, ((0, main), (main + gates, main + gates + rest),
                                                (main, main + gates), (main + gates + rest, cd_w_in.shape[2])), n_pad)
            z = _inproj(hn, w_perm)
            w2p = jnp.zeros((CD_TAIL, H_D * DK_D), F32).at[gates:gates + GLA_RANK].set(cd_alpha_w2[j])
            wts = dict(conv_w=cd_conv_w[j], i_bias=cd_i_bias[j], f_bias=cd_f_bias[j], gn_c=cd_norm_c[j])
            for tag, nseq, geo_h, geo_k in groups:
                if tag == "p":
                    c0, n0, m0 = zeros(nseq, H_C, DK_C, DV_C), zeros(nseq, H_C, DK_C), zeros(nseq, H_C)
                    cv0, s_gl = zeros(nseq, CONV_W - 1, 2 * H_C * DK_C), zeros(nseq, H_D, DK_D, DV_D)
                else:
                    c0, n0, m0 = state_mlstm_c[j], state_mlstm_n[j], state_mlstm_m[j]
                    cv0, s_gl = state_mlstm_conv[j], state_gla[j]
                y, c_new, n_new, m_new, cv_new = _mlstm_mixer(z, geo_h, c0, n0, m0, cv0, wts, y, 0)
                y, gl_new = _decay_mixer("gla", z, geo_k, s_gl,
                                         dict(w2=w2p, ab=cd_alpha_b[j], gn=cd_norm_d[j]), y, H_C * DV_C)
                for name, val in (("c", c_new), ("n", n_new), ("m", m_new), ("conv", cv_new), ("gla", gl_new)):
                    keep(name, tag, val)
            x = _outproj(x, y, cd_w_out[j].astype(BF16))
        last = l == DEPTH - 1
        x = _ffn(x, norm_gain[l, 2], w1, w3, w2, l, 1, post="final" if last else None,
                 post_gain=final_norm if last else None)

    y_p = x[:n_p].reshape(bp, tp, D_MODEL)
    y_s = x[n_p:].reshape(bs, ts, D_MODEL)
    st = lambda name: jnp.stack(outs[name])
    return (y_p, y_s, st("ret_p"), st("ret_s"), st("hgrn_p"), st("hgrn_s"), st("c_p"), st("c_s"),
            st("n_p"), st("n_s"), st("m_p"), st("m_s"), st("conv_p"), st("conv_s"), st("gla_p"), st("gla_s"))
```

```python
import functools

import numpy as np
import jax
import jax.numpy as jnp
from jax import lax
from jax.experimental import pallas as pl
from jax.experimental.pallas import tpu as pltpu

F32 = jnp.float32
BF16 = jnp.bfloat16

D_MODEL = 2048
DEPTH = 2
PAST_LEN = 16384
H_A, DK_A, DV_A = 4, 128, 256
ROPE_BASE = 10000.0
H_B, DK_B, DV_B = 8, 128, 128
H_C, DK_C, DV_C = 4, 256, 256
CONV_W = 4
H_D, DK_D, DV_D = 4, 128, 256
GLA_RANK = 16
GLA_TAU = 16.0
FF_DIM = 5632
EPS = 1e-6

CD_MAIN = 2 * H_C * DK_C + 2 * H_C * DV_C + 2 * H_D * DK_D + 2 * H_D * DV_D
CD_TAIL = 128
MIX = H_A * DV_A + H_B * DV_B

LANES = 128
SUBLANES = 8
VMEM_LIMIT = 56 * 1024 * 1024

TM = 768
TM_WIDE = 1024
TM_IN = 1152
TM_OUT = 512
TF = 512
N_SPLIT = 4
PROMPT_SLAB = {"ret": (512, 512), "hgrn": (512, 256), "mlstm": (256, 256), "gla": (512, 256)}
SEQ_PER_STEP = 32
PAIR_TILE = 128
LOG2E = 1.4426950408889634


def _cparams(sem):
    return pltpu.CompilerParams(dimension_semantics=sem, vmem_limit_bytes=VMEM_LIMIT)


def _rms(x, g):
    return x * lax.rsqrt(jnp.mean(x * x, axis=-1, keepdims=True) + EPS) * g


def _silu(x):
    return x * jax.nn.sigmoid(x)


def _log_sigmoid(x):
    return jnp.minimum(x, 0.0) - jnp.log1p(jnp.exp(-jnp.abs(x)))


def _pieces3(x):
    hi = x.astype(BF16).astype(F32)
    r1 = x - hi
    mid = r1.astype(BF16).astype(F32)
    lo = (r1 - mid).astype(BF16).astype(F32)
    return hi, mid, lo


def _dot(a, b):
    return jnp.dot(a, b, preferred_element_type=F32)


def _dot_nt(a, b):
    return lax.dot_general(a, b, (((1,), (1,)), ((), ())), preferred_element_type=F32)


def _dot_tn(a, b):
    return lax.dot_general(a, b, (((0,), (0,)), ((), ())), preferred_element_type=F32)


def _cat(parts, axis=0):
    return parts[0] if len(parts) == 1 else jnp.concatenate(parts, axis=axis)


def _prefix(g, blk):
    rows, n = g.shape
    row = lax.broadcasted_iota(jnp.int32, (SUBLANES, n), 0)
    outs = []
    off = None
    for r0 in range(0, rows, SUBLANES):
        x = g[r0:r0 + SUBLANES]
        for sh in (1, 2, 4):
            x = x + jnp.where(row >= sh, pltpu.roll(x, sh, 0), 0.0)
        if r0 % blk:
            x = x + off
        off = x[SUBLANES - 1:SUBLANES]
        outs.append(x)
    return _cat(outs)


def _block_last(x, blk):
    rows, n = x.shape
    return _cat([jnp.broadcast_to(x[r0 + blk - 1:r0 + blk], (blk, n)) for r0 in range(0, rows, blk)])


def _block_mask(rows, blk):
    tt = lax.broadcasted_iota(jnp.int32, (rows, rows), 0)
    ss = lax.broadcasted_iota(jnp.int32, (rows, rows), 1)
    mask = ss <= tt
    if blk < rows:
        sh = blk.bit_length() - 1
        mask = jnp.logical_and(mask, lax.shift_right_logical(tt, sh) == lax.shift_right_logical(ss, sh))
    return mask, tt - ss


def _row_bcast(col):
    c = col.shape[0]
    lane = lax.broadcasted_iota(jnp.int32, (c, LANES), 1)
    pieces = _pieces3(jnp.where(lane == 0, jnp.broadcast_to(col, (c, LANES)), 0.0))
    rm = jnp.concatenate(pieces, axis=1).astype(BF16)
    ones = jnp.ones((c, 3 * LANES), BF16)
    return _dot_nt(ones, rm)


def _levels(blk):
    out = []
    m = 1
    while m < blk:
        out.append(m)
        m *= 2
    return tuple(out)


def _pair_masks(rows, blk):
    t = np.arange(rows)[:, None]
    u = np.arange(rows)[None, :]
    masks = [t == u] + [((t % (2 * m)) >= m) & ((u % (2 * m)) < m) & (t // (2 * m) == u // (2 * m))
                        for m in _levels(blk)]
    return jnp.asarray(np.stack(masks).astype(np.float32))


def _level_exponent(b, g, m):
    rows, n = b.shape
    if m >= SUBLANES:
        mid = _cat([jnp.broadcast_to(b[r0 + m - 1:r0 + m], (2 * m, n)) for r0 in range(0, rows, 2 * m)])
        return -jnp.abs(b - mid)
    row = lax.broadcasted_iota(jnp.int32, (SUBLANES, n), 0)
    outs = []
    for r0 in range(0, rows, SUBLANES):
        x = b[r0:r0 + SUBLANES]
        if m == 1:
            outs.append(jnp.where(jnp.bitwise_and(row, 1) == 1, g[r0:r0 + SUBLANES], 0.0))
            continue
        if m == 2:
            mid = jnp.where(row < 4, jnp.broadcast_to(x[1:2], x.shape), jnp.broadcast_to(x[5:6], x.shape))
        else:
            mid = jnp.broadcast_to(x[3:4], x.shape)
        outs.append(-jnp.abs(x - mid))
    return _cat(outs)


def _state_update(kw, v, e_row, s):
    c, kd = kw.shape
    vd = v.shape[1]
    pad = 2 * SUBLANES
    lhs = jnp.concatenate(_pieces3(e_row) + (jnp.zeros((pad - 3, kd), F32), kw), axis=0)
    sel = (lax.broadcasted_iota(jnp.int32, (pad, LANES), 0) < 3).astype(F32)
    rhs = jnp.concatenate(
        [jnp.concatenate([jnp.zeros((pad, vd), F32), sel], axis=1),
         jnp.concatenate([v, jnp.zeros((c, LANES), F32)], axis=1)], axis=0)
    upd = _dot_tn(lhs.astype(BF16), rhs.astype(BF16))
    e_col = upd[:, vd:]
    if vd > LANES:
        e_col = jnp.concatenate([e_col] * (vd // LANES), axis=1)
    return e_col * s + upd[:, :vd]


def _head_out(o, gn, gate_act):
    return o * lax.rsqrt(jnp.mean(o * o, axis=-1, keepdims=True) + EPS) * gn * gate_act


def _ffn_kernel(*refs, nf, post):
    x_ref, g_ref, w1_ref, w3_ref, w2_ref = refs[:5]
    pg_ref = refs[5] if post else None
    outs = refs[6 if post else 5:-1]
    xn_ref = refs[-1]
    acc_ref = outs[0]
    f = pl.program_id(1)

    @pl.when(f == 0)
    def _():
        xn_ref[...] = _rms(x_ref[...], g_ref[...]).astype(BF16)
        acc_ref[...] = jnp.zeros_like(acc_ref)

    xn = xn_ref[...]
    h1 = _dot(xn, w1_ref[...])
    h3 = _dot(xn, w3_ref[...])
    acc_ref[...] += _dot((_silu(h1) * h3).astype(BF16), w2_ref[...])

    @pl.when(f == nf - 1)
    def _():
        y = x_ref[...] + 0.5 * acc_ref[...]
        if post == "final":
            outs[0][...] = _rms(y, pg_ref[...])
        else:
            outs[0][...] = y
            if post == "norm_out":
                outs[1][...] = _rms(y, pg_ref[...]).astype(BF16)


def _ffn(x, gain, w1, w3, w2, l, idx, post=None, post_gain=None):
    m = x.shape[0]
    nf = FF_DIM // TF
    tm = TM if post else TM_WIDE
    in_specs = [
        pl.BlockSpec((tm, D_MODEL), lambda i, f: (i, 0)),
        pl.BlockSpec((1, D_MODEL), lambda i, f: (0, 0)),
        pl.BlockSpec((None, None, D_MODEL, TF), lambda i, f: (l, idx, 0, f)),
        pl.BlockSpec((None, None, D_MODEL, TF), lambda i, f: (l, idx, 0, f)),
        pl.BlockSpec((None, None, TF, D_MODEL), lambda i, f: (l, idx, f, 0)),
    ]
    args = [x, gain.reshape(1, D_MODEL), w1, w3, w2]
    if post:
        in_specs.append(pl.BlockSpec((1, D_MODEL), lambda i, f: (0, 0)))
        args.append(post_gain.reshape(1, D_MODEL))
    row_spec = pl.BlockSpec((tm, D_MODEL), lambda i, f: (i, 0))
    out_specs, out_shape = [row_spec], [jax.ShapeDtypeStruct((m, D_MODEL), F32)]
    if post == "norm_out":
        out_specs.append(row_spec)
        out_shape.append(jax.ShapeDtypeStruct((m, D_MODEL), BF16))
    res = pl.pallas_call(
        functools.partial(_ffn_kernel, nf=nf, post=post),
        grid=(m // tm, nf),
        in_specs=in_specs,
        out_specs=out_specs,
        out_shape=out_shape,
        scratch_shapes=[pltpu.VMEM((tm, D_MODEL), BF16)],
        compiler_params=_cparams(("parallel", "arbitrary")),
        name="ffn_" + post if post else "ffn",
    )(*args)
    return res if post == "norm_out" else res[0]


def _permute_cols_kernel(w_ref, o_ref, *, pieces):
    x = w_ref[...]
    parts = [x[:, a:b] for a, b in pieces]
    width = sum(b - a for a, b in pieces)
    parts.append(jnp.zeros((x.shape[0], o_ref.shape[1] - width), x.dtype))
    o_ref[...] = jnp.concatenate(parts, axis=1).astype(o_ref.dtype)


def _permute_cols(w, j, pieces, n_out, rows_per_step=256):
    k = w.shape[1]
    return pl.pallas_call(
        functools.partial(_permute_cols_kernel, pieces=pieces),
        grid=(k // rows_per_step,),
        in_specs=[pl.BlockSpec((None, rows_per_step, w.shape[2]), lambda i: (j, i, 0))],
        out_specs=pl.BlockSpec((rows_per_step, n_out), lambda i: (i, 0)),
        out_shape=jax.ShapeDtypeStruct((k, n_out), BF16),
        compiler_params=_cparams(("parallel",)),
        name="permute_cols",
    )(w)


def _inproj_kernel(h_ref, w_ref, z_ref):
    z_ref[...] = _dot(h_ref[...], w_ref[...])


def _inproj(h, w):
    m, n = h.shape[0], w.shape[1]
    tn = n // N_SPLIT
    return pl.pallas_call(
        _inproj_kernel,
        grid=(N_SPLIT, m // TM_IN),
        in_specs=[
            pl.BlockSpec((TM_IN, D_MODEL), lambda j, i: (i, 0)),
            pl.BlockSpec((D_MODEL, tn), lambda j, i: (0, j)),
        ],
        out_specs=pl.BlockSpec((TM_IN, tn), lambda j, i: (i, j)),
        out_shape=jax.ShapeDtypeStruct((m, n), F32),
        compiler_params=_cparams(("parallel", "parallel")),
        name="inproj",
    )(h, w)


def _outproj_kernel(x_ref, y_ref, w_ref, o_ref):
    o_ref[...] = x_ref[...] + _dot(y_ref[...].astype(BF16), w_ref[...])


def _outproj(x, y, w):
    m, kdim = y.shape
    return pl.pallas_call(
        _outproj_kernel,
        grid=(m // TM_OUT,),
        in_specs=[
            pl.BlockSpec((TM_OUT, D_MODEL), lambda i: (i, 0)),
            pl.BlockSpec((TM_OUT, kdim), lambda i: (i, 0)),
            pl.BlockSpec((kdim, D_MODEL), lambda i: (0, 0)),
        ],
        out_specs=pl.BlockSpec((TM_OUT, D_MODEL), lambda i: (i, 0)),
        out_shape=jax.ShapeDtypeStruct((m, D_MODEL), F32),
        compiler_params=_cparams(("parallel",)),
        name="outproj",
    )(x, y, w)


def _states(per_seq, nblk, ins, outs):
    if per_seq:
        def get(i):
            return tuple(r[i, 0] for r in ins)

        def put(i, vals):
            for o, val in zip(outs, vals):
                o[i, 0] = val
        return get, put

    @pl.when(pl.program_id(2) == 0)
    def _():
        for o, r in zip(outs, ins):
            o[...] = r[...]

    carried = {"v": tuple(o[0, 0] for o in outs)}

    def get(i):
        return carried["v"]

    def put(i, vals):
        carried["v"] = tuple(vals)
        if i == nblk - 1:
            for o, val in zip(outs, vals):
                o[0, 0] = val
    return get, put


def _ret_kernel(q_ref, k_ref, v_ref, gt_ref, cos_ref, sin_ref, gn_ref, s_in, *rest, blk, nblk, per_seq):
    y_ref, s_out = rest[-2:]
    rows = blk * nblk
    hf = jnp.full((1, 1), pl.program_id(1), jnp.int32).astype(F32)
    lg = jnp.log1p(-jnp.exp2(-5.0 - hf))
    cos = _cat([cos_ref[...]] * (rows // cos_ref.shape[0]))
    sin = _cat([sin_ref[...]] * (rows // sin_ref.shape[0]))
    q = q_ref[...]
    k = k_ref[...]
    q = q * cos + pltpu.roll(q, DK_A // 2, 1) * sin
    k = (k * cos + pltpu.roll(k, DK_A // 2, 1) * sin) * (DK_A ** -0.5)
    vb = v_ref[...].astype(BF16)
    mask, dist = _block_mask(rows, blk)
    dec = jnp.where(mask, jnp.exp(jnp.maximum(dist, 0).astype(F32) * lg), 0.0)
    a = _dot_nt(q.astype(BF16), k.astype(BF16)) * dec
    o = _dot(a.astype(BF16), vb)
    tloc = jnp.bitwise_and(lax.broadcasted_iota(jnp.int32, (rows, 1), 0), blk - 1).astype(F32)
    qt = (q * jnp.exp((tloc + 1.0) * lg)).astype(BF16)
    kw = (k * jnp.exp((blk - 1.0 - tloc) * lg)).astype(BF16)
    e_blk = jnp.exp(blk * lg)
    get, put = _states(per_seq, nblk, (s_in,), (s_out,))
    parts = []
    for i in range(nblk):
        r = slice(i * blk, (i + 1) * blk)
        s, = get(i)
        parts.append(o[r] + _dot(qt[r], s.astype(BF16)))
        put(i, (e_blk * s + _dot_tn(kw[r], vb[r]),))
    y_ref[...] = _head_out(_cat(parts), gn_ref[...], _silu(gt_ref[...]))


def _key_decay_slab(q, k, v, g, mk_ref, blk, nblk, get, put):
    rows = q.shape[0]
    tile = min(rows, PAIR_TILE)
    tiles = [slice(i * tile, (i + 1) * tile) for i in range(rows // tile)]
    g2 = g * LOG2E
    b = _prefix(g2, blk)
    b_last = _block_last(b, blk)
    qt = (q * jnp.exp2(b)).astype(BF16)
    kw = k * jnp.exp2(b_last - b)
    e_last = jnp.exp2(b_last)
    vb = v.astype(BF16)
    qb, kb = q.astype(BF16), k.astype(BF16)
    a = {(i, i): mk_ref[0] * _dot_nt(qb[t], kb[t]) for i, t in enumerate(tiles)}
    for li, m in enumerate(_levels(blk)):
        w = jnp.exp2(_level_exponent(b, g2, m))
        ql, kl = (q * w).astype(BF16), (k * w).astype(BF16)
        if m < tile:
            for i, t in enumerate(tiles):
                a[i, i] = a[i, i] + mk_ref[li + 1] * _dot_nt(ql[t], kl[t])
        else:
            for r0 in range(0, rows, 2 * m):
                for i in range((r0 + m) // tile, (r0 + 2 * m) // tile):
                    for j in range(r0 // tile, (r0 + m) // tile):
                        a[i, j] = _dot_nt(ql[tiles[i]], kl[tiles[j]])
    o_tiles = []
    for i in range(len(tiles)):
        terms = [_dot(a[i, j].astype(BF16), vb[tiles[j]]) for j in range(len(tiles)) if (i, j) in a]
        o_tiles.append(functools.reduce(lambda x, y: x + y, terms))
    o = _cat(o_tiles)
    parts = []
    for i in range(nblk):
        r = slice(i * blk, (i + 1) * blk)
        s, = get(i)
        parts.append(o[r] + _dot(qt[r], s.astype(BF16)))
        put(i, (_state_update(kw[r], v[r], e_last[i * blk:i * blk + 1], s),))
    return _cat(parts)


def _hgrn_kernel(q_ref, f_ref, i_ref, gt_ref, lb_ref, gn_ref, *rest, blk, nblk, per_seq, layer):
    mk_ref, s_in = rest[:2]
    y_ref, s_out = rest[-2:]
    lbl = lb_ref[...]
    e = jnp.exp(lbl - jnp.max(lbl, axis=0, keepdims=True))
    lb = jnp.sum(e[0:layer + 1], axis=0, keepdims=True) / jnp.sum(e, axis=0, keepdims=True)
    get, put = _states(per_seq, nblk, (s_in,), (s_out,))
    k = (1.0 - lb) * (1.0 - jax.nn.sigmoid(f_ref[...]))
    o = _key_decay_slab(_silu(q_ref[...]), k, i_ref[...], jnp.log(1.0 - k), mk_ref, blk, nblk, get, put)
    y_ref[...] = _head_out(o, gn_ref[...], _silu(gt_ref[...]))


def _gla_kernel(q_ref, k_ref, v_ref, gt_ref, tail_ref, w2_ref, ab_ref, gn_ref, *rest, blk, nblk, per_seq):
    mk_ref, s_in = rest[:2]
    y_ref, s_out = rest[-2:]
    wh, wm, wl = (p.astype(BF16) for p in _pieces3(w2_ref[...]))
    get, put = _states(per_seq, nblk, (s_in,), (s_out,))
    th, tm_, tl = (p.astype(BF16) for p in _pieces3(tail_ref[...]))
    x = (_dot(th, wh) + (_dot(th, wm) + _dot(tm_, wh))
         + (_dot(th, wl) + _dot(tm_, wm) + _dot(tl, wh)))
    g = _log_sigmoid(x + ab_ref[...]) / GLA_TAU
    o = _key_decay_slab(q_ref[...] * (DK_D ** -0.5), k_ref[...], v_ref[...], g, mk_ref, blk, nblk, get, put)
    y_ref[...] = _head_out(o, gn_ref[...], _silu(gt_ref[...]))


def _mlstm_kernel(q_ref, k_ref, v_ref, og_ref, tail_ref, wq_ref, wk_ref, bias_ref, gn_ref,
                  cq_in, ck_in, c_in, n_in, m_in, *rest, blk, nblk, per_seq):
    y_ref, c_out, n_out, m_out, cq_out, ck_out, uq_ref, uk_ref, hq_ref, hk_ref = rest[-10:]
    assert per_seq or nblk == 1
    rows = blk * nblk
    h = pl.program_id(1)
    hist = CONV_W - 1
    lo = SUBLANES - hist

    if not per_seq:
        @pl.when(pl.program_id(2) == 0)
        def _():
            for h_ref, cin in ((hq_ref, cq_in), (hk_ref, ck_in)):
                h_ref[...] = jnp.zeros_like(h_ref)
                h_ref[lo:SUBLANES, :] = cin[0]

    def conv(u_ref, h_ref, cin, cout, raw_ref, w_ref):
        parts = []
        for i in range(nblk):
            raw = raw_ref[i * blk:(i + 1) * blk, :]
            if per_seq:
                u_ref[i, lo:SUBLANES, :] = cin[i]
            else:
                u_ref[i, 0:SUBLANES, :] = h_ref[...]
                h_ref[...] = raw[blk - SUBLANES:blk]
            u_ref[i, SUBLANES:SUBLANES + blk, :] = raw
            cout[i] = raw[blk - hist:blk]
            acc = u_ref[i, lo:lo + blk, :] * w_ref[0:1, :]
            for j in range(1, CONV_W):
                acc = acc + u_ref[i, lo + j:lo + j + blk, :] * w_ref[j:j + 1, :]
            parts.append(acc)
        return _silu(_cat(parts))

    q = conv(uq_ref, hq_ref, cq_in, cq_out, q_ref, wq_ref)
    k = conv(uk_ref, hk_ref, ck_in, ck_out, k_ref, wk_ref) * (DK_C ** -0.5)
    qb = q.astype(BF16)
    vb = v_ref[...].astype(BF16)

    lane = lax.broadcasted_iota(jnp.int32, (rows, LANES), 1)
    tl = tail_ref[...] + bias_ref[...]
    li = jnp.sum(jnp.where(lane == h, tl, 0.0), axis=-1, keepdims=True)
    lf = jnp.sum(jnp.where(lane == H_C + h, _log_sigmoid(tl), 0.0), axis=-1, keepdims=True)
    b = _prefix(jnp.broadcast_to(lf, (rows, LANES)), blk)[:, :1]

    get, put = _states(per_seq, nblk, (c_in, n_in, m_in), (c_out, n_out, m_out))
    states = [get(i) for i in range(nblk)] if per_seq else [get(0)]
    m_prev = _cat([jnp.broadcast_to(st[2][:, :1], (blk, 1)) for st in states])

    mask, _ = _block_mask(rows, blk)
    d = jnp.where(mask, b + _row_bcast(li - b), -jnp.inf)
    m_inter = b + m_prev
    m_t = jnp.maximum(m_inter, jnp.max(d, axis=-1, keepdims=True))
    w_inter = jnp.exp(m_inter - m_t)
    sc = _dot_nt(qb, k.astype(BF16)) * jnp.exp(d - m_t)
    num = _dot(sc.astype(BF16), vb)
    den = jnp.sum(sc, axis=-1, keepdims=True)
    m_new = _block_last(m_t, blk)
    b_last = _block_last(b, blk)
    kw = k * jnp.exp(b_last - b + li - m_new)
    w_c = jnp.exp(b_last + m_prev - m_new)
    nums, dens = [], []
    for i in range(nblk):
        r = slice(i * blk, (i + 1) * blk)
        cst, nrow, _ = states[i]
        nums.append(num[r] + w_inter[r] * _dot(qb[r], cst.astype(BF16)))
        dens.append(den[r] + w_inter[r] * jnp.sum(q[r] * nrow, axis=-1, keepdims=True))
        wc = w_c[i * blk:i * blk + 1]
        put(i, (wc * cst + _dot_tn(kw[r].astype(BF16), vb[r]),
                wc * nrow + jnp.sum(kw[r], axis=0, keepdims=True),
                jnp.broadcast_to(m_new[i * blk:i * blk + 1], (1, LANES))))
    hh = _cat(nums) / jnp.maximum(jnp.abs(_cat(dens)), jnp.exp(-m_t))
    y_ref[...] = _head_out(hh, gn_ref[...], jax.nn.sigmoid(og_ref[...]))


def _geometry(n_seq, seq_len, row0, kind):
    slab, chunk = PROMPT_SLAB[kind]
    if seq_len >= slab:
        blk, rows, per_seq, nseq_blk = chunk, slab, False, 1
        grid_seq, grid_slab = n_seq, seq_len // slab
    else:
        blk, nseq_blk, per_seq = seq_len, SEQ_PER_STEP, True
        rows = nseq_blk * blk
        grid_seq, grid_slab = n_seq // nseq_blk, 1
    base = row0 // rows
    return dict(blk=blk, rows=rows, nblk=rows // blk, per_seq=per_seq, nseq_blk=nseq_blk,
                grid_seq=grid_seq, grid_slab=grid_slab, rowblk=lambda b, s: base + b * grid_slab + s)


def _zspec(geo, width, col):
    rb = geo["rowblk"]
    return pl.BlockSpec((geo["rows"], width), lambda b, h, s: (rb(b, s), col + h))


def _const_spec(shape):
    return pl.BlockSpec(shape, lambda b, h, s: (0,) * len(shape))


def _head_spec(width, col=0):
    return pl.BlockSpec((1, width), lambda b, h, s: (0, col + h))


def _state_spec(geo, shape):
    return pl.BlockSpec((geo["nseq_blk"], 1) + shape, lambda b, h, s: (b, h, 0, 0))


def _mixer_call(kern, geo, heads, vd, y_col, y_prev, n_tok, in_specs, args, extra_specs, extra_shapes,
                scratch, name):
    if y_prev is not None:
        in_specs = in_specs + [pl.BlockSpec(memory_space=pl.ANY)]
        args = args + [y_prev]
    rb = geo["rowblk"]
    return pl.pallas_call(
        kern,
        grid=(geo["grid_seq"], heads, geo["grid_slab"]),
        in_specs=in_specs,
        out_specs=[pl.BlockSpec((geo["rows"], vd), lambda b, h, s: (rb(b, s), y_col // vd + h))] + extra_specs,
        out_shape=[jax.ShapeDtypeStruct((n_tok, MIX), F32)] + extra_shapes,
        input_output_aliases={} if y_prev is None else {len(args) - 1: 0},
        scratch_shapes=scratch,
        compiler_params=_cparams(("parallel", "parallel", "arbitrary")),
        name=name + ("_sample" if geo["per_seq"] else "_prompt"),
    )(*args)


def _decay_mixer(kind, z, geo, state, extra, y_prev, y_col):
    heads, kd, vd = {"ret": (H_A, DK_A, DV_A), "hgrn": (H_B, DK_B, DV_B), "gla": (H_D, DK_D, DV_D)}[kind]
    common = dict(blk=geo["blk"], nblk=geo["nblk"], per_seq=geo["per_seq"])
    kb, vb = LANES, vd
    if kind == "ret":
        cos, sin = extra["rope"]
        tab = pl.BlockSpec((geo["blk"] if geo["per_seq"] else geo["rows"], DK_A),
                           lambda b, h, s: (0 if geo["per_seq"] else s, 0))
        in_specs = [_zspec(geo, kb, 0), _zspec(geo, kb, H_A), _zspec(geo, vb, (2 * H_A * DK_A) // vb),
                    _zspec(geo, vb, (2 * H_A * DK_A) // vb + H_A), tab, tab, _head_spec(vd)]
        args = [z, z, z, z, cos, sin, extra["gn"].reshape(1, -1)]
        kern = functools.partial(_ret_kernel, **common)
    elif kind == "hgrn":
        off = (2 * H_A * DK_A + 2 * H_A * DV_A) // LANES
        in_specs = [_zspec(geo, kb, off), _zspec(geo, kb, off + H_B), _zspec(geo, kb, off + 2 * H_B),
                    _zspec(geo, kb, off + 3 * H_B),
                    pl.BlockSpec((DEPTH + 1, DK_B), lambda b, h, s: (0, h)), _head_spec(vd)]
        args = [z, z, z, z, extra["lb_logits"], extra["gn"].reshape(1, -1)]
        kern = functools.partial(_hgrn_kernel, layer=extra["layer"], **common)
    else:
        off = (2 * H_C * DK_C + 2 * H_C * DV_C) // LANES
        voff = (2 * H_C * DK_C + 2 * H_C * DV_C + 2 * H_D * DK_D) // vb
        in_specs = [_zspec(geo, kb, off), _zspec(geo, kb, off + H_D), _zspec(geo, vb, voff),
                    _zspec(geo, vb, voff + H_D),
                    pl.BlockSpec((geo["rows"], CD_TAIL), lambda b, h, s: (geo["rowblk"](b, s), CD_MAIN // CD_TAIL)),
                    pl.BlockSpec((CD_TAIL, DK_D), lambda b, h, s: (0, h)),
                    _head_spec(DK_D), _head_spec(vd)]
        args = [z, z, z, z, z, extra["w2"], extra["ab"].reshape(1, -1), extra["gn"].reshape(1, -1)]
        kern = functools.partial(_gla_kernel, **common)
    if kind != "ret":
        tile = min(geo["rows"], PAIR_TILE)
        mk = _pair_masks(tile, min(geo["blk"], tile))
        in_specs.append(_const_spec(mk.shape))
        args.append(mk)
    in_specs.append(_state_spec(geo, (kd, vd)))
    args.append(state)
    return _mixer_call(kern, geo, heads, vd, y_col, y_prev, z.shape[0], in_specs, args,
                       [_state_spec(geo, (kd, vd))], [jax.ShapeDtypeStruct(state.shape, F32)], [], kind)


def _mlstm_mixer(z, geo, c0, n0, m0, conv0, w, y_prev, y_col):
    n_seq = c0.shape[0]
    blk, nb = geo["blk"], geo["nseq_blk"]
    bias = jnp.zeros((1, LANES), F32).at[0, :H_C].set(w["i_bias"]).at[0, H_C:2 * H_C].set(w["f_bias"])
    kk = (H_C * DK_C) // DK_C
    kv = (2 * H_C * DK_C) // DV_C
    conv_spec = lambda col: pl.BlockSpec((nb, CONV_W - 1, DK_C), lambda b, h, s: (b, 0, col + h))
    in_specs = [
        _zspec(geo, DK_C, 0), _zspec(geo, DK_C, kk), _zspec(geo, DV_C, kv), _zspec(geo, DV_C, kv + H_C),
        pl.BlockSpec((geo["rows"], CD_TAIL), lambda b, h, s: (geo["rowblk"](b, s), CD_MAIN // CD_TAIL)),
        pl.BlockSpec((CONV_W, DK_C), lambda b, h, s: (0, h)),
        pl.BlockSpec((CONV_W, DK_C), lambda b, h, s: (0, H_C + h)),
        _const_spec((1, LANES)), _head_spec(DV_C),
        conv_spec(0), conv_spec(H_C),
        _state_spec(geo, (DK_C, DV_C)), _state_spec(geo, (1, DK_C)), _state_spec(geo, (1, LANES)),
    ]
    n4 = n0.reshape(n_seq, H_C, 1, DK_C)
    m4 = jnp.broadcast_to(m0[:, :, None, None], (n_seq, H_C, 1, LANES))
    args = [z, z, z, z, z, w["conv_w"], w["conv_w"], bias, w["gn_c"].reshape(1, -1),
            conv0, conv0, c0, n4, m4]
    conv_shape = jax.ShapeDtypeStruct((n_seq, CONV_W - 1, H_C * DK_C), F32)
    y, c_new, n_new, m_new, cq, ck = _mixer_call(
        functools.partial(_mlstm_kernel, blk=blk, nblk=geo["nblk"], per_seq=geo["per_seq"]),
        geo, H_C, DV_C, y_col, y_prev, z.shape[0], in_specs, args,
        [_state_spec(geo, (DK_C, DV_C)), _state_spec(geo, (1, DK_C)), _state_spec(geo, (1, LANES)),
         conv_spec(0), conv_spec(0)],
        [jax.ShapeDtypeStruct(c0.shape, F32), jax.ShapeDtypeStruct(n4.shape, F32),
         jax.ShapeDtypeStruct(m4.shape, F32), conv_shape, conv_shape],
        [pltpu.VMEM((geo["nblk"], SUBLANES + blk, DK_C), F32), pltpu.VMEM((geo["nblk"], SUBLANES + blk, DK_C), F32),
         pltpu.VMEM((SUBLANES, DK_C), F32), pltpu.VMEM((SUBLANES, DK_C), F32)],
        "mlstm")
    return (y, c_new, n_new.reshape(n_seq, H_C, DK_C), m_new[:, :, 0, 0],
            jnp.concatenate([cq, ck], axis=-1))


def _rope_tables(pos0, n):
    half = DK_A // 2
    inv = ROPE_BASE ** (-jnp.arange(half, dtype=F32) / half)
    ang = (jnp.arange(n, dtype=F32) + float(pos0))[:, None] * inv[None, :]
    cos, sin = jnp.cos(ang), jnp.sin(ang)
    return jnp.concatenate([cos, cos], axis=1), jnp.concatenate([-sin, sin], axis=1)


def kernel(x_prompt, x_sample, state_ret, state_hgrn, state_mlstm_c, state_mlstm_n, state_mlstm_m,
           state_mlstm_conv, state_gla, norm_gain, ffn_w1, ffn_w3, ffn_w2, ab_w_in, ab_w_out, ab_norm_a,
           ab_norm_b, lb_logits, cd_w_in, cd_w_out, cd_conv_w, cd_i_bias, cd_f_bias, cd_norm_c,
           cd_alpha_w2, cd_alpha_b, cd_norm_d, final_norm):
    bp, tp, _ = x_prompt.shape
    bs, ts, _ = x_sample.shape
    n_p, n_s = bp * tp, bs * ts
    x = jnp.concatenate([x_prompt.reshape(n_p, D_MODEL), x_sample.reshape(n_s, D_MODEL)], axis=0)
    w1, w3, w2 = ffn_w1.astype(BF16), ffn_w3.astype(BF16), ffn_w2.astype(BF16)
    groups = (("p", bp, {k: _geometry(bp, tp, 0, k) for k in PROMPT_SLAB}),
              ("s", bs, {k: _geometry(bs, ts, n_p, k) for k in PROMPT_SLAB}))
    rope = {"p": _rope_tables(0, tp), "s": _rope_tables(PAST_LEN, ts)}
    zeros = lambda *shape: jnp.zeros(shape, F32)
    outs = {}

    def keep(name, tag, val):
        outs.setdefault(name + "_" + tag, []).append(val)

    for l in range(DEPTH):
        j = l // 2
        x, hn = _ffn(x, norm_gain[l, 0], w1, w3, w2, l, 0, post="norm_out", post_gain=norm_gain[l, 1])
        y = None
        if l % 2 == 0:
            z = _inproj(hn, ab_w_in[j].astype(BF16))
            for tag, nseq, geo in groups:
                s_ret = zeros(nseq, H_A, DK_A, DV_A) if tag == "p" else state_ret[j]
                s_hg = zeros(nseq, H_B, DK_B, DV_B) if tag == "p" else state_hgrn[j]
                y, r_new = _decay_mixer("ret", z, geo["ret"], s_ret, dict(rope=rope[tag], gn=ab_norm_a[j]), y, 0)
                y, g_new = _decay_mixer("hgrn", z, geo["hgrn"], s_hg,
                                        dict(lb_logits=lb_logits, gn=ab_norm_b[j], layer=l), y, H_A * DV_A)
                keep("ret", tag, r_new)
                keep("hgrn", tag, g_new)
            x = _outproj(x, y, ab_w_out[j].astype(BF16))
        else:
            gates = 2 * H_C
            main = 2 * H_C * DK_C + 2 * H_C * DV_C
            rest = CD_MAIN - main
            n_pad = -(-(CD_MAIN + CD_TAIL) // (N_SPLIT * LANES)) * (N_SPLIT * LANES)
            w_perm = _permute_cols(cd_w_in.astype(BF16), j, ((0, main), (main + gates, main + gates + rest),
                                                (main, main + gates), (main + gates + rest, cd_w_in.shape[2])), n_pad)
            z = _inproj(hn, w_perm)
            w2p = jnp.zeros((CD_TAIL, H_D * DK_D), F32).at[gates:gates + GLA_RANK].set(cd_alpha_w2[j])
            wts = dict(conv_w=cd_conv_w[j], i_bias=cd_i_bias[j], f_bias=cd_f_bias[j], gn_c=cd_norm_c[j])
            for tag, nseq, geo in groups:
                if tag == "p":
                    c0, n0, m0 = zeros(nseq, H_C, DK_C, DV_C), zeros(nseq, H_C, DK_C), zeros(nseq, H_C)
                    cv0, s_gl = zeros(nseq, CONV_W - 1, 2 * H_C * DK_C), zeros(nseq, H_D, DK_D, DV_D)
                else:
                    c0, n0, m0 = state_mlstm_c[j], state_mlstm_n[j], state_mlstm_m[j]
                    cv0, s_gl = state_mlstm_conv[j], state_gla[j]
                y, c_new, n_new, m_new, cv_new = _mlstm_mixer(z, geo["mlstm"], c0, n0, m0, cv0, wts, y, 0)
                y, gl_new = _decay_mixer("gla", z, geo["gla"], s_gl,
                                         dict(w2=w2p, ab=cd_alpha_b[j], gn=cd_norm_d[j]), y, H_C * DV_C)
                for name, val in (("c", c_new), ("n", n_new), ("m", m_new), ("conv", cv_new), ("gla", gl_new)):
                    keep(name, tag, val)
            x = _outproj(x, y, cd_w_out[j].astype(BF16))
        last = l == DEPTH - 1
        x = _ffn(x, norm_gain[l, 2], w1, w3, w2, l, 1, post="final" if last else None,
                 post_gain=final_norm if last else None)

    y_p = x[:n_p].reshape(bp, tp, D_MODEL)
    y_s = x[n_p:].reshape(bs, ts, D_MODEL)
    st = lambda name: jnp.stack(outs[name])
    return (y_p, y_s, st("ret_p"), st("ret_s"), st("hgrn_p"), st("hgrn_s"), st("c_p"), st("c_s"),
            st("n_p"), st("n_s"), st("m_p"), st("m_s"), st("conv_p"), st("conv_s"), st("gla_p"), st("gla_s"))
```

```python
import functools

import numpy as np
import jax
import jax.numpy as jnp
from jax import lax
from jax.experimental import pallas as pl
from jax.experimental.pallas import tpu as pltpu

F32 = jnp.float32
BF16 = jnp.bfloat16

D_MODEL = 2048
DEPTH = 2
PAST_LEN = 16384
H_A, DK_A, DV_A = 4, 128, 256
ROPE_BASE = 10000.0
H_B, DK_B, DV_B = 8, 128, 128
H_C, DK_C, DV_C = 4, 256, 256
CONV_W = 4
H_D, DK_D, DV_D = 4, 128, 256
GLA_RANK = 16
GLA_TAU = 16.0
FF_DIM = 5632
EPS = 1e-6

CD_MAIN = 2 * H_C * DK_C + 2 * H_C * DV_C + 2 * H_D * DK_D + 2 * H_D * DV_D
CD_TAIL = 128
MIX = H_A * DV_A + H_B * DV_B

LANES = 128
SUBLANES = 8
VMEM_LIMIT = 56 * 1024 * 1024

TM = 768
TM_WIDE = 1024
TM_IN = 1152
TM_OUT = 512
TF = 512
N_SPLIT = 4
PROMPT_SLAB = {"ret": (512, 512), "hgrn": (1024, 256), "mlstm": (256, 256), "gla": (1024, 256)}
SEQ_PER_STEP = 32
PAIR_TILE = 128
LOG2E = 1.4426950408889634


def _cparams(sem):
    return pltpu.CompilerParams(dimension_semantics=sem, vmem_limit_bytes=VMEM_LIMIT)


def _rms(x, g):
    return x * lax.rsqrt(jnp.mean(x * x, axis=-1, keepdims=True) + EPS) * g


def _silu(x):
    return x * jax.nn.sigmoid(x)


def _log_sigmoid(x):
    return jnp.minimum(x, 0.0) - jnp.log1p(jnp.exp(-jnp.abs(x)))


def _pieces3(x):
    hi = x.astype(BF16).astype(F32)
    r1 = x - hi
    mid = r1.astype(BF16).astype(F32)
    lo = (r1 - mid).astype(BF16).astype(F32)
    return hi, mid, lo


def _dot(a, b):
    return jnp.dot(a, b, preferred_element_type=F32)


def _dot_nt(a, b):
    return lax.dot_general(a, b, (((1,), (1,)), ((), ())), preferred_element_type=F32)


def _dot_tn(a, b):
    return lax.dot_general(a, b, (((0,), (0,)), ((), ())), preferred_element_type=F32)


def _cat(parts, axis=0):
    return parts[0] if len(parts) == 1 else jnp.concatenate(parts, axis=axis)


def _prefix(g, blk):
    rows, n = g.shape
    row = lax.broadcasted_iota(jnp.int32, (SUBLANES, n), 0)
    outs = []
    off = None
    for r0 in range(0, rows, SUBLANES):
        x = g[r0:r0 + SUBLANES]
        for sh in (1, 2, 4):
            x = x + jnp.where(row >= sh, pltpu.roll(x, sh, 0), 0.0)
        if r0 % blk:
            x = x + off
        off = x[SUBLANES - 1:SUBLANES]
        outs.append(x)
    return _cat(outs)


def _block_last(x, blk):
    rows, n = x.shape
    return _cat([jnp.broadcast_to(x[r0 + blk - 1:r0 + blk], (blk, n)) for r0 in range(0, rows, blk)])


def _block_mask(rows, blk):
    tt = lax.broadcasted_iota(jnp.int32, (rows, rows), 0)
    ss = lax.broadcasted_iota(jnp.int32, (rows, rows), 1)
    mask = ss <= tt
    if blk < rows:
        sh = blk.bit_length() - 1
        mask = jnp.logical_and(mask, lax.shift_right_logical(tt, sh) == lax.shift_right_logical(ss, sh))
    return mask, tt - ss


def _row_bcast(col):
    c = col.shape[0]
    lane = lax.broadcasted_iota(jnp.int32, (c, LANES), 1)
    pieces = _pieces3(jnp.where(lane == 0, jnp.broadcast_to(col, (c, LANES)), 0.0))
    rm = jnp.concatenate(pieces, axis=1).astype(BF16)
    ones = jnp.ones((c, 3 * LANES), BF16)
    return _dot_nt(ones, rm)


def _levels(blk):
    out = []
    m = 1
    while m < blk:
        out.append(m)
        m *= 2
    return tuple(out)


def _pair_masks(rows, blk):
    t = np.arange(rows)[:, None]
    u = np.arange(rows)[None, :]
    masks = [t == u] + [((t % (2 * m)) >= m) & ((u % (2 * m)) < m) & (t // (2 * m) == u // (2 * m))
                        for m in _levels(blk)]
    return jnp.asarray(np.stack(masks).astype(np.float32))


def _level_exponent(b, g, m):
    rows, n = b.shape
    if m >= SUBLANES:
        mid = _cat([jnp.broadcast_to(b[r0 + m - 1:r0 + m], (2 * m, n)) for r0 in range(0, rows, 2 * m)])
        return -jnp.abs(b - mid)
    row = lax.broadcasted_iota(jnp.int32, (SUBLANES, n), 0)
    outs = []
    for r0 in range(0, rows, SUBLANES):
        x = b[r0:r0 + SUBLANES]
        if m == 1:
            outs.append(jnp.where(jnp.bitwise_and(row, 1) == 1, g[r0:r0 + SUBLANES], 0.0))
            continue
        if m == 2:
            mid = jnp.where(row < 4, jnp.broadcast_to(x[1:2], x.shape), jnp.broadcast_to(x[5:6], x.shape))
        else:
            mid = jnp.broadcast_to(x[3:4], x.shape)
        outs.append(-jnp.abs(x - mid))
    return _cat(outs)


def _state_update(kw, v, e_row, s):
    c, kd = kw.shape
    vd = v.shape[1]
    pad = 2 * SUBLANES
    lhs = jnp.concatenate(_pieces3(e_row) + (jnp.zeros((pad - 3, kd), F32), kw), axis=0)
    sel = (lax.broadcasted_iota(jnp.int32, (pad, LANES), 0) < 3).astype(F32)
    rhs = jnp.concatenate(
        [jnp.concatenate([jnp.zeros((pad, vd), F32), sel], axis=1),
         jnp.concatenate([v, jnp.zeros((c, LANES), F32)], axis=1)], axis=0)
    upd = _dot_tn(lhs.astype(BF16), rhs.astype(BF16))
    e_col = upd[:, vd:]
    if vd > LANES:
        e_col = jnp.concatenate([e_col] * (vd // LANES), axis=1)
    return e_col * s + upd[:, :vd]


def _head_out(o, gn, gate_act):
    return (o * lax.rsqrt(jnp.mean(o * o, axis=-1, keepdims=True) + EPS) * gn * gate_act).astype(BF16)


def _ffn_kernel(*refs, nf, post):
    x_ref, g_ref, w1_ref, w3_ref, w2_ref = refs[:5]
    pg_ref = refs[5] if post else None
    outs = refs[6 if post else 5:-1]
    xn_ref = refs[-1]
    acc_ref = outs[0]
    f = pl.program_id(1)

    @pl.when(f == 0)
    def _():
        xn_ref[...] = _rms(x_ref[...], g_ref[...]).astype(BF16)
        acc_ref[...] = jnp.zeros_like(acc_ref)

    xn = xn_ref[...]
    h1 = _dot(xn, w1_ref[...])
    h3 = _dot(xn, w3_ref[...])
    acc_ref[...] += _dot((_silu(h1) * h3).astype(BF16), w2_ref[...])

    @pl.when(f == nf - 1)
    def _():
        y = x_ref[...] + 0.5 * acc_ref[...]
        if post == "final":
            outs[0][...] = _rms(y, pg_ref[...])
        else:
            outs[0][...] = y
            if post == "norm_out":
                outs[1][...] = _rms(y, pg_ref[...]).astype(BF16)


def _ffn(x, gain, w1, w3, w2, l, idx, post=None, post_gain=None):
    m = x.shape[0]
    nf = FF_DIM // TF
    tm = TM if post else TM_WIDE
    in_specs = [
        pl.BlockSpec((tm, D_MODEL), lambda i, f: (i, 0)),
        pl.BlockSpec((1, D_MODEL), lambda i, f: (0, 0)),
        pl.BlockSpec((None, None, D_MODEL, TF), lambda i, f: (l, idx, 0, f)),
        pl.BlockSpec((None, None, D_MODEL, TF), lambda i, f: (l, idx, 0, f)),
        pl.BlockSpec((None, None, TF, D_MODEL), lambda i, f: (l, idx, f, 0)),
    ]
    args = [x, gain.reshape(1, D_MODEL), w1, w3, w2]
    if post:
        in_specs.append(pl.BlockSpec((1, D_MODEL), lambda i, f: (0, 0)))
        args.append(post_gain.reshape(1, D_MODEL))
    row_spec = pl.BlockSpec((tm, D_MODEL), lambda i, f: (i, 0))
    out_specs, out_shape = [row_spec], [jax.ShapeDtypeStruct((m, D_MODEL), F32)]
    if post == "norm_out":
        out_specs.append(row_spec)
        out_shape.append(jax.ShapeDtypeStruct((m, D_MODEL), BF16))
    res = pl.pallas_call(
        functools.partial(_ffn_kernel, nf=nf, post=post),
        grid=(m // tm, nf),
        in_specs=in_specs,
        out_specs=out_specs,
        out_shape=out_shape,
        scratch_shapes=[pltpu.VMEM((tm, D_MODEL), BF16)],
        compiler_params=_cparams(("parallel", "arbitrary")),
        name="ffn_" + post if post else "ffn",
    )(*args)
    return res if post == "norm_out" else res[0]


def _permute_cols_kernel(w_ref, o_ref, *, pieces):
    x = w_ref[...]
    parts = [x[:, a:b] for a, b in pieces]
    width = sum(b - a for a, b in pieces)
    parts.append(jnp.zeros((x.shape[0], o_ref.shape[1] - width), x.dtype))
    o_ref[...] = jnp.concatenate(parts, axis=1).astype(o_ref.dtype)


def _permute_cols(w, j, pieces, n_out, rows_per_step=256):
    k = w.shape[1]
    return pl.pallas_call(
        functools.partial(_permute_cols_kernel, pieces=pieces),
        grid=(k // rows_per_step,),
        in_specs=[pl.BlockSpec((None, rows_per_step, w.shape[2]), lambda i: (j, i, 0))],
        out_specs=pl.BlockSpec((rows_per_step, n_out), lambda i: (i, 0)),
        out_shape=jax.ShapeDtypeStruct((k, n_out), BF16),
        compiler_params=_cparams(("parallel",)),
        name="permute_cols",
    )(w)


def _inproj_kernel(h_ref, w_ref, z_ref):
    z_ref[...] = _dot(h_ref[...], w_ref[...])


def _inproj(h, w):
    m, n = h.shape[0], w.shape[1]
    tn = n // N_SPLIT
    return pl.pallas_call(
        _inproj_kernel,
        grid=(N_SPLIT, m // TM_IN),
        in_specs=[
            pl.BlockSpec((TM_IN, D_MODEL), lambda j, i: (i, 0)),
            pl.BlockSpec((D_MODEL, tn), lambda j, i: (0, j)),
        ],
        out_specs=pl.BlockSpec((TM_IN, tn), lambda j, i: (i, j)),
        out_shape=jax.ShapeDtypeStruct((m, n), F32),
        compiler_params=_cparams(("parallel", "parallel")),
        name="inproj",
    )(h, w)


def _outproj_kernel(x_ref, y_ref, w_ref, o_ref):
    o_ref[...] = x_ref[...] + _dot(y_ref[...], w_ref[...])


def _outproj(x, y, w):
    m, kdim = y.shape
    return pl.pallas_call(
        _outproj_kernel,
        grid=(m // TM_OUT,),
        in_specs=[
            pl.BlockSpec((TM_OUT, D_MODEL), lambda i: (i, 0)),
            pl.BlockSpec((TM_OUT, kdim), lambda i: (i, 0)),
            pl.BlockSpec((kdim, D_MODEL), lambda i: (0, 0)),
        ],
        out_specs=pl.BlockSpec((TM_OUT, D_MODEL), lambda i: (i, 0)),
        out_shape=jax.ShapeDtypeStruct((m, D_MODEL), F32),
        compiler_params=_cparams(("parallel",)),
        name="outproj",
    )(x, y, w)


def _states(per_seq, nblk, ins, outs):
    if per_seq:
        def get(i):
            return tuple(r[i, 0] for r in ins)

        def put(i, vals):
            for o, val in zip(outs, vals):
                o[i, 0] = val
        return get, put

    @pl.when(pl.program_id(2) == 0)
    def _():
        for o, r in zip(outs, ins):
            o[...] = r[...]

    carried = {"v": tuple(o[0, 0] for o in outs)}

    def get(i):
        return carried["v"]

    def put(i, vals):
        carried["v"] = tuple(vals)
        if i == nblk - 1:
            for o, val in zip(outs, vals):
                o[0, 0] = val
    return get, put


def _ret_kernel(q_ref, k_ref, v_ref, gt_ref, cos_ref, sin_ref, gn_ref, s_in, *rest, blk, nblk, per_seq):
    y_ref, s_out = rest[-2:]
    rows = blk * nblk
    hf = jnp.full((1, 1), pl.program_id(1), jnp.int32).astype(F32)
    lg = jnp.log1p(-jnp.exp2(-5.0 - hf))
    cos = _cat([cos_ref[...]] * (rows // cos_ref.shape[0]))
    sin = _cat([sin_ref[...]] * (rows // sin_ref.shape[0]))
    q = q_ref[...]
    k = k_ref[...]
    q = q * cos + pltpu.roll(q, DK_A // 2, 1) * sin
    k = (k * cos + pltpu.roll(k, DK_A // 2, 1) * sin) * (DK_A ** -0.5)
    vb = v_ref[...].astype(BF16)
    mask, dist = _block_mask(rows, blk)
    dec = jnp.where(mask, jnp.exp(jnp.maximum(dist, 0).astype(F32) * lg), 0.0)
    a = _dot_nt(q.astype(BF16), k.astype(BF16)) * dec
    o = _dot(a.astype(BF16), vb)
    tloc = jnp.bitwise_and(lax.broadcasted_iota(jnp.int32, (rows, 1), 0), blk - 1).astype(F32)
    qt = (q * jnp.exp((tloc + 1.0) * lg)).astype(BF16)
    kw = (k * jnp.exp((blk - 1.0 - tloc) * lg)).astype(BF16)
    e_blk = jnp.exp(blk * lg)
    get, put = _states(per_seq, nblk, (s_in,), (s_out,))
    parts = []
    for i in range(nblk):
        r = slice(i * blk, (i + 1) * blk)
        s, = get(i)
        parts.append(o[r] + _dot(qt[r], s.astype(BF16)))
        put(i, (e_blk * s + _dot_tn(kw[r], vb[r]),))
    y_ref[...] = _head_out(_cat(parts), gn_ref[...], _silu(gt_ref[...]))


def _key_decay_slab(q, k, v, g, mk_ref, blk, nblk, get, put):
    rows = q.shape[0]
    tile = min(rows, PAIR_TILE)
    tiles = [slice(i * tile, (i + 1) * tile) for i in range(rows // tile)]
    g2 = g * LOG2E
    b = _prefix(g2, blk)
    b_last = _block_last(b, blk)
    qt = (q * jnp.exp2(b)).astype(BF16)
    kw = k * jnp.exp2(b_last - b)
    e_last = jnp.exp2(b_last)
    vb = v.astype(BF16)
    qb, kb = q.astype(BF16), k.astype(BF16)
    a = {(i, i): mk_ref[0] * _dot_nt(qb[t], kb[t]) for i, t in enumerate(tiles)}
    for li, m in enumerate(_levels(blk)):
        w = jnp.exp2(_level_exponent(b, g2, m))
        ql, kl = (q * w).astype(BF16), (k * w).astype(BF16)
        if m < tile:
            for i, t in enumerate(tiles):
                a[i, i] = a[i, i] + mk_ref[li + 1] * _dot_nt(ql[t], kl[t])
        else:
            for r0 in range(0, rows, 2 * m):
                for i in range((r0 + m) // tile, (r0 + 2 * m) // tile):
                    for j in range(r0 // tile, (r0 + m) // tile):
                        a[i, j] = _dot_nt(ql[tiles[i]], kl[tiles[j]])
    o_tiles = []
    for i in range(len(tiles)):
        terms = [_dot(a[i, j].astype(BF16), vb[tiles[j]]) for j in range(len(tiles)) if (i, j) in a]
        o_tiles.append(functools.reduce(lambda x, y: x + y, terms))
    o = _cat(o_tiles)
    parts = []
    for i in range(nblk):
        r = slice(i * blk, (i + 1) * blk)
        s, = get(i)
        parts.append(o[r] + _dot(qt[r], s.astype(BF16)))
        put(i, (_state_update(kw[r], v[r], e_last[i * blk:i * blk + 1], s),))
    return _cat(parts)


def _hgrn_kernel(q_ref, f_ref, i_ref, gt_ref, lb_ref, gn_ref, *rest, blk, nblk, per_seq, layer):
    mk_ref, s_in = rest[:2]
    y_ref, s_out = rest[-2:]
    lbl = lb_ref[...]
    e = jnp.exp(lbl - jnp.max(lbl, axis=0, keepdims=True))
    lb = jnp.sum(e[0:layer + 1], axis=0, keepdims=True) / jnp.sum(e, axis=0, keepdims=True)
    get, put = _states(per_seq, nblk, (s_in,), (s_out,))
    k = (1.0 - lb) * (1.0 - jax.nn.sigmoid(f_ref[...]))
    o = _key_decay_slab(_silu(q_ref[...]), k, i_ref[...], jnp.log(1.0 - k), mk_ref, blk, nblk, get, put)
    y_ref[...] = _head_out(o, gn_ref[...], _silu(gt_ref[...]))


def _gla_kernel(q_ref, k_ref, v_ref, gt_ref, tail_ref, w2_ref, ab_ref, gn_ref, *rest, blk, nblk, per_seq):
    mk_ref, s_in = rest[:2]
    y_ref, s_out = rest[-2:]
    wh, wm, wl = (p.astype(BF16) for p in _pieces3(w2_ref[...]))
    get, put = _states(per_seq, nblk, (s_in,), (s_out,))
    th, tm_, tl = (p.astype(BF16) for p in _pieces3(tail_ref[...]))
    x = (_dot(th, wh) + (_dot(th, wm) + _dot(tm_, wh))
         + (_dot(th, wl) + _dot(tm_, wm) + _dot(tl, wh)))
    g = _log_sigmoid(x + ab_ref[...]) / GLA_TAU
    o = _key_decay_slab(q_ref[...] * (DK_D ** -0.5), k_ref[...], v_ref[...], g, mk_ref, blk, nblk, get, put)
    y_ref[...] = _head_out(o, gn_ref[...], _silu(gt_ref[...]))


def _mlstm_kernel(q_ref, k_ref, v_ref, og_ref, tail_ref, wq_ref, wk_ref, bias_ref, gn_ref,
                  cq_in, ck_in, c_in, n_in, m_in, *rest, blk, nblk, per_seq):
    y_ref, c_out, n_out, m_out, cq_out, ck_out, uq_ref, uk_ref, hq_ref, hk_ref = rest[-10:]
    assert per_seq or nblk == 1
    rows = blk * nblk
    h = pl.program_id(1)
    hist = CONV_W - 1
    lo = SUBLANES - hist

    if not per_seq:
        @pl.when(pl.program_id(2) == 0)
        def _():
            for h_ref, cin in ((hq_ref, cq_in), (hk_ref, ck_in)):
                h_ref[...] = jnp.zeros_like(h_ref)
                h_ref[lo:SUBLANES, :] = cin[0]

    def conv(u_ref, h_ref, cin, cout, raw_ref, w_ref):
        parts = []
        for i in range(nblk):
            raw = raw_ref[i * blk:(i + 1) * blk, :]
            if per_seq:
                u_ref[i, lo:SUBLANES, :] = cin[i]
            else:
                u_ref[i, 0:SUBLANES, :] = h_ref[...]
                h_ref[...] = raw[blk - SUBLANES:blk]
            u_ref[i, SUBLANES:SUBLANES + blk, :] = raw
            cout[i] = raw[blk - hist:blk]
            acc = u_ref[i, lo:lo + blk, :] * w_ref[0:1, :]
            for j in range(1, CONV_W):
                acc = acc + u_ref[i, lo + j:lo + j + blk, :] * w_ref[j:j + 1, :]
            parts.append(acc)
        return _silu(_cat(parts))

    q = conv(uq_ref, hq_ref, cq_in, cq_out, q_ref, wq_ref)
    k = conv(uk_ref, hk_ref, ck_in, ck_out, k_ref, wk_ref) * (DK_C ** -0.5)
    qb = q.astype(BF16)
    vb = v_ref[...].astype(BF16)

    lane = lax.broadcasted_iota(jnp.int32, (rows, LANES), 1)
    tl = tail_ref[...] + bias_ref[...]
    li = jnp.sum(jnp.where(lane == h, tl, 0.0), axis=-1, keepdims=True)
    lf = jnp.sum(jnp.where(lane == H_C + h, _log_sigmoid(tl), 0.0), axis=-1, keepdims=True)
    b = _prefix(jnp.broadcast_to(lf, (rows, LANES)), blk)[:, :1]

    get, put = _states(per_seq, nblk, (c_in, n_in, m_in), (c_out, n_out, m_out))
    states = [get(i) for i in range(nblk)] if per_seq else [get(0)]
    m_prev = _cat([jnp.broadcast_to(st[2][:, :1], (blk, 1)) for st in states])

    mask, _ = _block_mask(rows, blk)
    d = jnp.where(mask, b + _row_bcast(li - b), -jnp.inf)
    m_inter = b + m_prev
    m_t = jnp.maximum(m_inter, jnp.max(d, axis=-1, keepdims=True))
    w_inter = jnp.exp(m_inter - m_t)
    sc = _dot_nt(qb, k.astype(BF16)) * jnp.exp(d - m_t)
    num = _dot(sc.astype(BF16), vb)
    den = jnp.sum(sc, axis=-1, keepdims=True)
    m_new = _block_last(m_t, blk)
    b_last = _block_last(b, blk)
    kw = k * jnp.exp(b_last - b + li - m_new)
    w_c = jnp.exp(b_last + m_prev - m_new)
    nums, dens = [], []
    for i in range(nblk):
        r = slice(i * blk, (i + 1) * blk)
        cst, nrow, _ = states[i]
        nums.append(num[r] + w_inter[r] * _dot(qb[r], cst.astype(BF16)))
        dens.append(den[r] + w_inter[r] * jnp.sum(q[r] * nrow, axis=-1, keepdims=True))
        wc = w_c[i * blk:i * blk + 1]
        put(i, (wc * cst + _dot_tn(kw[r].astype(BF16), vb[r]),
                wc * nrow + jnp.sum(kw[r], axis=0, keepdims=True),
                jnp.broadcast_to(m_new[i * blk:i * blk + 1], (1, LANES))))
    hh = _cat(nums) / jnp.maximum(jnp.abs(_cat(dens)), jnp.exp(-m_t))
    y_ref[...] = _head_out(hh, gn_ref[...], jax.nn.sigmoid(og_ref[...]))


def _geometry(n_seq, seq_len, row0, kind):
    slab, chunk = PROMPT_SLAB[kind]
    if seq_len >= slab:
        blk, rows, per_seq, nseq_blk = chunk, slab, False, 1
        grid_seq, grid_slab = n_seq, seq_len // slab
    else:
        blk, nseq_blk, per_seq = seq_len, SEQ_PER_STEP, True
        rows = nseq_blk * blk
        grid_seq, grid_slab = n_seq // nseq_blk, 1
    base = row0 // rows
    return dict(blk=blk, rows=rows, nblk=rows // blk, per_seq=per_seq, nseq_blk=nseq_blk,
                grid_seq=grid_seq, grid_slab=grid_slab, rowblk=lambda b, s: base + b * grid_slab + s)


def _zspec(geo, width, col):
    rb = geo["rowblk"]
    return pl.BlockSpec((geo["rows"], width), lambda b, h, s: (rb(b, s), col + h))


def _const_spec(shape):
    return pl.BlockSpec(shape, lambda b, h, s: (0,) * len(shape))


def _head_spec(width, col=0):
    return pl.BlockSpec((1, width), lambda b, h, s: (0, col + h))


def _state_spec(geo, shape):
    return pl.BlockSpec((geo["nseq_blk"], 1) + shape, lambda b, h, s: (b, h, 0, 0))


def _mixer_call(kern, geo, heads, vd, y_col, y_prev, n_tok, in_specs, args, extra_specs, extra_shapes,
                scratch, name):
    if y_prev is not None:
        in_specs = in_specs + [pl.BlockSpec(memory_space=pl.ANY)]
        args = args + [y_prev]
    rb = geo["rowblk"]
    return pl.pallas_call(
        kern,
        grid=(geo["grid_seq"], heads, geo["grid_slab"]),
        in_specs=in_specs,
        out_specs=[pl.BlockSpec((geo["rows"], vd), lambda b, h, s: (rb(b, s), y_col // vd + h))] + extra_specs,
        out_shape=[jax.ShapeDtypeStruct((n_tok, MIX), BF16)] + extra_shapes,
        input_output_aliases={} if y_prev is None else {len(args) - 1: 0},
        scratch_shapes=scratch,
        compiler_params=_cparams(("parallel", "parallel", "arbitrary")),
        name=name + ("_sample" if geo["per_seq"] else "_prompt"),
    )(*args)


def _decay_mixer(kind, z, geo, state, extra, y_prev, y_col):
    heads, kd, vd = {"ret": (H_A, DK_A, DV_A), "hgrn": (H_B, DK_B, DV_B), "gla": (H_D, DK_D, DV_D)}[kind]
    common = dict(blk=geo["blk"], nblk=geo["nblk"], per_seq=geo["per_seq"])
    kb, vb = LANES, vd
    if kind == "ret":
        cos, sin = extra["rope"]
        tab = pl.BlockSpec((geo["blk"] if geo["per_seq"] else geo["rows"], DK_A),
                           lambda b, h, s: (0 if geo["per_seq"] else s, 0))
        in_specs = [_zspec(geo, kb, 0), _zspec(geo, kb, H_A), _zspec(geo, vb, (2 * H_A * DK_A) // vb),
                    _zspec(geo, vb, (2 * H_A * DK_A) // vb + H_A), tab, tab, _head_spec(vd)]
        args = [z, z, z, z, cos, sin, extra["gn"].reshape(1, -1)]
        kern = functools.partial(_ret_kernel, **common)
    elif kind == "hgrn":
        off = (2 * H_A * DK_A + 2 * H_A * DV_A) // LANES
        in_specs = [_zspec(geo, kb, off), _zspec(geo, kb, off + H_B), _zspec(geo, kb, off + 2 * H_B),
                    _zspec(geo, kb, off + 3 * H_B),
                    pl.BlockSpec((DEPTH + 1, DK_B), lambda b, h, s: (0, h)), _head_spec(vd)]
        args = [z, z, z, z, extra["lb_logits"], extra["gn"].reshape(1, -1)]
        kern = functools.partial(_hgrn_kernel, layer=extra["layer"], **common)
    else:
        off = (2 * H_C * DK_C + 2 * H_C * DV_C) // LANES
        voff = (2 * H_C * DK_C + 2 * H_C * DV_C + 2 * H_D * DK_D) // vb
        in_specs = [_zspec(geo, kb, off), _zspec(geo, kb, off + H_D), _zspec(geo, vb, voff),
                    _zspec(geo, vb, voff + H_D),
                    pl.BlockSpec((geo["rows"], CD_TAIL), lambda b, h, s: (geo["rowblk"](b, s), CD_MAIN // CD_TAIL)),
                    pl.BlockSpec((CD_TAIL, DK_D), lambda b, h, s: (0, h)),
                    _head_spec(DK_D), _head_spec(vd)]
        args = [z, z, z, z, z, extra["w2"], extra["ab"].reshape(1, -1), extra["gn"].reshape(1, -1)]
        kern = functools.partial(_gla_kernel, **common)
    if kind != "ret":
        tile = min(geo["rows"], PAIR_TILE)
        mk = _pair_masks(tile, min(geo["blk"], tile))
        in_specs.append(_const_spec(mk.shape))
        args.append(mk)
    in_specs.append(_state_spec(geo, (kd, vd)))
    args.append(state)
    return _mixer_call(kern, geo, heads, vd, y_col, y_prev, z.shape[0], in_specs, args,
                       [_state_spec(geo, (kd, vd))], [jax.ShapeDtypeStruct(state.shape, F32)], [], kind)


def _mlstm_mixer(z, geo, c0, n0, m0, conv0, w, y_prev, y_col):
    n_seq = c0.shape[0]
    blk, nb = geo["blk"], geo["nseq_blk"]
    bias = jnp.zeros((1, LANES), F32).at[0, :H_C].set(w["i_bias"]).at[0, H_C:2 * H_C].set(w["f_bias"])
    kk = (H_C * DK_C) // DK_C
    kv = (2 * H_C * DK_C) // DV_C
    conv_spec = lambda col: pl.BlockSpec((nb, CONV_W - 1, DK_C), lambda b, h, s: (b, 0, col + h))
    in_specs = [
        _zspec(geo, DK_C, 0), _zspec(geo, DK_C, kk), _zspec(geo, DV_C, kv), _zspec(geo, DV_C, kv + H_C),
        pl.BlockSpec((geo["rows"], CD_TAIL), lambda b, h, s: (geo["rowblk"](b, s), CD_MAIN // CD_TAIL)),
        pl.BlockSpec((CONV_W, DK_C), lambda b, h, s: (0, h)),
        pl.BlockSpec((CONV_W, DK_C), lambda b, h, s: (0, H_C + h)),
        _const_spec((1, LANES)), _head_spec(DV_C),
        conv_spec(0), conv_spec(H_C),
        _state_spec(geo, (DK_C, DV_C)), _state_spec(geo, (1, DK_C)), _state_spec(geo, (1, LANES)),
    ]
    n4 = n0.reshape(n_seq, H_C, 1, DK_C)
    m4 = jnp.broadcast_to(m0[:, :, None, None], (n_seq, H_C, 1, LANES))
    args = [z, z, z, z, z, w["conv_w"], w["conv_w"], bias, w["gn_c"].reshape(1, -1),
            conv0, conv0, c0, n4, m4]
    conv_shape = jax.ShapeDtypeStruct((n_seq, CONV_W - 1, H_C * DK_C), F32)
    y, c_new, n_new, m_new, cq, ck = _mixer_call(
        functools.partial(_mlstm_kernel, blk=blk, nblk=geo["nblk"], per_seq=geo["per_seq"]),
        geo, H_C, DV_C, y_col, y_prev, z.shape[0], in_specs, args,
        [_state_spec(geo, (DK_C, DV_C)), _state_spec(geo, (1, DK_C)), _state_spec(geo, (1, LANES)),
         conv_spec(0), conv_spec(0)],
        [jax.ShapeDtypeStruct(c0.shape, F32), jax.ShapeDtypeStruct(n4.shape, F32),
         jax.ShapeDtypeStruct(m4.shape, F32), conv_shape, conv_shape],
        [pltpu.VMEM((geo["nblk"], SUBLANES + blk, DK_C), F32), pltpu.VMEM((geo["nblk"], SUBLANES + blk, DK_C), F32),
         pltpu.VMEM((SUBLANES, DK_C), F32), pltpu.VMEM((SUBLANES, DK_C), F32)],
        "mlstm")
    return (y, c_new, n_new.reshape(n_seq, H_C, DK_C), m_new[:, :, 0, 0],
            jnp.concatenate([cq, ck], axis=-1))


def _rope_tables(pos0, n):
    half = DK_A // 2
    inv = ROPE_BASE ** (-jnp.arange(half, dtype=F32) / half)
    ang = (jnp.arange(n, dtype=F32) + float(pos0))[:, None] * inv[None, :]
    cos, sin = jnp.cos(ang), jnp.sin(ang)
    return jnp.concatenate([cos, cos], axis=1), jnp.concatenate([-sin, sin], axis=1)


def kernel(x_prompt, x_sample, state_ret, state_hgrn, state_mlstm_c, state_mlstm_n, state_mlstm_m,
           state_mlstm_conv, state_gla, norm_gain, ffn_w1, ffn_w3, ffn_w2, ab_w_in, ab_w_out, ab_norm_a,
           ab_norm_b, lb_logits, cd_w_in, cd_w_out, cd_conv_w, cd_i_bias, cd_f_bias, cd_norm_c,
           cd_alpha_w2, cd_alpha_b, cd_norm_d, final_norm):
    bp, tp, _ = x_prompt.shape
    bs, ts, _ = x_sample.shape
    n_p, n_s = bp * tp, bs * ts
    x = jnp.concatenate([x_prompt.reshape(n_p, D_MODEL), x_sample.reshape(n_s, D_MODEL)], axis=0)
    w1, w3, w2 = ffn_w1.astype(BF16), ffn_w3.astype(BF16), ffn_w2.astype(BF16)
    groups = (("p", bp, {k: _geometry(bp, tp, 0, k) for k in PROMPT_SLAB}),
              ("s", bs, {k: _geometry(bs, ts, n_p, k) for k in PROMPT_SLAB}))
    rope = {"p": _rope_tables(0, tp), "s": _rope_tables(PAST_LEN, ts)}
    zeros = lambda *shape: jnp.zeros(shape, F32)
    outs = {}

    def keep(name, tag, val):
        outs.setdefault(name + "_" + tag, []).append(val)

    for l in range(DEPTH):
        j = l // 2
        x, hn = _ffn(x, norm_gain[l, 0], w1, w3, w2, l, 0, post="norm_out", post_gain=norm_gain[l, 1])
        y = None
        if l % 2 == 0:
            z = _inproj(hn, ab_w_in[j].astype(BF16))
            for tag, nseq, geo in groups:
                s_ret = zeros(nseq, H_A, DK_A, DV_A) if tag == "p" else state_ret[j]
                s_hg = zeros(nseq, H_B, DK_B, DV_B) if tag == "p" else state_hgrn[j]
                y, r_new = _decay_mixer("ret", z, geo["ret"], s_ret, dict(rope=rope[tag], gn=ab_norm_a[j]), y, 0)
                y, g_new = _decay_mixer("hgrn", z, geo["hgrn"], s_hg,
                                        dict(lb_logits=lb_logits, gn=ab_norm_b[j], layer=l), y, H_A * DV_A)
                keep("ret", tag, r_new)
                keep("hgrn", tag, g_new)
            x = _outproj(x, y, ab_w_out[j].astype(BF16))
        else:
            gates = 2 * H_C
            main = 2 * H_C * DK_C + 2 * H_C * DV_C
            rest = CD_MAIN - main
            n_pad = -(-(CD_MAIN + CD_TAIL) // (N_SPLIT * LANES)) * (N_SPLIT * LANES)
            lane_pad = -cd_w_in.shape[2] % LANES
            w_perm = _permute_cols(jnp.pad(cd_w_in.astype(BF16), ((0, 0), (0, 0), (0, lane_pad))), j, ((0, main), (main + gates, main + gates + rest),
                                                (main, main + gates), (main + gates + rest, cd_w_in.shape[2])), n_pad)
            z = _inproj(hn, w_perm)
            w2p = jnp.zeros((CD_TAIL, H_D * DK_D), F32).at[gates:gates + GLA_RANK].set(cd_alpha_w2[j])
            wts = dict(conv_w=cd_conv_w[j], i_bias=cd_i_bias[j], f_bias=cd_f_bias[j], gn_c=cd_norm_c[j])
            for tag, nseq, geo in groups:
                if tag == "p":
                    c0, n0, m0 = zeros(nseq, H_C, DK_C, DV_C), zeros(nseq, H_C, DK_C), zeros(nseq, H_C)
                    cv0, s_gl = zeros(nseq, CONV_W - 1, 2 * H_C * DK_C), zeros(nseq, H_D, DK_D, DV_D)
                else:
                    c0, n0, m0 = state_mlstm_c[j], state_mlstm_n[j], state_mlstm_m[j]
                    cv0, s_gl = state_mlstm_conv[j], state_gla[j]
                y, c_new, n_new, m_new, cv_new = _mlstm_mixer(z, geo["mlstm"], c0, n0, m0, cv0, wts, y, 0)
                y, gl_new = _decay_mixer("gla", z, geo["gla"], s_gl,
                                         dict(w2=w2p, ab=cd_alpha_b[j], gn=cd_norm_d[j]), y, H_C * DV_C)
                for name, val in (("c", c_new), ("n", n_new), ("m", m_new), ("conv", cv_new), ("gla", gl_new)):
                    keep(name, tag, val)
            x = _outproj(x, y, cd_w_out[j].astype(BF16))
        last = l == DEPTH - 1
        x = _ffn(x, norm_gain[l, 2], w1, w3, w2, l, 1, post="final" if last else None,
                 post_gain=final_norm if last else None)

    y_p = x[:n_p].reshape(bp, tp, D_MODEL)
    y_s = x[n_p:].reshape(bs, ts, D_MODEL)
    st = lambda name: jnp.stack(outs[name])
    return (y_p, y_s, st("ret_p"), st("ret_s"), st("hgrn_p"), st("hgrn_s"), st("c_p"), st("c_s"),
            st("n_p"), st("n_s"), st("m_p"), st("m_s"), st("conv_p"), st("conv_s"), st("gla_p"), st("gla_s"))
```

```python
import functools

import numpy as np
import jax
import jax.numpy as jnp
from jax import lax
from jax.experimental import pallas as pl
from jax.experimental.pallas import tpu as pltpu

F32 = jnp.float32
BF16 = jnp.bfloat16

D_MODEL = 2048
DEPTH = 2
PAST_LEN = 16384
H_A, DK_A, DV_A = 4, 128, 256
ROPE_BASE = 10000.0
H_B, DK_B, DV_B = 8, 128, 128
H_C, DK_C, DV_C = 4, 256, 256
CONV_W = 4
H_D, DK_D, DV_D = 4, 128, 256
GLA_RANK = 16
GLA_TAU = 16.0
FF_DIM = 5632
EPS = 1e-6

CD_MAIN = 2 * H_C * DK_C + 2 * H_C * DV_C + 2 * H_D * DK_D + 2 * H_D * DV_D
CD_TAIL = 128
MIX = H_A * DV_A + H_B * DV_B

LANES = 128
SUBLANES = 8
VMEM_LIMIT = 56 * 1024 * 1024

TM = 768
TM_WIDE = 1024
TM_IN = 1152
TM_OUT = 512
TF = 512
N_SPLIT = 4
PROMPT_SLAB = {"ret": (512, 512), "hgrn": (1024, 256), "mlstm": (256, 256), "gla": (1024, 256)}
SEQ_PER_STEP = 32
PAIR_TILE = 128
LOG2E = 1.4426950408889634


def _cparams(sem):
    return pltpu.CompilerParams(dimension_semantics=sem, vmem_limit_bytes=VMEM_LIMIT)


def _rms(x, g):
    return x * lax.rsqrt(jnp.mean(x * x, axis=-1, keepdims=True) + EPS) * g


def _silu(x):
    return x * jax.nn.sigmoid(x)


def _log_sigmoid(x):
    return jnp.minimum(x, 0.0) - jnp.log1p(jnp.exp(-jnp.abs(x)))


def _pieces3(x):
    hi = x.astype(BF16).astype(F32)
    r1 = x - hi
    mid = r1.astype(BF16).astype(F32)
    lo = (r1 - mid).astype(BF16).astype(F32)
    return hi, mid, lo


def _dot(a, b):
    return jnp.dot(a, b, preferred_element_type=F32)


def _dot_nt(a, b):
    return lax.dot_general(a, b, (((1,), (1,)), ((), ())), preferred_element_type=F32)


def _dot_tn(a, b):
    return lax.dot_general(a, b, (((0,), (0,)), ((), ())), preferred_element_type=F32)


def _cat(parts, axis=0):
    return parts[0] if len(parts) == 1 else jnp.concatenate(parts, axis=axis)


def _prefix(g, blk):
    rows, n = g.shape
    row = lax.broadcasted_iota(jnp.int32, (SUBLANES, n), 0)
    outs = []
    off = None
    for r0 in range(0, rows, SUBLANES):
        x = g[r0:r0 + SUBLANES]
        for sh in (1, 2, 4):
            x = x + jnp.where(row >= sh, pltpu.roll(x, sh, 0), 0.0)
        if r0 % blk:
            x = x + off
        off = x[SUBLANES - 1:SUBLANES]
        outs.append(x)
    return _cat(outs)


def _block_last(x, blk):
    rows, n = x.shape
    return _cat([jnp.broadcast_to(x[r0 + blk - 1:r0 + blk], (blk, n)) for r0 in range(0, rows, blk)])


def _block_mask(rows, blk):
    tt = lax.broadcasted_iota(jnp.int32, (rows, rows), 0)
    ss = lax.broadcasted_iota(jnp.int32, (rows, rows), 1)
    mask = ss <= tt
    if blk < rows:
        sh = blk.bit_length() - 1
        mask = jnp.logical_and(mask, lax.shift_right_logical(tt, sh) == lax.shift_right_logical(ss, sh))
    return mask, tt - ss


def _row_bcast(col):
    c = col.shape[0]
    lane = lax.broadcasted_iota(jnp.int32, (c, LANES), 1)
    pieces = _pieces3(jnp.where(lane == 0, jnp.broadcast_to(col, (c, LANES)), 0.0))
    rm = jnp.concatenate(pieces, axis=1).astype(BF16)
    ones = jnp.ones((c, 3 * LANES), BF16)
    return _dot_nt(ones, rm)


def _levels(blk):
    out = []
    m = 1
    while m < blk:
        out.append(m)
        m *= 2
    return tuple(out)


def _pair_masks(rows, blk):
    t = np.arange(rows)[:, None]
    u = np.arange(rows)[None, :]
    masks = [t == u] + [((t % (2 * m)) >= m) & ((u % (2 * m)) < m) & (t // (2 * m) == u // (2 * m))
                        for m in _levels(blk)]
    return jnp.asarray(np.stack(masks).astype(np.float32))


def _level_exponent(b, g, m):
    rows, n = b.shape
    if m >= SUBLANES:
        mid = _cat([jnp.broadcast_to(b[r0 + m - 1:r0 + m], (2 * m, n)) for r0 in range(0, rows, 2 * m)])
        return -jnp.abs(b - mid)
    row = lax.broadcasted_iota(jnp.int32, (SUBLANES, n), 0)
    outs = []
    for r0 in range(0, rows, SUBLANES):
        x = b[r0:r0 + SUBLANES]
        if m == 1:
            outs.append(jnp.where(jnp.bitwise_and(row, 1) == 1, g[r0:r0 + SUBLANES], 0.0))
            continue
        if m == 2:
            mid = jnp.where(row < 4, jnp.broadcast_to(x[1:2], x.shape), jnp.broadcast_to(x[5:6], x.shape))
        else:
            mid = jnp.broadcast_to(x[3:4], x.shape)
        outs.append(-jnp.abs(x - mid))
    return _cat(outs)


def _state_update(kw, v, e_row, s):
    c, kd = kw.shape
    vd = v.shape[1]
    pad = 2 * SUBLANES
    lhs = jnp.concatenate(_pieces3(e_row) + (jnp.zeros((pad - 3, kd), F32), kw), axis=0)
    sel = (lax.broadcasted_iota(jnp.int32, (pad, LANES), 0) < 3).astype(F32)
    rhs = jnp.concatenate(
        [jnp.concatenate([jnp.zeros((pad, vd), F32), sel], axis=1),
         jnp.concatenate([v, jnp.zeros((c, LANES), F32)], axis=1)], axis=0)
    upd = _dot_tn(lhs.astype(BF16), rhs.astype(BF16))
    e_col = upd[:, vd:]
    if vd > LANES:
        e_col = jnp.concatenate([e_col] * (vd // LANES), axis=1)
    return e_col * s + upd[:, :vd]


def _head_out(o, gn, gate_act):
    return (o * lax.rsqrt(jnp.mean(o * o, axis=-1, keepdims=True) + EPS) * gn * gate_act).astype(BF16)


def _ffn_kernel(*refs, nf, post):
    x_ref, g_ref, w1_ref, w3_ref, w2_ref = refs[:5]
    pg_ref = refs[5] if post else None
    outs = refs[6 if post else 5:-1]
    xn_ref = refs[-1]
    acc_ref = outs[0]
    f = pl.program_id(1)

    @pl.when(f == 0)
    def _():
        xn_ref[...] = _rms(x_ref[...], g_ref[...]).astype(BF16)
        acc_ref[...] = jnp.zeros_like(acc_ref)

    xn = xn_ref[...]
    h1 = _dot(xn, w1_ref[...])
    h3 = _dot(xn, w3_ref[...])
    acc_ref[...] += _dot((_silu(h1) * h3).astype(BF16), w2_ref[...])

    @pl.when(f == nf - 1)
    def _():
        y = x_ref[...] + 0.5 * acc_ref[...]
        if post == "final":
            outs[0][...] = _rms(y, pg_ref[...])
        else:
            outs[0][...] = y
            if post == "norm_out":
                outs[1][...] = _rms(y, pg_ref[...]).astype(BF16)


def _ffn(x, gain, w1, w3, w2, l, idx, post=None, post_gain=None):
    m = x.shape[0]
    nf = FF_DIM // TF
    tm = TM if post else TM_WIDE
    in_specs = [
        pl.BlockSpec((tm, D_MODEL), lambda i, f: (i, 0)),
        pl.BlockSpec((1, D_MODEL), lambda i, f: (0, 0)),
        pl.BlockSpec((None, None, D_MODEL, TF), lambda i, f: (l, idx, 0, f)),
        pl.BlockSpec((None, None, D_MODEL, TF), lambda i, f: (l, idx, 0, f)),
        pl.BlockSpec((None, None, TF, D_MODEL), lambda i, f: (l, idx, f, 0)),
    ]
    args = [x, gain.reshape(1, D_MODEL), w1, w3, w2]
    if post:
        in_specs.append(pl.BlockSpec((1, D_MODEL), lambda i, f: (0, 0)))
        args.append(post_gain.reshape(1, D_MODEL))
    row_spec = pl.BlockSpec((tm, D_MODEL), lambda i, f: (i, 0))
    out_specs, out_shape = [row_spec], [jax.ShapeDtypeStruct((m, D_MODEL), F32)]
    if post == "norm_out":
        out_specs.append(row_spec)
        out_shape.append(jax.ShapeDtypeStruct((m, D_MODEL), BF16))
    res = pl.pallas_call(
        functools.partial(_ffn_kernel, nf=nf, post=post),
        grid=(m // tm, nf),
        in_specs=in_specs,
        out_specs=out_specs,
        out_shape=out_shape,
        scratch_shapes=[pltpu.VMEM((tm, D_MODEL), BF16)],
        compiler_params=_cparams(("parallel", "arbitrary")),
        name="ffn_" + post if post else "ffn",
    )(*args)
    return res if post == "norm_out" else res[0]


def _permute_cols_kernel(w_ref, o_ref, *, pieces):
    x = w_ref[...]
    parts = [x[:, a:b] for a, b in pieces]
    width = sum(b - a for a, b in pieces)
    parts.append(jnp.zeros((x.shape[0], o_ref.shape[1] - width), x.dtype))
    o_ref[...] = jnp.concatenate(parts, axis=1).astype(o_ref.dtype)


def _permute_cols(w, j, pieces, n_out, rows_per_step=256):
    k = w.shape[1]
    return pl.pallas_call(
        functools.partial(_permute_cols_kernel, pieces=pieces),
        grid=(k // rows_per_step,),
        in_specs=[pl.BlockSpec((None, rows_per_step, w.shape[2]), lambda i: (j, i, 0))],
        out_specs=pl.BlockSpec((rows_per_step, n_out), lambda i: (i, 0)),
        out_shape=jax.ShapeDtypeStruct((k, n_out), BF16),
        compiler_params=_cparams(("parallel",)),
        name="permute_cols",
    )(w)


def _inproj_kernel(h_ref, w_ref, z_ref):
    res = _dot(h_ref[...], w_ref[...])
    for c in range(z_ref.shape[0]):
        z_ref[c] = res[:, c * LANES:(c + 1) * LANES]


def _inproj(h, w):
    m, n = h.shape[0], w.shape[1]
    tn = n // N_SPLIT
    return pl.pallas_call(
        _inproj_kernel,
        grid=(N_SPLIT, m // TM_IN),
        in_specs=[
            pl.BlockSpec((TM_IN, D_MODEL), lambda j, i: (i, 0)),
            pl.BlockSpec((D_MODEL, tn), lambda j, i: (0, j)),
        ],
        out_specs=pl.BlockSpec((tn // LANES, TM_IN, LANES), lambda j, i: (j, i, 0)),
        out_shape=jax.ShapeDtypeStruct((n // LANES, m, LANES), F32),
        compiler_params=_cparams(("parallel", "parallel")),
        name="inproj",
    )(h, w)


def _outproj_kernel(x_ref, y_ref, w_ref, o_ref):
    o_ref[...] = x_ref[...] + _dot(y_ref[...], w_ref[...])


def _outproj(x, y, w):
    m, kdim = y.shape
    return pl.pallas_call(
        _outproj_kernel,
        grid=(m // TM_OUT,),
        in_specs=[
            pl.BlockSpec((TM_OUT, D_MODEL), lambda i: (i, 0)),
            pl.BlockSpec((TM_OUT, kdim), lambda i: (i, 0)),
            pl.BlockSpec((kdim, D_MODEL), lambda i: (0, 0)),
        ],
        out_specs=pl.BlockSpec((TM_OUT, D_MODEL), lambda i: (i, 0)),
        out_shape=jax.ShapeDtypeStruct((m, D_MODEL), F32),
        compiler_params=_cparams(("parallel",)),
        name="outproj",
    )(x, y, w)


def _states(per_seq, nblk, ins, outs):
    if per_seq:
        def get(i):
            return tuple(r[i, 0] for r in ins)

        def put(i, vals):
            for o, val in zip(outs, vals):
                o[i, 0] = val
        return get, put

    @pl.when(pl.program_id(2) == 0)
    def _():
        for o, r in zip(outs, ins):
            o[...] = r[...]

    carried = {"v": tuple(o[0, 0] for o in outs)}

    def get(i):
        return carried["v"]

    def put(i, vals):
        carried["v"] = tuple(vals)
        if i == nblk - 1:
            for o, val in zip(outs, vals):
                o[0, 0] = val
    return get, put


def _ret_kernel(q_ref, k_ref, v_ref, gt_ref, cos_ref, sin_ref, gn_ref, s_in, *rest, blk, nblk, per_seq):
    y_ref, s_out = rest[-2:]
    rows = blk * nblk
    hf = jnp.full((1, 1), pl.program_id(1), jnp.int32).astype(F32)
    lg = jnp.log1p(-jnp.exp2(-5.0 - hf))
    cos = _cat([cos_ref[...]] * (rows // cos_ref.shape[0]))
    sin = _cat([sin_ref[...]] * (rows // sin_ref.shape[0]))
    q = q_ref[...]
    k = k_ref[...]
    q = q * cos + pltpu.roll(q, DK_A // 2, 1) * sin
    k = (k * cos + pltpu.roll(k, DK_A // 2, 1) * sin) * (DK_A ** -0.5)
    vb = _rd(v_ref).astype(BF16)
    mask, dist = _block_mask(rows, blk)
    dec = jnp.where(mask, jnp.exp(jnp.maximum(dist, 0).astype(F32) * lg), 0.0)
    a = _dot_nt(q.astype(BF16), k.astype(BF16)) * dec
    o = _dot(a.astype(BF16), vb)
    tloc = jnp.bitwise_and(lax.broadcasted_iota(jnp.int32, (rows, 1), 0), blk - 1).astype(F32)
    qt = (q * jnp.exp((tloc + 1.0) * lg)).astype(BF16)
    kw = (k * jnp.exp((blk - 1.0 - tloc) * lg)).astype(BF16)
    e_blk = jnp.exp(blk * lg)
    get, put = _states(per_seq, nblk, (s_in,), (s_out,))
    parts = []
    for i in range(nblk):
        r = slice(i * blk, (i + 1) * blk)
        s, = get(i)
        parts.append(o[r] + _dot(qt[r], s.astype(BF16)))
        put(i, (e_blk * s + _dot_tn(kw[r], vb[r]),))
    y_ref[...] = _head_out(_cat(parts), gn_ref[...], _silu(_rd(gt_ref)))


def _key_decay_slab(q, k, v, g, mk_ref, blk, nblk, get, put):
    rows = q.shape[0]
    tile = min(rows, PAIR_TILE)
    tiles = [slice(i * tile, (i + 1) * tile) for i in range(rows // tile)]
    g2 = g * LOG2E
    b = _prefix(g2, blk)
    b_last = _block_last(b, blk)
    qt = (q * jnp.exp2(b)).astype(BF16)
    kw = k * jnp.exp2(b_last - b)
    e_last = jnp.exp2(b_last)
    vb = v.astype(BF16)
    qb, kb = q.astype(BF16), k.astype(BF16)
    a = {(i, i): mk_ref[0] * _dot_nt(qb[t], kb[t]) for i, t in enumerate(tiles)}
    for li, m in enumerate(_levels(blk)):
        w = jnp.exp2(_level_exponent(b, g2, m))
        ql, kl = (q * w).astype(BF16), (k * w).astype(BF16)
        if m < tile:
            for i, t in enumerate(tiles):
                a[i, i] = a[i, i] + mk_ref[li + 1] * _dot_nt(ql[t], kl[t])
        else:
            for r0 in range(0, rows, 2 * m):
                for i in range((r0 + m) // tile, (r0 + 2 * m) // tile):
                    for j in range(r0 // tile, (r0 + m) // tile):
                        a[i, j] = _dot_nt(ql[tiles[i]], kl[tiles[j]])
    o_tiles = []
    for i in range(len(tiles)):
        terms = [_dot(a[i, j].astype(BF16), vb[tiles[j]]) for j in range(len(tiles)) if (i, j) in a]
        o_tiles.append(functools.reduce(lambda x, y: x + y, terms))
    o = _cat(o_tiles)
    parts = []
    for i in range(nblk):
        r = slice(i * blk, (i + 1) * blk)
        s, = get(i)
        parts.append(o[r] + _dot(qt[r], s.astype(BF16)))
        put(i, (_state_update(kw[r], v[r], e_last[i * blk:i * blk + 1], s),))
    return _cat(parts)


def _hgrn_kernel(q_ref, f_ref, i_ref, gt_ref, lb_ref, gn_ref, *rest, blk, nblk, per_seq, layer):
    mk_ref, s_in = rest[:2]
    y_ref, s_out = rest[-2:]
    lbl = lb_ref[...]
    e = jnp.exp(lbl - jnp.max(lbl, axis=0, keepdims=True))
    lb = jnp.sum(e[0:layer + 1], axis=0, keepdims=True) / jnp.sum(e, axis=0, keepdims=True)
    get, put = _states(per_seq, nblk, (s_in,), (s_out,))
    k = (1.0 - lb) * (1.0 - jax.nn.sigmoid(f_ref[...]))
    o = _key_decay_slab(_silu(q_ref[...]), k, i_ref[...], jnp.log(1.0 - k), mk_ref, blk, nblk, get, put)
    y_ref[...] = _head_out(o, gn_ref[...], _silu(gt_ref[...]))


def _gla_kernel(q_ref, k_ref, v_ref, gt_ref, tail_ref, w2_ref, ab_ref, gn_ref, *rest, blk, nblk, per_seq):
    mk_ref, s_in = rest[:2]
    y_ref, s_out = rest[-2:]
    wh, wm, wl = (p.astype(BF16) for p in _pieces3(w2_ref[...]))
    get, put = _states(per_seq, nblk, (s_in,), (s_out,))
    th, tm_, tl = (p.astype(BF16) for p in _pieces3(tail_ref[...]))
    x = (_dot(th, wh) + (_dot(th, wm) + _dot(tm_, wh))
         + (_dot(th, wl) + _dot(tm_, wm) + _dot(tl, wh)))
    g = _log_sigmoid(x + ab_ref[...]) / GLA_TAU
    o = _key_decay_slab(q_ref[...] * (DK_D ** -0.5), k_ref[...], _rd(v_ref), g, mk_ref, blk, nblk, get, put)
    y_ref[...] = _head_out(o, gn_ref[...], _silu(_rd(gt_ref)))


def _mlstm_kernel(q_ref, k_ref, v_ref, og_ref, tail_ref, wq_ref, wk_ref, bias_ref, gn_ref,
                  cq_in, ck_in, c_in, n_in, m_in, *rest, blk, nblk, per_seq):
    y_ref, c_out, n_out, m_out, cq_out, ck_out, uq_ref, uk_ref, hq_ref, hk_ref = rest[-10:]
    assert per_seq or nblk == 1
    rows = blk * nblk
    h = pl.program_id(1)
    hist = CONV_W - 1
    lo = SUBLANES - hist

    if not per_seq:
        @pl.when(pl.program_id(2) == 0)
        def _():
            for h_ref, cin in ((hq_ref, cq_in), (hk_ref, ck_in)):
                h_ref[...] = jnp.zeros_like(h_ref)
                h_ref[lo:SUBLANES, :] = cin[0]

    def conv(u_ref, h_ref, cin, cout, raw_ref, w_ref):
        parts = []
        for i in range(nblk):
            raw = _rd(raw_ref, slice(i * blk, (i + 1) * blk))
            if per_seq:
                u_ref[i, lo:SUBLANES, :] = cin[i]
            else:
                u_ref[i, 0:SUBLANES, :] = h_ref[...]
                h_ref[...] = raw[blk - SUBLANES:blk]
            u_ref[i, SUBLANES:SUBLANES + blk, :] = raw
            cout[i] = raw[blk - hist:blk]
            acc = u_ref[i, lo:lo + blk, :] * w_ref[0:1, :]
            for j in range(1, CONV_W):
                acc = acc + u_ref[i, lo + j:lo + j + blk, :] * w_ref[j:j + 1, :]
            parts.append(acc)
        return _silu(_cat(parts))

    q = conv(uq_ref, hq_ref, cq_in, cq_out, q_ref, wq_ref)
    k = conv(uk_ref, hk_ref, ck_in, ck_out, k_ref, wk_ref) * (DK_C ** -0.5)
    qb = q.astype(BF16)
    vb = _rd(v_ref).astype(BF16)

    lane = lax.broadcasted_iota(jnp.int32, (rows, LANES), 1)
    tl = tail_ref[...] + bias_ref[...]
    li = jnp.sum(jnp.where(lane == h, tl, 0.0), axis=-1, keepdims=True)
    lf = jnp.sum(jnp.where(lane == H_C + h, _log_sigmoid(tl), 0.0), axis=-1, keepdims=True)
    b = _prefix(jnp.broadcast_to(lf, (rows, LANES)), blk)[:, :1]

    get, put = _states(per_seq, nblk, (c_in, n_in, m_in), (c_out, n_out, m_out))
    states = [get(i) for i in range(nblk)] if per_seq else [get(0)]
    m_prev = _cat([jnp.broadcast_to(st[2][:, :1], (blk, 1)) for st in states])

    mask, _ = _block_mask(rows, blk)
    d = jnp.where(mask, b + _row_bcast(li - b), -jnp.inf)
    m_inter = b + m_prev
    m_t = jnp.maximum(m_inter, jnp.max(d, axis=-1, keepdims=True))
    w_inter = jnp.exp(m_inter - m_t)
    sc = _dot_nt(qb, k.astype(BF16)) * jnp.exp(d - m_t)
    num = _dot(sc.astype(BF16), vb)
    den = jnp.sum(sc, axis=-1, keepdims=True)
    m_new = _block_last(m_t, blk)
    b_last = _block_last(b, blk)
    kw = k * jnp.exp(b_last - b + li - m_new)
    w_c = jnp.exp(b_last + m_prev - m_new)
    nums, dens = [], []
    for i in range(nblk):
        r = slice(i * blk, (i + 1) * blk)
        cst, nrow, _ = states[i]
        nums.append(num[r] + w_inter[r] * _dot(qb[r], cst.astype(BF16)))
        dens.append(den[r] + w_inter[r] * jnp.sum(q[r] * nrow, axis=-1, keepdims=True))
        wc = w_c[i * blk:i * blk + 1]
        put(i, (wc * cst + _dot_tn(kw[r].astype(BF16), vb[r]),
                wc * nrow + jnp.sum(kw[r], axis=0, keepdims=True),
                jnp.broadcast_to(m_new[i * blk:i * blk + 1], (1, LANES))))
    hh = _cat(nums) / jnp.maximum(jnp.abs(_cat(dens)), jnp.exp(-m_t))
    y_ref[...] = _head_out(hh, gn_ref[...], jax.nn.sigmoid(_rd(og_ref)))


def _geometry(n_seq, seq_len, row0, kind):
    slab, chunk = PROMPT_SLAB[kind]
    if seq_len >= slab:
        blk, rows, per_seq, nseq_blk = chunk, slab, False, 1
        grid_seq, grid_slab = n_seq, seq_len // slab
    else:
        blk, nseq_blk, per_seq = seq_len, SEQ_PER_STEP, True
        rows = nseq_blk * blk
        grid_seq, grid_slab = n_seq // nseq_blk, 1
    base = row0 // rows
    return dict(blk=blk, rows=rows, nblk=rows // blk, per_seq=per_seq, nseq_blk=nseq_blk,
                grid_seq=grid_seq, grid_slab=grid_slab, rowblk=lambda b, s: base + b * grid_slab + s)


def _zspec(geo, width, col):
    rb = geo["rowblk"]
    tiles = width // LANES
    return pl.BlockSpec((None if tiles == 1 else tiles, geo["rows"], LANES), lambda b, h, s: (col + h, rb(b, s), 0))


def _rd(ref, rows=slice(None)):
    if len(ref.shape) == 2:
        return ref[rows, :]
    return jnp.concatenate([ref[c, rows, :] for c in range(ref.shape[0])], axis=1)


def _const_spec(shape):
    return pl.BlockSpec(shape, lambda b, h, s: (0,) * len(shape))


def _head_spec(width, col=0):
    return pl.BlockSpec((1, width), lambda b, h, s: (0, col + h))


def _state_spec(geo, shape):
    return pl.BlockSpec((geo["nseq_blk"], 1) + shape, lambda b, h, s: (b, h, 0, 0))


def _mixer_call(kern, geo, heads, vd, y_col, y_prev, n_tok, in_specs, args, extra_specs, extra_shapes,
                scratch, name):
    if y_prev is not None:
        in_specs = in_specs + [pl.BlockSpec(memory_space=pl.ANY)]
        args = args + [y_prev]
    rb = geo["rowblk"]
    return pl.pallas_call(
        kern,
        grid=(geo["grid_seq"], heads, geo["grid_slab"]),
        in_specs=in_specs,
        out_specs=[pl.BlockSpec((geo["rows"], vd), lambda b, h, s: (rb(b, s), y_col // vd + h))] + extra_specs,
        out_shape=[jax.ShapeDtypeStruct((n_tok, MIX), BF16)] + extra_shapes,
        input_output_aliases={} if y_prev is None else {len(args) - 1: 0},
        scratch_shapes=scratch,
        compiler_params=_cparams(("parallel", "parallel", "arbitrary")),
        name=name + ("_sample" if geo["per_seq"] else "_prompt"),
    )(*args)


def _decay_mixer(kind, z, geo, state, extra, y_prev, y_col):
    heads, kd, vd = {"ret": (H_A, DK_A, DV_A), "hgrn": (H_B, DK_B, DV_B), "gla": (H_D, DK_D, DV_D)}[kind]
    common = dict(blk=geo["blk"], nblk=geo["nblk"], per_seq=geo["per_seq"])
    kb, vb = LANES, vd
    if kind == "ret":
        cos, sin = extra["rope"]
        tab = pl.BlockSpec((geo["blk"] if geo["per_seq"] else geo["rows"], DK_A),
                           lambda b, h, s: (0 if geo["per_seq"] else s, 0))
        in_specs = [_zspec(geo, kb, 0), _zspec(geo, kb, H_A), _zspec(geo, vb, (2 * H_A * DK_A) // vb),
                    _zspec(geo, vb, (2 * H_A * DK_A) // vb + H_A), tab, tab, _head_spec(vd)]
        args = [z, z, z, z, cos, sin, extra["gn"].reshape(1, -1)]
        kern = functools.partial(_ret_kernel, **common)
    elif kind == "hgrn":
        off = (2 * H_A * DK_A + 2 * H_A * DV_A) // LANES
        in_specs = [_zspec(geo, kb, off), _zspec(geo, kb, off + H_B), _zspec(geo, kb, off + 2 * H_B),
                    _zspec(geo, kb, off + 3 * H_B),
                    pl.BlockSpec((DEPTH + 1, DK_B), lambda b, h, s: (0, h)), _head_spec(vd)]
        args = [z, z, z, z, extra["lb_logits"], extra["gn"].reshape(1, -1)]
        kern = functools.partial(_hgrn_kernel, layer=extra["layer"], **common)
    else:
        off = (2 * H_C * DK_C + 2 * H_C * DV_C) // LANES
        voff = (2 * H_C * DK_C + 2 * H_C * DV_C + 2 * H_D * DK_D) // vb
        in_specs = [_zspec(geo, kb, off), _zspec(geo, kb, off + H_D), _zspec(geo, vb, voff),
                    _zspec(geo, vb, voff + H_D),
                    pl.BlockSpec((None, geo["rows"], CD_TAIL), lambda b, h, s: (CD_MAIN // CD_TAIL, geo["rowblk"](b, s), 0)),
                    pl.BlockSpec((CD_TAIL, DK_D), lambda b, h, s: (0, h)),
                    _head_spec(DK_D), _head_spec(vd)]
        args = [z, z, z, z, z, extra["w2"], extra["ab"].reshape(1, -1), extra["gn"].reshape(1, -1)]
        kern = functools.partial(_gla_kernel, **common)
    if kind != "ret":
        tile = min(geo["rows"], PAIR_TILE)
        mk = _pair_masks(tile, min(geo["blk"], tile))
        in_specs.append(_const_spec(mk.shape))
        args.append(mk)
    in_specs.append(_state_spec(geo, (kd, vd)))
    args.append(state)
    return _mixer_call(kern, geo, heads, vd, y_col, y_prev, z.shape[1], in_specs, args,
                       [_state_spec(geo, (kd, vd))], [jax.ShapeDtypeStruct(state.shape, F32)], [], kind)


def _mlstm_mixer(z, geo, c0, n0, m0, conv0, w, y_prev, y_col):
    n_seq = c0.shape[0]
    blk, nb = geo["blk"], geo["nseq_blk"]
    bias = jnp.zeros((1, LANES), F32).at[0, :H_C].set(w["i_bias"]).at[0, H_C:2 * H_C].set(w["f_bias"])
    kk = (H_C * DK_C) // DK_C
    kv = (2 * H_C * DK_C) // DV_C
    conv_spec = lambda col: pl.BlockSpec((nb, CONV_W - 1, DK_C), lambda b, h, s: (b, 0, col + h))
    in_specs = [
        _zspec(geo, DK_C, 0), _zspec(geo, DK_C, kk), _zspec(geo, DV_C, kv), _zspec(geo, DV_C, kv + H_C),
        pl.BlockSpec((None, geo["rows"], CD_TAIL), lambda b, h, s: (CD_MAIN // CD_TAIL, geo["rowblk"](b, s), 0)),
        pl.BlockSpec((CONV_W, DK_C), lambda b, h, s: (0, h)),
        pl.BlockSpec((CONV_W, DK_C), lambda b, h, s: (0, H_C + h)),
        _const_spec((1, LANES)), _head_spec(DV_C),
        conv_spec(0), conv_spec(H_C),
        _state_spec(geo, (DK_C, DV_C)), _state_spec(geo, (1, DK_C)), _state_spec(geo, (1, LANES)),
    ]
    n4 = n0.reshape(n_seq, H_C, 1, DK_C)
    m4 = jnp.broadcast_to(m0[:, :, None, None], (n_seq, H_C, 1, LANES))
    args = [z, z, z, z, z, w["conv_w"], w["conv_w"], bias, w["gn_c"].reshape(1, -1),
            conv0, conv0, c0, n4, m4]
    conv_shape = jax.ShapeDtypeStruct((n_seq, CONV_W - 1, H_C * DK_C), F32)
    y, c_new, n_new, m_new, cq, ck = _mixer_call(
        functools.partial(_mlstm_kernel, blk=blk, nblk=geo["nblk"], per_seq=geo["per_seq"]),
        geo, H_C, DV_C, y_col, y_prev, z.shape[1], in_specs, args,
        [_state_spec(geo, (DK_C, DV_C)), _state_spec(geo, (1, DK_C)), _state_spec(geo, (1, LANES)),
         conv_spec(0), conv_spec(0)],
        [jax.ShapeDtypeStruct(c0.shape, F32), jax.ShapeDtypeStruct(n4.shape, F32),
         jax.ShapeDtypeStruct(m4.shape, F32), conv_shape, conv_shape],
        [pltpu.VMEM((geo["nblk"], SUBLANES + blk, DK_C), F32), pltpu.VMEM((geo["nblk"], SUBLANES + blk, DK_C), F32),
         pltpu.VMEM((SUBLANES, DK_C), F32), pltpu.VMEM((SUBLANES, DK_C), F32)],
        "mlstm")
    return (y, c_new, n_new.reshape(n_seq, H_C, DK_C), m_new[:, :, 0, 0],
            jnp.concatenate([cq, ck], axis=-1))


def _rope_tables(pos0, n):
    half = DK_A // 2
    inv = ROPE_BASE ** (-jnp.arange(half, dtype=F32) / half)
    ang = (jnp.arange(n, dtype=F32) + float(pos0))[:, None] * inv[None, :]
    cos, sin = jnp.cos(ang), jnp.sin(ang)
    return jnp.concatenate([cos, cos], axis=1), jnp.concatenate([-sin, sin], axis=1)


def kernel(x_prompt, x_sample, state_ret, state_hgrn, state_mlstm_c, state_mlstm_n, state_mlstm_m,
           state_mlstm_conv, state_gla, norm_gain, ffn_w1, ffn_w3, ffn_w2, ab_w_in, ab_w_out, ab_norm_a,
           ab_norm_b, lb_logits, cd_w_in, cd_w_out, cd_conv_w, cd_i_bias, cd_f_bias, cd_norm_c,
           cd_alpha_w2, cd_alpha_b, cd_norm_d, final_norm):
    bp, tp, _ = x_prompt.shape
    bs, ts, _ = x_sample.shape
    n_p, n_s = bp * tp, bs * ts
    x = jnp.concatenate([x_prompt.reshape(n_p, D_MODEL), x_sample.reshape(n_s, D_MODEL)], axis=0)
    w1, w3, w2 = ffn_w1.astype(BF16), ffn_w3.astype(BF16), ffn_w2.astype(BF16)
    groups = (("p", bp, {k: _geometry(bp, tp, 0, k) for k in PROMPT_SLAB}),
              ("s", bs, {k: _geometry(bs, ts, n_p, k) for k in PROMPT_SLAB}))
    rope = {"p": _rope_tables(0, tp), "s": _rope_tables(PAST_LEN, ts)}
    zeros = lambda *shape: jnp.zeros(shape, F32)
    outs = {}

    def keep(name, tag, val):
        outs.setdefault(name + "_" + tag, []).append(val)

    for l in range(DEPTH):
        j = l // 2
        x, hn = _ffn(x, norm_gain[l, 0], w1, w3, w2, l, 0, post="norm_out", post_gain=norm_gain[l, 1])
        y = None
        if l % 2 == 0:
            z = _inproj(hn, ab_w_in[j].astype(BF16))
            for tag, nseq, geo in groups:
                s_ret = zeros(nseq, H_A, DK_A, DV_A) if tag == "p" else state_ret[j]
                s_hg = zeros(nseq, H_B, DK_B, DV_B) if tag == "p" else state_hgrn[j]
                y, r_new = _decay_mixer("ret", z, geo["ret"], s_ret, dict(rope=rope[tag], gn=ab_norm_a[j]), y, 0)
                y, g_new = _decay_mixer("hgrn", z, geo["hgrn"], s_hg,
                                        dict(lb_logits=lb_logits, gn=ab_norm_b[j], layer=l), y, H_A * DV_A)
                keep("ret", tag, r_new)
                keep("hgrn", tag, g_new)
            x = _outproj(x, y, ab_w_out[j].astype(BF16))
        else:
            gates = 2 * H_C
            main = 2 * H_C * DK_C + 2 * H_C * DV_C
            rest = CD_MAIN - main
            n_pad = -(-(CD_MAIN + CD_TAIL) // (N_SPLIT * LANES)) * (N_SPLIT * LANES)
            lane_pad = -cd_w_in.shape[2] % LANES
            w_perm = _permute_cols(jnp.pad(cd_w_in.astype(BF16), ((0, 0), (0, 0), (0, lane_pad))), j, ((0, main), (main + gates, main + gates + rest),
                                                (main, main + gates), (main + gates + rest, cd_w_in.shape[2])), n_pad)
            z = _inproj(hn, w_perm)
            w2p = jnp.zeros((CD_TAIL, H_D * DK_D), F32).at[gates:gates + GLA_RANK].set(cd_alpha_w2[j])
            wts = dict(conv_w=cd_conv_w[j], i_bias=cd_i_bias[j], f_bias=cd_f_bias[j], gn_c=cd_norm_c[j])
            for tag, nseq, geo in groups:
                if tag == "p":
                    c0, n0, m0 = zeros(nseq, H_C, DK_C, DV_C), zeros(nseq, H_C, DK_C), zeros(nseq, H_C)
                    cv0, s_gl = zeros(nseq, CONV_W - 1, 2 * H_C * DK_C), zeros(nseq, H_D, DK_D, DV_D)
                else:
                    c0, n0, m0 = state_mlstm_c[j], state_mlstm_n[j], state_mlstm_m[j]
                    cv0, s_gl = state_mlstm_conv[j], state_gla[j]
                y, c_new, n_new, m_new, cv_new = _mlstm_mixer(z, geo["mlstm"], c0, n0, m0, cv0, wts, y, 0)
                y, gl_new = _decay_mixer("gla", z, geo["gla"], s_gl,
                                         dict(w2=w2p, ab=cd_alpha_b[j], gn=cd_norm_d[j]), y, H_C * DV_C)
                for name, val in (("c", c_new), ("n", n_new), ("m", m_new), ("conv", cv_new), ("gla", gl_new)):
                    keep(name, tag, val)
            x = _outproj(x, y, cd_w_out[j].astype(BF16))
        last = l == DEPTH - 1
        x = _ffn(x, norm_gain[l, 2], w1, w3, w2, l, 1, post="final" if last else None,
                 post_gain=final_norm if last else None)

    y_p = x[:n_p].reshape(bp, tp, D_MODEL)
    y_s = x[n_p:].reshape(bs, ts, D_MODEL)
    st = lambda name: jnp.stack(outs[name])
    return (y_p, y_s, st("ret_p"), st("ret_s"), st("hgrn_p"), st("hgrn_s"), st("c_p"), st("c_s"),
            st("n_p"), st("n_s"), st("m_p"), st("m_s"), st("conv_p"), st("conv_s"), st("gla_p"), st("gla_s"))
```

```python
import functools

import numpy as np
import jax
import jax.numpy as jnp
from jax import lax
from jax.experimental import pallas as pl
from jax.experimental.pallas import tpu as pltpu

F32 = jnp.float32
BF16 = jnp.bfloat16

D_MODEL = 2048
DEPTH = 2
PAST_LEN = 16384
H_A, DK_A, DV_A = 4, 128, 256
ROPE_BASE = 10000.0
H_B, DK_B, DV_B = 8, 128, 128
H_C, DK_C, DV_C = 4, 256, 256
CONV_W = 4
H_D, DK_D, DV_D = 4, 128, 256
GLA_RANK = 16
GLA_TAU = 16.0
FF_DIM = 5632
EPS = 1e-6

CD_MAIN = 2 * H_C * DK_C + 2 * H_C * DV_C + 2 * H_D * DK_D + 2 * H_D * DV_D
CD_TAIL = 128
MIX = H_A * DV_A + H_B * DV_B

LANES = 128
SUBLANES = 8
VMEM_LIMIT = 56 * 1024 * 1024

TM = 768
TM_WIDE = 1024
TM_IN = 1152
TM_OUT = 512
TF = 512
N_SPLIT = 4
PROMPT_SLAB = {"ret": (1024, 512), "hgrn": (1024, 256), "mlstm": (512, 512), "gla": (1024, 256)}
SEQ_PER_STEP = 32
PAIR_TILE = 128
LOG2E = 1.4426950408889634


def _cparams(sem):
    return pltpu.CompilerParams(dimension_semantics=sem, vmem_limit_bytes=VMEM_LIMIT)


def _rms(x, g):
    return x * lax.rsqrt(jnp.mean(x * x, axis=-1, keepdims=True) + EPS) * g


def _silu(x):
    return x * jax.nn.sigmoid(x)


def _log_sigmoid(x):
    return jnp.minimum(x, 0.0) - jnp.log1p(jnp.exp(-jnp.abs(x)))


def _pieces3(x):
    hi = x.astype(BF16).astype(F32)
    r1 = x - hi
    mid = r1.astype(BF16).astype(F32)
    lo = (r1 - mid).astype(BF16).astype(F32)
    return hi, mid, lo


def _dot(a, b):
    return jnp.dot(a, b, preferred_element_type=F32)


def _dot_nt(a, b):
    return lax.dot_general(a, b, (((1,), (1,)), ((), ())), preferred_element_type=F32)


def _dot_tn(a, b):
    return lax.dot_general(a, b, (((0,), (0,)), ((), ())), preferred_element_type=F32)


def _cat(parts, axis=0):
    return parts[0] if len(parts) == 1 else jnp.concatenate(parts, axis=axis)


def _prefix(g, blk):
    rows, n = g.shape
    row = lax.broadcasted_iota(jnp.int32, (SUBLANES, n), 0)
    outs = []
    off = None
    for r0 in range(0, rows, SUBLANES):
        x = g[r0:r0 + SUBLANES]
        for sh in (1, 2, 4):
            x = x + jnp.where(row >= sh, pltpu.roll(x, sh, 0), 0.0)
        if r0 % blk:
            x = x + off
        off = x[SUBLANES - 1:SUBLANES]
        outs.append(x)
    return _cat(outs)


def _block_last(x, blk):
    rows, n = x.shape
    return _cat([jnp.broadcast_to(x[r0 + blk - 1:r0 + blk], (blk, n)) for r0 in range(0, rows, blk)])


def _block_mask(rows, blk):
    tt = lax.broadcasted_iota(jnp.int32, (rows, rows), 0)
    ss = lax.broadcasted_iota(jnp.int32, (rows, rows), 1)
    mask = ss <= tt
    if blk < rows:
        sh = blk.bit_length() - 1
        mask = jnp.logical_and(mask, lax.shift_right_logical(tt, sh) == lax.shift_right_logical(ss, sh))
    return mask, tt - ss


def _row_bcast(col):
    c = col.shape[0]
    lane = lax.broadcasted_iota(jnp.int32, (c, LANES), 1)
    pieces = _pieces3(jnp.where(lane == 0, jnp.broadcast_to(col, (c, LANES)), 0.0))
    rm = jnp.concatenate(pieces, axis=1).astype(BF16)
    ones = jnp.ones((c, 3 * LANES), BF16)
    return _dot_nt(ones, rm)


def _levels(blk):
    out = []
    m = 1
    while m < blk:
        out.append(m)
        m *= 2
    return tuple(out)


def _pair_masks(rows, blk):
    t = np.arange(rows)[:, None]
    u = np.arange(rows)[None, :]
    masks = [t == u] + [((t % (2 * m)) >= m) & ((u % (2 * m)) < m) & (t // (2 * m) == u // (2 * m))
                        for m in _levels(blk)]
    return jnp.asarray(np.stack(masks).astype(np.float32))


def _level_exponent(b, g, m):
    rows, n = b.shape
    if m >= SUBLANES:
        mid = _cat([jnp.broadcast_to(b[r0 + m - 1:r0 + m], (2 * m, n)) for r0 in range(0, rows, 2 * m)])
        return -jnp.abs(b - mid)
    row = lax.broadcasted_iota(jnp.int32, (SUBLANES, n), 0)
    outs = []
    for r0 in range(0, rows, SUBLANES):
        x = b[r0:r0 + SUBLANES]
        if m == 1:
            outs.append(jnp.where(jnp.bitwise_and(row, 1) == 1, g[r0:r0 + SUBLANES], 0.0))
            continue
        if m == 2:
            mid = jnp.where(row < 4, jnp.broadcast_to(x[1:2], x.shape), jnp.broadcast_to(x[5:6], x.shape))
        else:
            mid = jnp.broadcast_to(x[3:4], x.shape)
        outs.append(-jnp.abs(x - mid))
    return _cat(outs)


def _state_update(kw, v, e_row, s):
    c, kd = kw.shape
    vd = v.shape[1]
    pad = 2 * SUBLANES
    lhs = jnp.concatenate(_pieces3(e_row) + (jnp.zeros((pad - 3, kd), F32), kw), axis=0)
    sel = (lax.broadcasted_iota(jnp.int32, (pad, LANES), 0) < 3).astype(F32)
    rhs = jnp.concatenate(
        [jnp.concatenate([jnp.zeros((pad, vd), F32), sel], axis=1),
         jnp.concatenate([v, jnp.zeros((c, LANES), F32)], axis=1)], axis=0)
    upd = _dot_tn(lhs.astype(BF16), rhs.astype(BF16))
    e_col = upd[:, vd:]
    if vd > LANES:
        e_col = jnp.concatenate([e_col] * (vd // LANES), axis=1)
    return e_col * s + upd[:, :vd]


def _head_out(o, gn, gate_act):
    return (o * lax.rsqrt(jnp.mean(o * o, axis=-1, keepdims=True) + EPS) * gn * gate_act).astype(BF16)


def _ffn_kernel(*refs, nf, post):
    x_ref, g_ref, w1_ref, w3_ref, w2_ref = refs[:5]
    pg_ref = refs[5] if post else None
    outs = refs[6 if post else 5:-1]
    xn_ref = refs[-1]
    acc_ref = outs[0]
    f = pl.program_id(1)

    @pl.when(f == 0)
    def _():
        xn_ref[...] = _rms(x_ref[...], g_ref[...]).astype(BF16)
        acc_ref[...] = jnp.zeros_like(acc_ref)

    xn = xn_ref[...]
    h1 = _dot(xn, w1_ref[...])
    h3 = _dot(xn, w3_ref[...])
    acc_ref[...] += _dot((_silu(h1) * h3).astype(BF16), w2_ref[...])

    @pl.when(f == nf - 1)
    def _():
        y = x_ref[...] + 0.5 * acc_ref[...]
        if post == "final":
            outs[0][...] = _rms(y, pg_ref[...])
        else:
            outs[0][...] = y
            if post == "norm_out":
                outs[1][...] = _rms(y, pg_ref[...]).astype(BF16)


def _ffn(x, gain, w1, w3, w2, l, idx, post=None, post_gain=None):
    m = x.shape[0]
    nf = FF_DIM // TF
    tm = TM if post else TM_WIDE
    in_specs = [
        pl.BlockSpec((tm, D_MODEL), lambda i, f: (i, 0)),
        pl.BlockSpec((1, D_MODEL), lambda i, f: (0, 0)),
        pl.BlockSpec((None, None, D_MODEL, TF), lambda i, f: (l, idx, 0, f)),
        pl.BlockSpec((None, None, D_MODEL, TF), lambda i, f: (l, idx, 0, f)),
        pl.BlockSpec((None, None, TF, D_MODEL), lambda i, f: (l, idx, f, 0)),
    ]
    args = [x, gain.reshape(1, D_MODEL), w1, w3, w2]
    if post:
        in_specs.append(pl.BlockSpec((1, D_MODEL), lambda i, f: (0, 0)))
        args.append(post_gain.reshape(1, D_MODEL))
    row_spec = pl.BlockSpec((tm, D_MODEL), lambda i, f: (i, 0))
    out_specs, out_shape = [row_spec], [jax.ShapeDtypeStruct((m, D_MODEL), F32)]
    if post == "norm_out":
        out_specs.append(row_spec)
        out_shape.append(jax.ShapeDtypeStruct((m, D_MODEL), BF16))
    res = pl.pallas_call(
        functools.partial(_ffn_kernel, nf=nf, post=post),
        grid=(m // tm, nf),
        in_specs=in_specs,
        out_specs=out_specs,
        out_shape=out_shape,
        scratch_shapes=[pltpu.VMEM((tm, D_MODEL), BF16)],
        compiler_params=_cparams(("parallel", "arbitrary")),
        name="ffn_" + post if post else "ffn",
    )(*args)
    return res if post == "norm_out" else res[0]


def _permute_cols_kernel(w_ref, o_ref, *, pieces):
    x = w_ref[...]
    parts = [x[:, a:b] for a, b in pieces]
    width = sum(b - a for a, b in pieces)
    parts.append(jnp.zeros((x.shape[0], o_ref.shape[1] - width), x.dtype))
    o_ref[...] = jnp.concatenate(parts, axis=1).astype(o_ref.dtype)


def _permute_cols(w, j, pieces, n_out, rows_per_step=256):
    k = w.shape[1]
    return pl.pallas_call(
        functools.partial(_permute_cols_kernel, pieces=pieces),
        grid=(k // rows_per_step,),
        in_specs=[pl.BlockSpec((None, rows_per_step, w.shape[2]), lambda i: (j, i, 0))],
        out_specs=pl.BlockSpec((rows_per_step, n_out), lambda i: (i, 0)),
        out_shape=jax.ShapeDtypeStruct((k, n_out), BF16),
        compiler_params=_cparams(("parallel",)),
        name="permute_cols",
    )(w)


def _inproj_kernel(h_ref, w_ref, z_ref):
    res = _dot(h_ref[...], w_ref[...])
    for c in range(z_ref.shape[0]):
        z_ref[c] = res[:, c * LANES:(c + 1) * LANES]


def _inproj(h, w):
    m, n = h.shape[0], w.shape[1]
    tn = n // N_SPLIT
    return pl.pallas_call(
        _inproj_kernel,
        grid=(N_SPLIT, m // TM_IN),
        in_specs=[
            pl.BlockSpec((TM_IN, D_MODEL), lambda j, i: (i, 0)),
            pl.BlockSpec((D_MODEL, tn), lambda j, i: (0, j)),
        ],
        out_specs=pl.BlockSpec((tn // LANES, TM_IN, LANES), lambda j, i: (j, i, 0)),
        out_shape=jax.ShapeDtypeStruct((n // LANES, m, LANES), F32),
        compiler_params=_cparams(("parallel", "parallel")),
        name="inproj",
    )(h, w)


def _outproj_kernel(x_ref, y_ref, w_ref, o_ref):
    o_ref[...] = x_ref[...] + _dot(y_ref[...], w_ref[...])


def _outproj(x, y, w):
    m, kdim = y.shape
    return pl.pallas_call(
        _outproj_kernel,
        grid=(m // TM_OUT,),
        in_specs=[
            pl.BlockSpec((TM_OUT, D_MODEL), lambda i: (i, 0)),
            pl.BlockSpec((TM_OUT, kdim), lambda i: (i, 0)),
            pl.BlockSpec((kdim, D_MODEL), lambda i: (0, 0)),
        ],
        out_specs=pl.BlockSpec((TM_OUT, D_MODEL), lambda i: (i, 0)),
        out_shape=jax.ShapeDtypeStruct((m, D_MODEL), F32),
        compiler_params=_cparams(("parallel",)),
        name="outproj",
    )(x, y, w)


def _states(per_seq, nblk, ins, outs):
    if per_seq:
        def get(i):
            return tuple(r[i, 0] for r in ins)

        def put(i, vals):
            for o, val in zip(outs, vals):
                o[i, 0] = val
        return get, put

    @pl.when(pl.program_id(2) == 0)
    def _():
        for o, r in zip(outs, ins):
            o[...] = r[...]

    carried = {"v": tuple(o[0, 0] for o in outs)}

    def get(i):
        return carried["v"]

    def put(i, vals):
        carried["v"] = tuple(vals)
        if i == nblk - 1:
            for o, val in zip(outs, vals):
                o[0, 0] = val
    return get, put


def _ret_kernel(q_ref, k_ref, v_ref, gt_ref, cos_ref, sin_ref, gn_ref, s_in, *rest, blk, nblk, per_seq):
    y_ref, s_out = rest[-2:]
    rows = blk * nblk
    hf = jnp.full((1, 1), pl.program_id(1), jnp.int32).astype(F32)
    lg = jnp.log1p(-jnp.exp2(-5.0 - hf))
    cos = _cat([cos_ref[...]] * (rows // cos_ref.shape[0]))
    sin = _cat([sin_ref[...]] * (rows // sin_ref.shape[0]))
    q = q_ref[...]
    k = k_ref[...]
    q = q * cos + pltpu.roll(q, DK_A // 2, 1) * sin
    k = (k * cos + pltpu.roll(k, DK_A // 2, 1) * sin) * (DK_A ** -0.5)
    vb = _rd(v_ref).astype(BF16)
    mask, dist = _block_mask(rows, blk)
    dec = jnp.where(mask, jnp.exp(jnp.maximum(dist, 0).astype(F32) * lg), 0.0)
    a = _dot_nt(q.astype(BF16), k.astype(BF16)) * dec
    o = _dot(a.astype(BF16), vb)
    tloc = jnp.bitwise_and(lax.broadcasted_iota(jnp.int32, (rows, 1), 0), blk - 1).astype(F32)
    qt = (q * jnp.exp((tloc + 1.0) * lg)).astype(BF16)
    kw = (k * jnp.exp((blk - 1.0 - tloc) * lg)).astype(BF16)
    e_blk = jnp.exp(blk * lg)
    get, put = _states(per_seq, nblk, (s_in,), (s_out,))
    parts = []
    for i in range(nblk):
        r = slice(i * blk, (i + 1) * blk)
        s, = get(i)
        parts.append(o[r] + _dot(qt[r], s.astype(BF16)))
        put(i, (e_blk * s + _dot_tn(kw[r], vb[r]),))
    y_ref[...] = _head_out(_cat(parts), gn_ref[...], _silu(_rd(gt_ref)))


def _key_decay_slab(q, k, v, g, mk_ref, blk, nblk, get, put):
    rows = q.shape[0]
    tile = min(rows, PAIR_TILE)
    tiles = [slice(i * tile, (i + 1) * tile) for i in range(rows // tile)]
    g2 = g * LOG2E
    b = _prefix(g2, blk)
    b_last = _block_last(b, blk)
    qt = (q * jnp.exp2(b)).astype(BF16)
    kw = k * jnp.exp2(b_last - b)
    e_last = jnp.exp2(b_last)
    vb = v.astype(BF16)
    qb, kb = q.astype(BF16), k.astype(BF16)
    a = {(i, i): mk_ref[0] * _dot_nt(qb[t], kb[t]) for i, t in enumerate(tiles)}
    for li, m in enumerate(_levels(blk)):
        w = jnp.exp2(_level_exponent(b, g2, m))
        ql, kl = (q * w).astype(BF16), (k * w).astype(BF16)
        if m < tile:
            for i, t in enumerate(tiles):
                a[i, i] = a[i, i] + mk_ref[li + 1] * _dot_nt(ql[t], kl[t])
        else:
            for r0 in range(0, rows, 2 * m):
                for i in range((r0 + m) // tile, (r0 + 2 * m) // tile):
                    for j in range(r0 // tile, (r0 + m) // tile):
                        a[i, j] = _dot_nt(ql[tiles[i]], kl[tiles[j]])
    o_tiles = []
    for i in range(len(tiles)):
        terms = [_dot(a[i, j].astype(BF16), vb[tiles[j]]) for j in range(len(tiles)) if (i, j) in a]
        o_tiles.append(functools.reduce(lambda x, y: x + y, terms))
    o = _cat(o_tiles)
    parts = []
    for i in range(nblk):
        r = slice(i * blk, (i + 1) * blk)
        s, = get(i)
        parts.append(o[r] + _dot(qt[r], s.astype(BF16)))
        put(i, (_state_update(kw[r], v[r], e_last[i * blk:i * blk + 1], s),))
    return _cat(parts)


def _hgrn_kernel(q_ref, f_ref, i_ref, gt_ref, lb_ref, gn_ref, *rest, blk, nblk, per_seq, layer):
    mk_ref, s_in = rest[:2]
    y_ref, s_out = rest[-2:]
    lbl = lb_ref[...]
    e = jnp.exp(lbl - jnp.max(lbl, axis=0, keepdims=True))
    lb = jnp.sum(e[0:layer + 1], axis=0, keepdims=True) / jnp.sum(e, axis=0, keepdims=True)
    get, put = _states(per_seq, nblk, (s_in,), (s_out,))
    k = (1.0 - lb) * (1.0 - jax.nn.sigmoid(f_ref[...]))
    o = _key_decay_slab(_silu(q_ref[...]), k, i_ref[...], jnp.log(1.0 - k), mk_ref, blk, nblk, get, put)
    y_ref[...] = _head_out(o, gn_ref[...], _silu(gt_ref[...]))


def _gla_kernel(q_ref, k_ref, v_ref, gt_ref, tail_ref, w2_ref, ab_ref, gn_ref, *rest, blk, nblk, per_seq):
    mk_ref, s_in = rest[:2]
    y_ref, s_out = rest[-2:]
    wh, wm, wl = (p.astype(BF16) for p in _pieces3(w2_ref[...]))
    get, put = _states(per_seq, nblk, (s_in,), (s_out,))
    th, tm_, tl = (p.astype(BF16) for p in _pieces3(tail_ref[...]))
    x = (_dot(th, wh) + (_dot(th, wm) + _dot(tm_, wh))
         + (_dot(th, wl) + _dot(tm_, wm) + _dot(tl, wh)))
    g = _log_sigmoid(x + ab_ref[...]) / GLA_TAU
    o = _key_decay_slab(q_ref[...] * (DK_D ** -0.5), k_ref[...], _rd(v_ref), g, mk_ref, blk, nblk, get, put)
    y_ref[...] = _head_out(o, gn_ref[...], _silu(_rd(gt_ref)))


def _mlstm_kernel(q_ref, k_ref, v_ref, og_ref, tail_ref, wq_ref, wk_ref, bias_ref, gn_ref,
                  cq_in, ck_in, c_in, n_in, m_in, *rest, blk, nblk, per_seq):
    y_ref, c_out, n_out, m_out, cq_out, ck_out, uq_ref, uk_ref, hq_ref, hk_ref = rest[-10:]
    assert per_seq or nblk == 1
    rows = blk * nblk
    h = pl.program_id(1)
    hist = CONV_W - 1
    lo = SUBLANES - hist

    if not per_seq:
        @pl.when(pl.program_id(2) == 0)
        def _():
            for h_ref, cin in ((hq_ref, cq_in), (hk_ref, ck_in)):
                h_ref[...] = jnp.zeros_like(h_ref)
                h_ref[lo:SUBLANES, :] = cin[0]

    def conv(u_ref, h_ref, cin, cout, raw_ref, w_ref):
        parts = []
        for i in range(nblk):
            raw = _rd(raw_ref, slice(i * blk, (i + 1) * blk))
            if per_seq:
                u_ref[i, lo:SUBLANES, :] = cin[i]
            else:
                u_ref[i, 0:SUBLANES, :] = h_ref[...]
                h_ref[...] = raw[blk - SUBLANES:blk]
            u_ref[i, SUBLANES:SUBLANES + blk, :] = raw
            cout[i] = raw[blk - hist:blk]
            acc = u_ref[i, lo:lo + blk, :] * w_ref[0:1, :]
            for j in range(1, CONV_W):
                acc = acc + u_ref[i, lo + j:lo + j + blk, :] * w_ref[j:j + 1, :]
            parts.append(acc)
        return _silu(_cat(parts))

    q = conv(uq_ref, hq_ref, cq_in, cq_out, q_ref, wq_ref)
    k = conv(uk_ref, hk_ref, ck_in, ck_out, k_ref, wk_ref) * (DK_C ** -0.5)
    qb = q.astype(BF16)
    vb = _rd(v_ref).astype(BF16)

    lane = lax.broadcasted_iota(jnp.int32, (rows, LANES), 1)
    tl = tail_ref[...] + bias_ref[...]
    li = jnp.sum(jnp.where(lane == h, tl, 0.0), axis=-1, keepdims=True)
    lf = jnp.sum(jnp.where(lane == H_C + h, _log_sigmoid(tl), 0.0), axis=-1, keepdims=True)
    b = _prefix(jnp.broadcast_to(lf, (rows, LANES)), blk)[:, :1]

    get, put = _states(per_seq, nblk, (c_in, n_in, m_in), (c_out, n_out, m_out))
    states = [get(i) for i in range(nblk)] if per_seq else [get(0)]
    m_prev = _cat([jnp.broadcast_to(st[2][:, :1], (blk, 1)) for st in states])

    mask, _ = _block_mask(rows, blk)
    d = jnp.where(mask, b + _row_bcast(li - b), -jnp.inf)
    m_inter = b + m_prev
    m_t = jnp.maximum(m_inter, jnp.max(d, axis=-1, keepdims=True))
    w_inter = jnp.exp(m_inter - m_t)
    sc = _dot_nt(qb, k.astype(BF16)) * jnp.exp(d - m_t)
    num = _dot(sc.astype(BF16), vb)
    den = jnp.sum(sc, axis=-1, keepdims=True)
    m_new = _block_last(m_t, blk)
    b_last = _block_last(b, blk)
    kw = k * jnp.exp(b_last - b + li - m_new)
    w_c = jnp.exp(b_last + m_prev - m_new)
    nums, dens = [], []
    for i in range(nblk):
        r = slice(i * blk, (i + 1) * blk)
        cst, nrow, _ = states[i]
        nums.append(num[r] + w_inter[r] * _dot(qb[r], cst.astype(BF16)))
        dens.append(den[r] + w_inter[r] * jnp.sum(q[r] * nrow, axis=-1, keepdims=True))
        wc = w_c[i * blk:i * blk + 1]
        put(i, (wc * cst + _dot_tn(kw[r].astype(BF16), vb[r]),
                wc * nrow + jnp.sum(kw[r], axis=0, keepdims=True),
                jnp.broadcast_to(m_new[i * blk:i * blk + 1], (1, LANES))))
    hh = _cat(nums) / jnp.maximum(jnp.abs(_cat(dens)), jnp.exp(-m_t))
    y_ref[...] = _head_out(hh, gn_ref[...], jax.nn.sigmoid(_rd(og_ref)))


def _geometry(n_seq, seq_len, row0, kind):
    slab, chunk = PROMPT_SLAB[kind]
    if seq_len >= slab:
        blk, rows, per_seq, nseq_blk = chunk, slab, False, 1
        grid_seq, grid_slab = n_seq, seq_len // slab
    else:
        blk, nseq_blk, per_seq = seq_len, SEQ_PER_STEP, True
        rows = nseq_blk * blk
        grid_seq, grid_slab = n_seq // nseq_blk, 1
    base = row0 // rows
    return dict(blk=blk, rows=rows, nblk=rows // blk, per_seq=per_seq, nseq_blk=nseq_blk,
                grid_seq=grid_seq, grid_slab=grid_slab, rowblk=lambda b, s: base + b * grid_slab + s)


def _zspec(geo, width, col):
    rb = geo["rowblk"]
    tiles = width // LANES
    return pl.BlockSpec((None if tiles == 1 else tiles, geo["rows"], LANES), lambda b, h, s: (col + h, rb(b, s), 0))


def _rd(ref, rows=slice(None)):
    if len(ref.shape) == 2:
        return ref[rows, :]
    return jnp.concatenate([ref[c, rows, :] for c in range(ref.shape[0])], axis=1)


def _const_spec(shape):
    return pl.BlockSpec(shape, lambda b, h, s: (0,) * len(shape))


def _head_spec(width, col=0):
    return pl.BlockSpec((1, width), lambda b, h, s: (0, col + h))


def _state_spec(geo, shape):
    return pl.BlockSpec((geo["nseq_blk"], 1) + shape, lambda b, h, s: (b, h, 0, 0))


def _mixer_call(kern, geo, heads, vd, y_col, y_prev, n_tok, in_specs, args, extra_specs, extra_shapes,
                scratch, name):
    if y_prev is not None:
        in_specs = in_specs + [pl.BlockSpec(memory_space=pl.ANY)]
        args = args + [y_prev]
    rb = geo["rowblk"]
    return pl.pallas_call(
        kern,
        grid=(geo["grid_seq"], heads, geo["grid_slab"]),
        in_specs=in_specs,
        out_specs=[pl.BlockSpec((geo["rows"], vd), lambda b, h, s: (rb(b, s), y_col // vd + h))] + extra_specs,
        out_shape=[jax.ShapeDtypeStruct((n_tok, MIX), BF16)] + extra_shapes,
        input_output_aliases={} if y_prev is None else {len(args) - 1: 0},
        scratch_shapes=scratch,
        compiler_params=_cparams(("parallel", "parallel", "arbitrary")),
        name=name + ("_sample" if geo["per_seq"] else "_prompt"),
    )(*args)


def _decay_mixer(kind, z, geo, state, extra, y_prev, y_col):
    heads, kd, vd = {"ret": (H_A, DK_A, DV_A), "hgrn": (H_B, DK_B, DV_B), "gla": (H_D, DK_D, DV_D)}[kind]
    common = dict(blk=geo["blk"], nblk=geo["nblk"], per_seq=geo["per_seq"])
    kb, vb = LANES, vd
    if kind == "ret":
        cos, sin = extra["rope"]
        tab = pl.BlockSpec((geo["blk"] if geo["per_seq"] else geo["rows"], DK_A),
                           lambda b, h, s: (0 if geo["per_seq"] else s, 0))
        in_specs = [_zspec(geo, kb, 0), _zspec(geo, kb, H_A), _zspec(geo, vb, (2 * H_A * DK_A) // vb),
                    _zspec(geo, vb, (2 * H_A * DK_A) // vb + H_A), tab, tab, _head_spec(vd)]
        args = [z, z, z, z, cos, sin, extra["gn"].reshape(1, -1)]
        kern = functools.partial(_ret_kernel, **common)
    elif kind == "hgrn":
        off = (2 * H_A * DK_A + 2 * H_A * DV_A) // LANES
        in_specs = [_zspec(geo, kb, off), _zspec(geo, kb, off + H_B), _zspec(geo, kb, off + 2 * H_B),
                    _zspec(geo, kb, off + 3 * H_B),
                    pl.BlockSpec((DEPTH + 1, DK_B), lambda b, h, s: (0, h)), _head_spec(vd)]
        args = [z, z, z, z, extra["lb_logits"], extra["gn"].reshape(1, -1)]
        kern = functools.partial(_hgrn_kernel, layer=extra["layer"], **common)
    else:
        off = (2 * H_C * DK_C + 2 * H_C * DV_C) // LANES
        voff = (2 * H_C * DK_C + 2 * H_C * DV_C + 2 * H_D * DK_D) // vb
        in_specs = [_zspec(geo, kb, off), _zspec(geo, kb, off + H_D), _zspec(geo, vb, voff),
                    _zspec(geo, vb, voff + H_D),
                    pl.BlockSpec((None, geo["rows"], CD_TAIL), lambda b, h, s: (CD_MAIN // CD_TAIL, geo["rowblk"](b, s), 0)),
                    pl.BlockSpec((CD_TAIL, DK_D), lambda b, h, s: (0, h)),
                    _head_spec(DK_D), _head_spec(vd)]
        args = [z, z, z, z, z, extra["w2"], extra["ab"].reshape(1, -1), extra["gn"].reshape(1, -1)]
        kern = functools.partial(_gla_kernel, **common)
    if kind != "ret":
        tile = min(geo["rows"], PAIR_TILE)
        mk = _pair_masks(tile, min(geo["blk"], tile))
        in_specs.append(_const_spec(mk.shape))
        args.append(mk)
    in_specs.append(_state_spec(geo, (kd, vd)))
    args.append(state)
    return _mixer_call(kern, geo, heads, vd, y_col, y_prev, z.shape[1], in_specs, args,
                       [_state_spec(geo, (kd, vd))], [jax.ShapeDtypeStruct(state.shape, F32)], [], kind)


def _mlstm_mixer(z, geo, c0, n0, m0, conv0, w, y_prev, y_col):
    n_seq = c0.shape[0]
    blk, nb = geo["blk"], geo["nseq_blk"]
    bias = jnp.zeros((1, LANES), F32).at[0, :H_C].set(w["i_bias"]).at[0, H_C:2 * H_C].set(w["f_bias"])
    kk = (H_C * DK_C) // DK_C
    kv = (2 * H_C * DK_C) // DV_C
    conv_spec = lambda col: pl.BlockSpec((nb, CONV_W - 1, DK_C), lambda b, h, s: (b, 0, col + h))
    in_specs = [
        _zspec(geo, DK_C, 0), _zspec(geo, DK_C, kk), _zspec(geo, DV_C, kv), _zspec(geo, DV_C, kv + H_C),
        pl.BlockSpec((None, geo["rows"], CD_TAIL), lambda b, h, s: (CD_MAIN // CD_TAIL, geo["rowblk"](b, s), 0)),
        pl.BlockSpec((CONV_W, DK_C), lambda b, h, s: (0, h)),
        pl.BlockSpec((CONV_W, DK_C), lambda b, h, s: (0, H_C + h)),
        _const_spec((1, LANES)), _head_spec(DV_C),
        conv_spec(0), conv_spec(H_C),
        _state_spec(geo, (DK_C, DV_C)), _state_spec(geo, (1, DK_C)), _state_spec(geo, (1, LANES)),
    ]
    n4 = n0.reshape(n_seq, H_C, 1, DK_C)
    m4 = jnp.broadcast_to(m0[:, :, None, None], (n_seq, H_C, 1, LANES))
    args = [z, z, z, z, z, w["conv_w"], w["conv_w"], bias, w["gn_c"].reshape(1, -1),
            conv0, conv0, c0, n4, m4]
    conv_shape = jax.ShapeDtypeStruct((n_seq, CONV_W - 1, H_C * DK_C), F32)
    y, c_new, n_new, m_new, cq, ck = _mixer_call(
        functools.partial(_mlstm_kernel, blk=blk, nblk=geo["nblk"], per_seq=geo["per_seq"]),
        geo, H_C, DV_C, y_col, y_prev, z.shape[1], in_specs, args,
        [_state_spec(geo, (DK_C, DV_C)), _state_spec(geo, (1, DK_C)), _state_spec(geo, (1, LANES)),
         conv_spec(0), conv_spec(0)],
        [jax.ShapeDtypeStruct(c0.shape, F32), jax.ShapeDtypeStruct(n4.shape, F32),
         jax.ShapeDtypeStruct(m4.shape, F32), conv_shape, conv_shape],
        [pltpu.VMEM((geo["nblk"], SUBLANES + blk, DK_C), F32), pltpu.VMEM((geo["nblk"], SUBLANES + blk, DK_C), F32),
         pltpu.VMEM((SUBLANES, DK_C), F32), pltpu.VMEM((SUBLANES, DK_C), F32)],
        "mlstm")
    return (y, c_new, n_new.reshape(n_seq, H_C, DK_C), m_new[:, :, 0, 0],
            jnp.concatenate([cq, ck], axis=-1))


def _rope_tables(pos0, n):
    half = DK_A // 2
    inv = ROPE_BASE ** (-jnp.arange(half, dtype=F32) / half)
    ang = (jnp.arange(n, dtype=F32) + float(pos0))[:, None] * inv[None, :]
    cos, sin = jnp.cos(ang), jnp.sin(ang)
    return jnp.concatenate([cos, cos], axis=1), jnp.concatenate([-sin, sin], axis=1)


def kernel(x_prompt, x_sample, state_ret, state_hgrn, state_mlstm_c, state_mlstm_n, state_mlstm_m,
           state_mlstm_conv, state_gla, norm_gain, ffn_w1, ffn_w3, ffn_w2, ab_w_in, ab_w_out, ab_norm_a,
           ab_norm_b, lb_logits, cd_w_in, cd_w_out, cd_conv_w, cd_i_bias, cd_f_bias, cd_norm_c,
           cd_alpha_w2, cd_alpha_b, cd_norm_d, final_norm):
    bp, tp, _ = x_prompt.shape
    bs, ts, _ = x_sample.shape
    n_p, n_s = bp * tp, bs * ts
    x = jnp.concatenate([x_prompt.reshape(n_p, D_MODEL), x_sample.reshape(n_s, D_MODEL)], axis=0)
    w1, w3, w2 = ffn_w1.astype(BF16), ffn_w3.astype(BF16), ffn_w2.astype(BF16)
    groups = (("p", bp, {k: _geometry(bp, tp, 0, k) for k in PROMPT_SLAB}),
              ("s", bs, {k: _geometry(bs, ts, n_p, k) for k in PROMPT_SLAB}))
    rope = {"p": _rope_tables(0, tp), "s": _rope_tables(PAST_LEN, ts)}
    zeros = lambda *shape: jnp.zeros(shape, F32)
    outs = {}

    def keep(name, tag, val):
        outs.setdefault(name + "_" + tag, []).append(val)

    for l in range(DEPTH):
        j = l // 2
        x, hn = _ffn(x, norm_gain[l, 0], w1, w3, w2, l, 0, post="norm_out", post_gain=norm_gain[l, 1])
        y = None
        if l % 2 == 0:
            z = _inproj(hn, ab_w_in[j].astype(BF16))
            for tag, nseq, geo in groups:
                s_ret = zeros(nseq, H_A, DK_A, DV_A) if tag == "p" else state_ret[j]
                s_hg = zeros(nseq, H_B, DK_B, DV_B) if tag == "p" else state_hgrn[j]
                y, r_new = _decay_mixer("ret", z, geo["ret"], s_ret, dict(rope=rope[tag], gn=ab_norm_a[j]), y, 0)
                y, g_new = _decay_mixer("hgrn", z, geo["hgrn"], s_hg,
                                        dict(lb_logits=lb_logits, gn=ab_norm_b[j], layer=l), y, H_A * DV_A)
                keep("ret", tag, r_new)
                keep("hgrn", tag, g_new)
            x = _outproj(x, y, ab_w_out[j].astype(BF16))
        else:
            gates = 2 * H_C
            main = 2 * H_C * DK_C + 2 * H_C * DV_C
            rest = CD_MAIN - main
            n_pad = -(-(CD_MAIN + CD_TAIL) // (N_SPLIT * LANES)) * (N_SPLIT * LANES)
            lane_pad = -cd_w_in.shape[2] % LANES
            w_perm = _permute_cols(jnp.pad(cd_w_in.astype(BF16), ((0, 0), (0, 0), (0, lane_pad))), j, ((0, main), (main + gates, main + gates + rest),
                                                (main, main + gates), (main + gates + rest, cd_w_in.shape[2])), n_pad)
            z = _inproj(hn, w_perm)
            w2p = jnp.zeros((CD_TAIL, H_D * DK_D), F32).at[gates:gates + GLA_RANK].set(cd_alpha_w2[j])
            wts = dict(conv_w=cd_conv_w[j], i_bias=cd_i_bias[j], f_bias=cd_f_bias[j], gn_c=cd_norm_c[j])
            for tag, nseq, geo in groups:
                if tag == "p":
                    c0, n0, m0 = zeros(nseq, H_C, DK_C, DV_C), zeros(nseq, H_C, DK_C), zeros(nseq, H_C)
                    cv0, s_gl = zeros(nseq, CONV_W - 1, 2 * H_C * DK_C), zeros(nseq, H_D, DK_D, DV_D)
                else:
                    c0, n0, m0 = state_mlstm_c[j], state_mlstm_n[j], state_mlstm_m[j]
                    cv0, s_gl = state_mlstm_conv[j], state_gla[j]
                y, c_new, n_new, m_new, cv_new = _mlstm_mixer(z, geo["mlstm"], c0, n0, m0, cv0, wts, y, 0)
                y, gl_new = _decay_mixer("gla", z, geo["gla"], s_gl,
                                         dict(w2=w2p, ab=cd_alpha_b[j], gn=cd_norm_d[j]), y, H_C * DV_C)
                for name, val in (("c", c_new), ("n", n_new), ("m", m_new), ("conv", cv_new), ("gla", gl_new)):
                    keep(name, tag, val)
            x = _outproj(x, y, cd_w_out[j].astype(BF16))
        last = l == DEPTH - 1
        x = _ffn(x, norm_gain[l, 2], w1, w3, w2, l, 1, post="final" if last else None,
                 post_gain=final_norm if last else None)

    y_p = x[:n_p].reshape(bp, tp, D_MODEL)
    y_s = x[n_p:].reshape(bs, ts, D_MODEL)
    st = lambda name: jnp.stack(outs[name])
    return (y_p, y_s, st("ret_p"), st("ret_s"), st("hgrn_p"), st("hgrn_s"), st("c_p"), st("c_s"),
            st("n_p"), st("n_s"), st("m_p"), st("m_s"), st("conv_p"), st("conv_s"), st("gla_p"), st("gla_s"))
```

```python
import functools

import numpy as np
import jax
import jax.numpy as jnp
from jax import lax
from jax.experimental import pallas as pl
from jax.experimental.pallas import tpu as pltpu

F32 = jnp.float32
BF16 = jnp.bfloat16

D_MODEL = 2048
DEPTH = 2
PAST_LEN = 16384
H_A, DK_A, DV_A = 4, 128, 256
ROPE_BASE = 10000.0
H_B, DK_B, DV_B = 8, 128, 128
H_C, DK_C, DV_C = 4, 256, 256
CONV_W = 4
H_D, DK_D, DV_D = 4, 128, 256
GLA_RANK = 16
GLA_TAU = 16.0
FF_DIM = 5632
EPS = 1e-6

CD_MAIN = 2 * H_C * DK_C + 2 * H_C * DV_C + 2 * H_D * DK_D + 2 * H_D * DV_D
CD_TAIL = 128
MIX = H_A * DV_A + H_B * DV_B

LANES = 128
SUBLANES = 8
VMEM_LIMIT = 56 * 1024 * 1024

TM = 768
TM_WIDE = 1024
TM_IN = 1152
TM_OUT = 512
TF = 512
N_SPLIT = 4
PROMPT_SLAB = {"ret": (1024, 512), "hgrn": (2048, 256), "mlstm": (512, 512), "gla": (2048, 256)}
SEQ_PER_STEP = 32
PAIR_TILE = 128
LOG2E = 1.4426950408889634


def _cparams(sem):
    return pltpu.CompilerParams(dimension_semantics=sem, vmem_limit_bytes=VMEM_LIMIT)


def _rms(x, g):
    return x * lax.rsqrt(jnp.mean(x * x, axis=-1, keepdims=True) + EPS) * g


def _silu(x):
    return x * jax.nn.sigmoid(x)


def _log_sigmoid(x):
    return jnp.minimum(x, 0.0) - jnp.log1p(jnp.exp(-jnp.abs(x)))


def _pieces3(x):
    hi = x.astype(BF16).astype(F32)
    r1 = x - hi
    mid = r1.astype(BF16).astype(F32)
    lo = (r1 - mid).astype(BF16).astype(F32)
    return hi, mid, lo


def _dot(a, b):
    return jnp.dot(a, b, preferred_element_type=F32)


def _dot_nt(a, b):
    return lax.dot_general(a, b, (((1,), (1,)), ((), ())), preferred_element_type=F32)


def _dot_tn(a, b):
    return lax.dot_general(a, b, (((0,), (0,)), ((), ())), preferred_element_type=F32)


def _cat(parts, axis=0):
    return parts[0] if len(parts) == 1 else jnp.concatenate(parts, axis=axis)


def _prefix(g, blk):
    rows, n = g.shape
    row = lax.broadcasted_iota(jnp.int32, (SUBLANES, n), 0)
    outs = []
    off = None
    for r0 in range(0, rows, SUBLANES):
        x = g[r0:r0 + SUBLANES]
        for sh in (1, 2, 4):
            x = x + jnp.where(row >= sh, pltpu.roll(x, sh, 0), 0.0)
        if r0 % blk:
            x = x + off
        off = x[SUBLANES - 1:SUBLANES]
        outs.append(x)
    return _cat(outs)


def _block_last(x, blk):
    rows, n = x.shape
    return _cat([jnp.broadcast_to(x[r0 + blk - 1:r0 + blk], (blk, n)) for r0 in range(0, rows, blk)])


def _block_mask(rows, blk):
    tt = lax.broadcasted_iota(jnp.int32, (rows, rows), 0)
    ss = lax.broadcasted_iota(jnp.int32, (rows, rows), 1)
    mask = ss <= tt
    if blk < rows:
        sh = blk.bit_length() - 1
        mask = jnp.logical_and(mask, lax.shift_right_logical(tt, sh) == lax.shift_right_logical(ss, sh))
    return mask, tt - ss


def _row_bcast(col):
    c = col.shape[0]
    lane = lax.broadcasted_iota(jnp.int32, (c, LANES), 1)
    pieces = _pieces3(jnp.where(lane == 0, jnp.broadcast_to(col, (c, LANES)), 0.0))
    rm = jnp.concatenate(pieces, axis=1).astype(BF16)
    ones = jnp.ones((c, 3 * LANES), BF16)
    return _dot_nt(ones, rm)


def _levels(blk):
    out = []
    m = 1
    while m < blk:
        out.append(m)
        m *= 2
    return tuple(out)


def _pair_masks(rows, blk):
    t = np.arange(rows)[:, None]
    u = np.arange(rows)[None, :]
    masks = [t == u] + [((t % (2 * m)) >= m) & ((u % (2 * m)) < m) & (t // (2 * m) == u // (2 * m))
                        for m in _levels(blk)]
    return jnp.asarray(np.stack(masks).astype(np.float32))


def _level_exponent(b, g, m):
    rows, n = b.shape
    if m >= SUBLANES:
        mid = _cat([jnp.broadcast_to(b[r0 + m - 1:r0 + m], (2 * m, n)) for r0 in range(0, rows, 2 * m)])
        return -jnp.abs(b - mid)
    row = lax.broadcasted_iota(jnp.int32, (SUBLANES, n), 0)
    outs = []
    for r0 in range(0, rows, SUBLANES):
        x = b[r0:r0 + SUBLANES]
        if m == 1:
            outs.append(jnp.where(jnp.bitwise_and(row, 1) == 1, g[r0:r0 + SUBLANES], 0.0))
            continue
        if m == 2:
            mid = jnp.where(row < 4, jnp.broadcast_to(x[1:2], x.shape), jnp.broadcast_to(x[5:6], x.shape))
        else:
            mid = jnp.broadcast_to(x[3:4], x.shape)
        outs.append(-jnp.abs(x - mid))
    return _cat(outs)


def _state_update(kw, v, e_row, s):
    c, kd = kw.shape
    vd = v.shape[1]
    pad = 2 * SUBLANES
    lhs = jnp.concatenate(_pieces3(e_row) + (jnp.zeros((pad - 3, kd), F32), kw), axis=0)
    sel = (lax.broadcasted_iota(jnp.int32, (pad, LANES), 0) < 3).astype(F32)
    rhs = jnp.concatenate(
        [jnp.concatenate([jnp.zeros((pad, vd), F32), sel], axis=1),
         jnp.concatenate([v, jnp.zeros((c, LANES), F32)], axis=1)], axis=0)
    upd = _dot_tn(lhs.astype(BF16), rhs.astype(BF16))
    e_col = upd[:, vd:]
    if vd > LANES:
        e_col = jnp.concatenate([e_col] * (vd // LANES), axis=1)
    return e_col * s + upd[:, :vd]


def _head_out(o, gn, gate_act):
    return (o * lax.rsqrt(jnp.mean(o * o, axis=-1, keepdims=True) + EPS) * gn * gate_act).astype(BF16)


def _ffn_kernel(*refs, nf, post):
    x_ref, g_ref, w1_ref, w3_ref, w2_ref = refs[:5]
    pg_ref = refs[5] if post else None
    outs = refs[6 if post else 5:-1]
    xn_ref = refs[-1]
    acc_ref = outs[0]
    f = pl.program_id(1)

    @pl.when(f == 0)
    def _():
        xn_ref[...] = _rms(x_ref[...], g_ref[...]).astype(BF16)
        acc_ref[...] = jnp.zeros_like(acc_ref)

    xn = xn_ref[...]
    h1 = _dot(xn, w1_ref[...])
    h3 = _dot(xn, w3_ref[...])
    acc_ref[...] += _dot((_silu(h1) * h3).astype(BF16), w2_ref[...])

    @pl.when(f == nf - 1)
    def _():
        y = x_ref[...] + 0.5 * acc_ref[...]
        if post == "final":
            outs[0][...] = _rms(y, pg_ref[...])
        else:
            outs[0][...] = y
            if post == "norm_out":
                outs[1][...] = _rms(y, pg_ref[...]).astype(BF16)


def _ffn(x, gain, w1, w3, w2, l, idx, post=None, post_gain=None):
    m = x.shape[0]
    nf = FF_DIM // TF
    tm = TM if post else TM_WIDE
    in_specs = [
        pl.BlockSpec((tm, D_MODEL), lambda i, f: (i, 0)),
        pl.BlockSpec((1, D_MODEL), lambda i, f: (0, 0)),
        pl.BlockSpec((None, None, D_MODEL, TF), lambda i, f: (l, idx, 0, f)),
        pl.BlockSpec((None, None, D_MODEL, TF), lambda i, f: (l, idx, 0, f)),
        pl.BlockSpec((None, None, TF, D_MODEL), lambda i, f: (l, idx, f, 0)),
    ]
    args = [x, gain.reshape(1, D_MODEL), w1, w3, w2]
    if post:
        in_specs.append(pl.BlockSpec((1, D_MODEL), lambda i, f: (0, 0)))
        args.append(post_gain.reshape(1, D_MODEL))
    row_spec = pl.BlockSpec((tm, D_MODEL), lambda i, f: (i, 0))
    out_specs, out_shape = [row_spec], [jax.ShapeDtypeStruct((m, D_MODEL), F32)]
    if post == "norm_out":
        out_specs.append(row_spec)
        out_shape.append(jax.ShapeDtypeStruct((m, D_MODEL), BF16))
    res = pl.pallas_call(
        functools.partial(_ffn_kernel, nf=nf, post=post),
        grid=(m // tm, nf),
        in_specs=in_specs,
        out_specs=out_specs,
        out_shape=out_shape,
        scratch_shapes=[pltpu.VMEM((tm, D_MODEL), BF16)],
        compiler_params=_cparams(("parallel", "arbitrary")),
        name="ffn_" + post if post else "ffn",
    )(*args)
    return res if post == "norm_out" else res[0]


def _permute_cols_kernel(w_ref, o_ref, *, pieces):
    x = w_ref[...]
    parts = [x[:, a:b] for a, b in pieces]
    width = sum(b - a for a, b in pieces)
    parts.append(jnp.zeros((x.shape[0], o_ref.shape[1] - width), x.dtype))
    o_ref[...] = jnp.concatenate(parts, axis=1).astype(o_ref.dtype)


def _permute_cols(w, j, pieces, n_out, rows_per_step=256):
    k = w.shape[1]
    return pl.pallas_call(
        functools.partial(_permute_cols_kernel, pieces=pieces),
        grid=(k // rows_per_step,),
        in_specs=[pl.BlockSpec((None, rows_per_step, w.shape[2]), lambda i: (j, i, 0))],
        out_specs=pl.BlockSpec((rows_per_step, n_out), lambda i: (i, 0)),
        out_shape=jax.ShapeDtypeStruct((k, n_out), BF16),
        compiler_params=_cparams(("parallel",)),
        name="permute_cols",
    )(w)


def _inproj_kernel(h_ref, w_ref, z_ref):
    res = _dot(h_ref[...], w_ref[...])
    for c in range(z_ref.shape[0]):
        z_ref[c] = res[:, c * LANES:(c + 1) * LANES]


def _inproj(h, w):
    m, n = h.shape[0], w.shape[1]
    tn = n // N_SPLIT
    return pl.pallas_call(
        _inproj_kernel,
        grid=(N_SPLIT, m // TM_IN),
        in_specs=[
            pl.BlockSpec((TM_IN, D_MODEL), lambda j, i: (i, 0)),
            pl.BlockSpec((D_MODEL, tn), lambda j, i: (0, j)),
        ],
        out_specs=pl.BlockSpec((tn // LANES, TM_IN, LANES), lambda j, i: (j, i, 0)),
        out_shape=jax.ShapeDtypeStruct((n // LANES, m, LANES), F32),
        compiler_params=_cparams(("parallel", "parallel")),
        name="inproj",
    )(h, w)


def _outproj_kernel(x_ref, y_ref, w_ref, o_ref):
    o_ref[...] = x_ref[...] + _dot(y_ref[...], w_ref[...])


def _outproj(x, y, w):
    m, kdim = y.shape
    return pl.pallas_call(
        _outproj_kernel,
        grid=(m // TM_OUT,),
        in_specs=[
            pl.BlockSpec((TM_OUT, D_MODEL), lambda i: (i, 0)),
            pl.BlockSpec((TM_OUT, kdim), lambda i: (i, 0)),
            pl.BlockSpec((kdim, D_MODEL), lambda i: (0, 0)),
        ],
        out_specs=pl.BlockSpec((TM_OUT, D_MODEL), lambda i: (i, 0)),
        out_shape=jax.ShapeDtypeStruct((m, D_MODEL), F32),
        compiler_params=_cparams(("parallel",)),
        name="outproj",
    )(x, y, w)


def _states(per_seq, nblk, ins, outs):
    if per_seq:
        def get(i):
            return tuple(r[i, 0] for r in ins)

        def put(i, vals):
            for o, val in zip(outs, vals):
                o[i, 0] = val
        return get, put

    @pl.when(pl.program_id(2) == 0)
    def _():
        for o, r in zip(outs, ins):
            o[...] = r[...]

    carried = {"v": tuple(o[0, 0] for o in outs)}

    def get(i):
        return carried["v"]

    def put(i, vals):
        carried["v"] = tuple(vals)
        if i == nblk - 1:
            for o, val in zip(outs, vals):
                o[0, 0] = val
    return get, put


def _ret_kernel(q_ref, k_ref, v_ref, gt_ref, cos_ref, sin_ref, gn_ref, s_in, *rest, blk, nblk, per_seq):
    y_ref, s_out = rest[-2:]
    rows = blk * nblk
    hf = jnp.full((1, 1), pl.program_id(1), jnp.int32).astype(F32)
    lg = jnp.log1p(-jnp.exp2(-5.0 - hf))
    cos = _cat([cos_ref[...]] * (rows // cos_ref.shape[0]))
    sin = _cat([sin_ref[...]] * (rows // sin_ref.shape[0]))
    q = q_ref[...]
    k = k_ref[...]
    q = q * cos + pltpu.roll(q, DK_A // 2, 1) * sin
    k = (k * cos + pltpu.roll(k, DK_A // 2, 1) * sin) * (DK_A ** -0.5)
    vb = _rd(v_ref).astype(BF16)
    mask, dist = _block_mask(rows, blk)
    dec = jnp.where(mask, jnp.exp(jnp.maximum(dist, 0).astype(F32) * lg), 0.0)
    a = _dot_nt(q.astype(BF16), k.astype(BF16)) * dec
    o = _dot(a.astype(BF16), vb)
    tloc = jnp.bitwise_and(lax.broadcasted_iota(jnp.int32, (rows, 1), 0), blk - 1).astype(F32)
    qt = (q * jnp.exp((tloc + 1.0) * lg)).astype(BF16)
    kw = (k * jnp.exp((blk - 1.0 - tloc) * lg)).astype(BF16)
    e_blk = jnp.exp(blk * lg)
    get, put = _states(per_seq, nblk, (s_in,), (s_out,))
    parts = []
    for i in range(nblk):
        r = slice(i * blk, (i + 1) * blk)
        s, = get(i)
        parts.append(o[r] + _dot(qt[r], s.astype(BF16)))
        put(i, (e_blk * s + _dot_tn(kw[r], vb[r]),))
    y_ref[...] = _head_out(_cat(parts), gn_ref[...], _silu(_rd(gt_ref)))


def _key_decay_slab(q, k, v, g, mk_ref, blk, nblk, get, put):
    rows = q.shape[0]
    tile = min(rows, PAIR_TILE)
    tiles = [slice(i * tile, (i + 1) * tile) for i in range(rows // tile)]
    g2 = g * LOG2E
    b = _prefix(g2, blk)
    b_last = _block_last(b, blk)
    qt = (q * jnp.exp2(b)).astype(BF16)
    kw = k * jnp.exp2(b_last - b)
    e_last = jnp.exp2(b_last)
    vb = v.astype(BF16)
    qb, kb = q.astype(BF16), k.astype(BF16)
    a = {(i, i): mk_ref[0] * _dot_nt(qb[t], kb[t]) for i, t in enumerate(tiles)}
    for li, m in enumerate(_levels(blk)):
        w = jnp.exp2(_level_exponent(b, g2, m))
        ql, kl = (q * w).astype(BF16), (k * w).astype(BF16)
        if m < tile:
            for i, t in enumerate(tiles):
                a[i, i] = a[i, i] + mk_ref[li + 1] * _dot_nt(ql[t], kl[t])
        else:
            for r0 in range(0, rows, 2 * m):
                for i in range((r0 + m) // tile, (r0 + 2 * m) // tile):
                    for j in range(r0 // tile, (r0 + m) // tile):
                        a[i, j] = _dot_nt(ql[tiles[i]], kl[tiles[j]])
    o_tiles = []
    for i in range(len(tiles)):
        terms = [_dot(a[i, j].astype(BF16), vb[tiles[j]]) for j in range(len(tiles)) if (i, j) in a]
        o_tiles.append(functools.reduce(lambda x, y: x + y, terms))
    o = _cat(o_tiles)
    parts = []
    for i in range(nblk):
        r = slice(i * blk, (i + 1) * blk)
        s, = get(i)
        parts.append(o[r] + _dot(qt[r], s.astype(BF16)))
        put(i, (_state_update(kw[r], v[r], e_last[i * blk:i * blk + 1], s),))
    return _cat(parts)


def _hgrn_kernel(q_ref, f_ref, i_ref, gt_ref, lb_ref, gn_ref, *rest, blk, nblk, per_seq, layer):
    mk_ref, s_in = rest[:2]
    y_ref, s_out = rest[-2:]
    lbl = lb_ref[...]
    e = jnp.exp(lbl - jnp.max(lbl, axis=0, keepdims=True))
    lb = jnp.sum(e[0:layer + 1], axis=0, keepdims=True) / jnp.sum(e, axis=0, keepdims=True)
    get, put = _states(per_seq, nblk, (s_in,), (s_out,))
    k = (1.0 - lb) * (1.0 - jax.nn.sigmoid(f_ref[...]))
    o = _key_decay_slab(_silu(q_ref[...]), k, i_ref[...], jnp.log(1.0 - k), mk_ref, blk, nblk, get, put)
    y_ref[...] = _head_out(o, gn_ref[...], _silu(gt_ref[...]))


def _gla_kernel(q_ref, k_ref, v_ref, gt_ref, tail_ref, w2_ref, ab_ref, gn_ref, *rest, blk, nblk, per_seq):
    mk_ref, s_in = rest[:2]
    y_ref, s_out = rest[-2:]
    wh, wm, wl = (p.astype(BF16) for p in _pieces3(w2_ref[...]))
    get, put = _states(per_seq, nblk, (s_in,), (s_out,))
    th, tm_, tl = (p.astype(BF16) for p in _pieces3(tail_ref[...]))
    x = (_dot(th, wh) + (_dot(th, wm) + _dot(tm_, wh))
         + (_dot(th, wl) + _dot(tm_, wm) + _dot(tl, wh)))
    g = _log_sigmoid(x + ab_ref[...]) / GLA_TAU
    o = _key_decay_slab(q_ref[...] * (DK_D ** -0.5), k_ref[...], _rd(v_ref), g, mk_ref, blk, nblk, get, put)
    y_ref[...] = _head_out(o, gn_ref[...], _silu(_rd(gt_ref)))


def _mlstm_kernel(q_ref, k_ref, v_ref, og_ref, tail_ref, wq_ref, wk_ref, bias_ref, gn_ref,
                  cq_in, ck_in, c_in, n_in, m_in, *rest, blk, nblk, per_seq):
    y_ref, c_out, n_out, m_out, cq_out, ck_out, uq_ref, uk_ref, hq_ref, hk_ref = rest[-10:]
    assert per_seq or nblk == 1
    rows = blk * nblk
    h = pl.program_id(1)
    hist = CONV_W - 1
    lo = SUBLANES - hist

    if not per_seq:
        @pl.when(pl.program_id(2) == 0)
        def _():
            for h_ref, cin in ((hq_ref, cq_in), (hk_ref, ck_in)):
                h_ref[...] = jnp.zeros_like(h_ref)
                h_ref[lo:SUBLANES, :] = cin[0]

    def conv(u_ref, h_ref, cin, cout, raw_ref, w_ref):
        parts = []
        for i in range(nblk):
            raw = _rd(raw_ref, slice(i * blk, (i + 1) * blk))
            if per_seq:
                u_ref[i, lo:SUBLANES, :] = cin[i]
            else:
                u_ref[i, 0:SUBLANES, :] = h_ref[...]
                h_ref[...] = raw[blk - SUBLANES:blk]
            u_ref[i, SUBLANES:SUBLANES + blk, :] = raw
            cout[i] = raw[blk - hist:blk]
            acc = u_ref[i, lo:lo + blk, :] * w_ref[0:1, :]
            for j in range(1, CONV_W):
                acc = acc + u_ref[i, lo + j:lo + j + blk, :] * w_ref[j:j + 1, :]
            parts.append(acc)
        return _silu(_cat(parts))

    q = conv(uq_ref, hq_ref, cq_in, cq_out, q_ref, wq_ref)
    k = conv(uk_ref, hk_ref, ck_in, ck_out, k_ref, wk_ref) * (DK_C ** -0.5)
    qb = q.astype(BF16)
    vb = _rd(v_ref).astype(BF16)

    lane = lax.broadcasted_iota(jnp.int32, (rows, LANES), 1)
    tl = tail_ref[...] + bias_ref[...]
    li = jnp.sum(jnp.where(lane == h, tl, 0.0), axis=-1, keepdims=True)
    lf = jnp.sum(jnp.where(lane == H_C + h, _log_sigmoid(tl), 0.0), axis=-1, keepdims=True)
    b = _prefix(jnp.broadcast_to(lf, (rows, LANES)), blk)[:, :1]

    get, put = _states(per_seq, nblk, (c_in, n_in, m_in), (c_out, n_out, m_out))
    states = [get(i) for i in range(nblk)] if per_seq else [get(0)]
    m_prev = _cat([jnp.broadcast_to(st[2][:, :1], (blk, 1)) for st in states])

    mask, _ = _block_mask(rows, blk)
    d = jnp.where(mask, b + _row_bcast(li - b), -jnp.inf)
    m_inter = b + m_prev
    m_t = jnp.maximum(m_inter, jnp.max(d, axis=-1, keepdims=True))
    w_inter = jnp.exp(m_inter - m_t)
    sc = _dot_nt(qb, k.astype(BF16)) * jnp.exp(d - m_t)
    num = _dot(sc.astype(BF16), vb)
    den = jnp.sum(sc, axis=-1, keepdims=True)
    m_new = _block_last(m_t, blk)
    b_last = _block_last(b, blk)
    kw = k * jnp.exp(b_last - b + li - m_new)
    w_c = jnp.exp(b_last + m_prev - m_new)
    nums, dens = [], []
    for i in range(nblk):
        r = slice(i * blk, (i + 1) * blk)
        cst, nrow, _ = states[i]
        nums.append(num[r] + w_inter[r] * _dot(qb[r], cst.astype(BF16)))
        dens.append(den[r] + w_inter[r] * jnp.sum(q[r] * nrow, axis=-1, keepdims=True))
        wc = w_c[i * blk:i * blk + 1]
        put(i, (wc * cst + _dot_tn(kw[r].astype(BF16), vb[r]),
                wc * nrow + jnp.sum(kw[r], axis=0, keepdims=True),
                jnp.broadcast_to(m_new[i * blk:i * blk + 1], (1, LANES))))
    hh = _cat(nums) / jnp.maximum(jnp.abs(_cat(dens)), jnp.exp(-m_t))
    y_ref[...] = _head_out(hh, gn_ref[...], jax.nn.sigmoid(_rd(og_ref)))


def _geometry(n_seq, seq_len, row0, kind):
    slab, chunk = PROMPT_SLAB[kind]
    if seq_len >= slab:
        blk, rows, per_seq, nseq_blk = chunk, slab, False, 1
        grid_seq, grid_slab = n_seq, seq_len // slab
    else:
        blk, nseq_blk, per_seq = seq_len, SEQ_PER_STEP, True
        rows = nseq_blk * blk
        grid_seq, grid_slab = n_seq // nseq_blk, 1
    base = row0 // rows
    return dict(blk=blk, rows=rows, nblk=rows // blk, per_seq=per_seq, nseq_blk=nseq_blk,
                grid_seq=grid_seq, grid_slab=grid_slab, rowblk=lambda b, s: base + b * grid_slab + s)


def _zspec(geo, width, col):
    rb = geo["rowblk"]
    tiles = width // LANES
    return pl.BlockSpec((None if tiles == 1 else tiles, geo["rows"], LANES), lambda b, h, s: (col + h, rb(b, s), 0))


def _rd(ref, rows=slice(None)):
    if len(ref.shape) == 2:
        return ref[rows, :]
    return jnp.concatenate([ref[c, rows, :] for c in range(ref.shape[0])], axis=1)


def _const_spec(shape):
    return pl.BlockSpec(shape, lambda b, h, s: (0,) * len(shape))


def _head_spec(width, col=0):
    return pl.BlockSpec((1, width), lambda b, h, s: (0, col + h))


def _state_spec(geo, shape):
    return pl.BlockSpec((geo["nseq_blk"], 1) + shape, lambda b, h, s: (b, h, 0, 0))


def _mixer_call(kern, geo, heads, vd, y_col, y_prev, n_tok, in_specs, args, extra_specs, extra_shapes,
                scratch, name):
    if y_prev is not None:
        in_specs = in_specs + [pl.BlockSpec(memory_space=pl.ANY)]
        args = args + [y_prev]
    rb = geo["rowblk"]
    return pl.pallas_call(
        kern,
        grid=(geo["grid_seq"], heads, geo["grid_slab"]),
        in_specs=in_specs,
        out_specs=[pl.BlockSpec((geo["rows"], vd), lambda b, h, s: (rb(b, s), y_col // vd + h))] + extra_specs,
        out_shape=[jax.ShapeDtypeStruct((n_tok, MIX), BF16)] + extra_shapes,
        input_output_aliases={} if y_prev is None else {len(args) - 1: 0},
        scratch_shapes=scratch,
        compiler_params=_cparams(("parallel", "parallel", "arbitrary")),
        name=name + ("_sample" if geo["per_seq"] else "_prompt"),
    )(*args)


def _decay_mixer(kind, z, geo, state, extra, y_prev, y_col):
    heads, kd, vd = {"ret": (H_A, DK_A, DV_A), "hgrn": (H_B, DK_B, DV_B), "gla": (H_D, DK_D, DV_D)}[kind]
    common = dict(blk=geo["blk"], nblk=geo["nblk"], per_seq=geo["per_seq"])
    kb, vb = LANES, vd
    if kind == "ret":
        cos, sin = extra["rope"]
        tab = pl.BlockSpec((geo["blk"] if geo["per_seq"] else geo["rows"], DK_A),
                           lambda b, h, s: (0 if geo["per_seq"] else s, 0))
        in_specs = [_zspec(geo, kb, 0), _zspec(geo, kb, H_A), _zspec(geo, vb, (2 * H_A * DK_A) // vb),
                    _zspec(geo, vb, (2 * H_A * DK_A) // vb + H_A), tab, tab, _head_spec(vd)]
        args = [z, z, z, z, cos, sin, extra["gn"].reshape(1, -1)]
        kern = functools.partial(_ret_kernel, **common)
    elif kind == "hgrn":
        off = (2 * H_A * DK_A + 2 * H_A * DV_A) // LANES
        in_specs = [_zspec(geo, kb, off), _zspec(geo, kb, off + H_B), _zspec(geo, kb, off + 2 * H_B),
                    _zspec(geo, kb, off + 3 * H_B),
                    pl.BlockSpec((DEPTH + 1, DK_B), lambda b, h, s: (0, h)), _head_spec(vd)]
        args = [z, z, z, z, extra["lb_logits"], extra["gn"].reshape(1, -1)]
        kern = functools.partial(_hgrn_kernel, layer=extra["layer"], **common)
    else:
        off = (2 * H_C * DK_C + 2 * H_C * DV_C) // LANES
        voff = (2 * H_C * DK_C + 2 * H_C * DV_C + 2 * H_D * DK_D) // vb
        in_specs = [_zspec(geo, kb, off), _zspec(geo, kb, off + H_D), _zspec(geo, vb, voff),
                    _zspec(geo, vb, voff + H_D),
                    pl.BlockSpec((None, geo["rows"], CD_TAIL), lambda b, h, s: (CD_MAIN // CD_TAIL, geo["rowblk"](b, s), 0)),
                    pl.BlockSpec((CD_TAIL, DK_D), lambda b, h, s: (0, h)),
                    _head_spec(DK_D), _head_spec(vd)]
        args = [z, z, z, z, z, extra["w2"], extra["ab"].reshape(1, -1), extra["gn"].reshape(1, -1)]
        kern = functools.partial(_gla_kernel, **common)
    if kind != "ret":
        tile = min(geo["rows"], PAIR_TILE)
        mk = _pair_masks(tile, min(geo["blk"], tile))
        in_specs.append(_const_spec(mk.shape))
        args.append(mk)
    in_specs.append(_state_spec(geo, (kd, vd)))
    args.append(state)
    return _mixer_call(kern, geo, heads, vd, y_col, y_prev, z.shape[1], in_specs, args,
                       [_state_spec(geo, (kd, vd))], [jax.ShapeDtypeStruct(state.shape, F32)], [], kind)


def _mlstm_mixer(z, geo, c0, n0, m0, conv0, w, y_prev, y_col):
    n_seq = c0.shape[0]
    blk, nb = geo["blk"], geo["nseq_blk"]
    bias = jnp.zeros((1, LANES), F32).at[0, :H_C].set(w["i_bias"]).at[0, H_C:2 * H_C].set(w["f_bias"])
    kk = (H_C * DK_C) // DK_C
    kv = (2 * H_C * DK_C) // DV_C
    conv_spec = lambda col: pl.BlockSpec((nb, CONV_W - 1, DK_C), lambda b, h, s: (b, 0, col + h))
    in_specs = [
        _zspec(geo, DK_C, 0), _zspec(geo, DK_C, kk), _zspec(geo, DV_C, kv), _zspec(geo, DV_C, kv + H_C),
        pl.BlockSpec((None, geo["rows"], CD_TAIL), lambda b, h, s: (CD_MAIN // CD_TAIL, geo["rowblk"](b, s), 0)),
        pl.BlockSpec((CONV_W, DK_C), lambda b, h, s: (0, h)),
        pl.BlockSpec((CONV_W, DK_C), lambda b, h, s: (0, H_C + h)),
        _const_spec((1, LANES)), _head_spec(DV_C),
        conv_spec(0), conv_spec(H_C),
        _state_spec(geo, (DK_C, DV_C)), _state_spec(geo, (1, DK_C)), _state_spec(geo, (1, LANES)),
    ]
    n4 = n0.reshape(n_seq, H_C, 1, DK_C)
    m4 = jnp.broadcast_to(m0[:, :, None, None], (n_seq, H_C, 1, LANES))
    args = [z, z, z, z, z, w["conv_w"], w["conv_w"], bias, w["gn_c"].reshape(1, -1),
            conv0, conv0, c0, n4, m4]
    conv_shape = jax.ShapeDtypeStruct((n_seq, CONV_W - 1, H_C * DK_C), F32)
    y, c_new, n_new, m_new, cq, ck = _mixer_call(
        functools.partial(_mlstm_kernel, blk=blk, nblk=geo["nblk"], per_seq=geo["per_seq"]),
        geo, H_C, DV_C, y_col, y_prev, z.shape[1], in_specs, args,
        [_state_spec(geo, (DK_C, DV_C)), _state_spec(geo, (1, DK_C)), _state_spec(geo, (1, LANES)),
         conv_spec(0), conv_spec(0)],
        [jax.ShapeDtypeStruct(c0.shape, F32), jax.ShapeDtypeStruct(n4.shape, F32),
         jax.ShapeDtypeStruct(m4.shape, F32), conv_shape, conv_shape],
        [pltpu.VMEM((geo["nblk"], SUBLANES + blk, DK_C), F32), pltpu.VMEM((geo["nblk"], SUBLANES + blk, DK_C), F32),
         pltpu.VMEM((SUBLANES, DK_C), F32), pltpu.VMEM((SUBLANES, DK_C), F32)],
        "mlstm")
    return (y, c_new, n_new.reshape(n_seq, H_C, DK_C), m_new[:, :, 0, 0],
            jnp.concatenate([cq, ck], axis=-1))


def _rope_tables(pos0, n):
    half = DK_A // 2
    inv = ROPE_BASE ** (-jnp.arange(half, dtype=F32) / half)
    ang = (jnp.arange(n, dtype=F32) + float(pos0))[:, None] * inv[None, :]
    cos, sin = jnp.cos(ang), jnp.sin(ang)
    return jnp.concatenate([cos, cos], axis=1), jnp.concatenate([-sin, sin], axis=1)


def kernel(x_prompt, x_sample, state_ret, state_hgrn, state_mlstm_c, state_mlstm_n, state_mlstm_m,
           state_mlstm_conv, state_gla, norm_gain, ffn_w1, ffn_w3, ffn_w2, ab_w_in, ab_w_out, ab_norm_a,
           ab_norm_b, lb_logits, cd_w_in, cd_w_out, cd_conv_w, cd_i_bias, cd_f_bias, cd_norm_c,
           cd_alpha_w2, cd_alpha_b, cd_norm_d, final_norm):
    bp, tp, _ = x_prompt.shape
    bs, ts, _ = x_sample.shape
    n_p, n_s = bp * tp, bs * ts
    x = jnp.concatenate([x_prompt.reshape(n_p, D_MODEL), x_sample.reshape(n_s, D_MODEL)], axis=0)
    w1, w3, w2 = ffn_w1.astype(BF16), ffn_w3.astype(BF16), ffn_w2.astype(BF16)
    groups = (("p", bp, {k: _geometry(bp, tp, 0, k) for k in PROMPT_SLAB}),
              ("s", bs, {k: _geometry(bs, ts, n_p, k) for k in PROMPT_SLAB}))
    rope = {"p": _rope_tables(0, tp), "s": _rope_tables(PAST_LEN, ts)}
    zeros = lambda *shape: jnp.zeros(shape, F32)
    outs = {}

    def keep(name, tag, val):
        outs.setdefault(name + "_" + tag, []).append(val)

    for l in range(DEPTH):
        j = l // 2
        x, hn = _ffn(x, norm_gain[l, 0], w1, w3, w2, l, 0, post="norm_out", post_gain=norm_gain[l, 1])
        y = None
        if l % 2 == 0:
            z = _inproj(hn, ab_w_in[j].astype(BF16))
            for tag, nseq, geo in groups:
                s_ret = zeros(nseq, H_A, DK_A, DV_A) if tag == "p" else state_ret[j]
                s_hg = zeros(nseq, H_B, DK_B, DV_B) if tag == "p" else state_hgrn[j]
                y, r_new = _decay_mixer("ret", z, geo["ret"], s_ret, dict(rope=rope[tag], gn=ab_norm_a[j]), y, 0)
                y, g_new = _decay_mixer("hgrn", z, geo["hgrn"], s_hg,
                                        dict(lb_logits=lb_logits, gn=ab_norm_b[j], layer=l), y, H_A * DV_A)
                keep("ret", tag, r_new)
                keep("hgrn", tag, g_new)
            x = _outproj(x, y, ab_w_out[j].astype(BF16))
        else:
            gates = 2 * H_C
            main = 2 * H_C * DK_C + 2 * H_C * DV_C
            rest = CD_MAIN - main
            n_pad = -(-(CD_MAIN + CD_TAIL) // (N_SPLIT * LANES)) * (N_SPLIT * LANES)
            lane_pad = -cd_w_in.shape[2] % LANES
            w_perm = _permute_cols(jnp.pad(cd_w_in.astype(BF16), ((0, 0), (0, 0), (0, lane_pad))), j, ((0, main), (main + gates, main + gates + rest),
                                                (main, main + gates), (main + gates + rest, cd_w_in.shape[2])), n_pad)
            z = _inproj(hn, w_perm)
            w2p = jnp.zeros((CD_TAIL, H_D * DK_D), F32).at[gates:gates + GLA_RANK].set(cd_alpha_w2[j])
            wts = dict(conv_w=cd_conv_w[j], i_bias=cd_i_bias[j], f_bias=cd_f_bias[j], gn_c=cd_norm_c[j])
            for tag, nseq, geo in groups:
                if tag == "p":
                    c0, n0, m0 = zeros(nseq, H_C, DK_C, DV_C), zeros(nseq, H_C, DK_C), zeros(nseq, H_C)
                    cv0, s_gl = zeros(nseq, CONV_W - 1, 2 * H_C * DK_C), zeros(nseq, H_D, DK_D, DV_D)
                else:
                    c0, n0, m0 = state_mlstm_c[j], state_mlstm_n[j], state_mlstm_m[j]
                    cv0, s_gl = state_mlstm_conv[j], state_gla[j]
                y, c_new, n_new, m_new, cv_new = _mlstm_mixer(z, geo["mlstm"], c0, n0, m0, cv0, wts, y, 0)
                y, gl_new = _decay_mixer("gla", z, geo["gla"], s_gl,
                                         dict(w2=w2p, ab=cd_alpha_b[j], gn=cd_norm_d[j]), y, H_C * DV_C)
                for name, val in (("c", c_new), ("n", n_new), ("m", m_new), ("conv", cv_new), ("gla", gl_new)):
                    keep(name, tag, val)
            x = _outproj(x, y, cd_w_out[j].astype(BF16))
        last = l == DEPTH - 1
        x = _ffn(x, norm_gain[l, 2], w1, w3, w2, l, 1, post="final" if last else None,
                 post_gain=final_norm if last else None)

    y_p = x[:n_p].reshape(bp, tp, D_MODEL)
    y_s = x[n_p:].reshape(bs, ts, D_MODEL)
    st = lambda name: jnp.stack(outs[name])
    return (y_p, y_s, st("ret_p"), st("ret_s"), st("hgrn_p"), st("hgrn_s"), st("c_p"), st("c_s"),
            st("n_p"), st("n_s"), st("m_p"), st("m_s"), st("conv_p"), st("conv_s"), st("gla_p"), st("gla_s"))
```

```python
import functools

import numpy as np
import jax
import jax.numpy as jnp
from jax import lax
from jax.experimental import pallas as pl
from jax.experimental.pallas import tpu as pltpu

F32 = jnp.float32
BF16 = jnp.bfloat16

D_MODEL = 2048
DEPTH = 2
PAST_LEN = 16384
H_A, DK_A, DV_A = 4, 128, 256
ROPE_BASE = 10000.0
H_B, DK_B, DV_B = 8, 128, 128
H_C, DK_C, DV_C = 4, 256, 256
CONV_W = 4
H_D, DK_D, DV_D = 4, 128, 256
GLA_RANK = 16
GLA_TAU = 16.0
FF_DIM = 5632
EPS = 1e-6

CD_MAIN = 2 * H_C * DK_C + 2 * H_C * DV_C + 2 * H_D * DK_D + 2 * H_D * DV_D
CD_TAIL = 128
MIX = H_A * DV_A + H_B * DV_B

LANES = 128
SUBLANES = 8
VMEM_LIMIT = 56 * 1024 * 1024

TM = 768
TM_WIDE = 1024
TM_IN = 1152
TM_OUT = 512
TF = 512
N_SPLIT = 4
PROMPT_SLAB = {"ret": (1024, 512), "hgrn": (1024, 256), "mlstm": (512, 512), "gla": (1024, 256)}
SEQ_PER_STEP = 32
PAIR_TILE = 128
LOG2E = 1.4426950408889634


def _cparams(sem):
    return pltpu.CompilerParams(dimension_semantics=sem, vmem_limit_bytes=VMEM_LIMIT)


def _rms(x, g):
    return x * lax.rsqrt(jnp.mean(x * x, axis=-1, keepdims=True) + EPS) * g


def _silu(x):
    return x * jax.nn.sigmoid(x)


def _log_sigmoid(x):
    return jnp.minimum(x, 0.0) - jnp.log1p(jnp.exp(-jnp.abs(x)))


def _pieces3(x):
    hi = x.astype(BF16).astype(F32)
    r1 = x - hi
    mid = r1.astype(BF16).astype(F32)
    lo = (r1 - mid).astype(BF16).astype(F32)
    return hi, mid, lo


def _dot(a, b):
    return jnp.dot(a, b, preferred_element_type=F32)


def _dot_nt(a, b):
    return lax.dot_general(a, b, (((1,), (1,)), ((), ())), preferred_element_type=F32)


def _dot_tn(a, b):
    return lax.dot_general(a, b, (((0,), (0,)), ((), ())), preferred_element_type=F32)


def _cat(parts, axis=0):
    return parts[0] if len(parts) == 1 else jnp.concatenate(parts, axis=axis)


def _prefix(g, blk):
    rows, n = g.shape
    row = lax.broadcasted_iota(jnp.int32, (SUBLANES, n), 0)
    outs = []
    off = None
    for r0 in range(0, rows, SUBLANES):
        x = g[r0:r0 + SUBLANES]
        for sh in (1, 2, 4):
            x = x + jnp.where(row >= sh, pltpu.roll(x, sh, 0), 0.0)
        if r0 % blk:
            x = x + off
        off = x[SUBLANES - 1:SUBLANES]
        outs.append(x)
    return _cat(outs)


def _block_last(x, blk):
    rows, n = x.shape
    return _cat([jnp.broadcast_to(x[r0 + blk - 1:r0 + blk], (blk, n)) for r0 in range(0, rows, blk)])


def _block_mask(rows, blk):
    tt = lax.broadcasted_iota(jnp.int32, (rows, rows), 0)
    ss = lax.broadcasted_iota(jnp.int32, (rows, rows), 1)
    mask = ss <= tt
    if blk < rows:
        sh = blk.bit_length() - 1
        mask = jnp.logical_and(mask, lax.shift_right_logical(tt, sh) == lax.shift_right_logical(ss, sh))
    return mask, tt - ss


def _row_bcast(col):
    c = col.shape[0]
    lane = lax.broadcasted_iota(jnp.int32, (c, LANES), 1)
    pieces = _pieces3(jnp.where(lane == 0, jnp.broadcast_to(col, (c, LANES)), 0.0))
    rm = jnp.concatenate(pieces, axis=1).astype(BF16)
    ones = jnp.ones((c, 3 * LANES), BF16)
    return _dot_nt(ones, rm)


def _levels(blk):
    out = []
    m = 1
    while m < blk:
        out.append(m)
        m *= 2
    return tuple(out)


def _pair_masks(rows, blk):
    t = np.arange(rows)[:, None]
    u = np.arange(rows)[None, :]
    masks = [t == u] + [((t % (2 * m)) >= m) & ((u % (2 * m)) < m) & (t // (2 * m) == u // (2 * m))
                        for m in _levels(blk)]
    return jnp.asarray(np.stack(masks).astype(np.float32))


def _level_exponent(b, g, m):
    rows, n = b.shape
    if m >= SUBLANES:
        mid = _cat([jnp.broadcast_to(b[r0 + m - 1:r0 + m], (2 * m, n)) for r0 in range(0, rows, 2 * m)])
        return -jnp.abs(b - mid)
    row = lax.broadcasted_iota(jnp.int32, (SUBLANES, n), 0)
    outs = []
    for r0 in range(0, rows, SUBLANES):
        x = b[r0:r0 + SUBLANES]
        if m == 1:
            outs.append(jnp.where(jnp.bitwise_and(row, 1) == 1, g[r0:r0 + SUBLANES], 0.0))
            continue
        if m == 2:
            mid = jnp.where(row < 4, jnp.broadcast_to(x[1:2], x.shape), jnp.broadcast_to(x[5:6], x.shape))
        else:
            mid = jnp.broadcast_to(x[3:4], x.shape)
        outs.append(-jnp.abs(x - mid))
    return _cat(outs)


def _state_update(kw, v, e_row, s):
    c, kd = kw.shape
    vd = v.shape[1]
    pad = 2 * SUBLANES
    lhs = jnp.concatenate(_pieces3(e_row) + (jnp.zeros((pad - 3, kd), F32), kw), axis=0)
    sel = (lax.broadcasted_iota(jnp.int32, (pad, LANES), 0) < 3).astype(F32)
    rhs = jnp.concatenate(
        [jnp.concatenate([jnp.zeros((pad, vd), F32), sel], axis=1),
         jnp.concatenate([v, jnp.zeros((c, LANES), F32)], axis=1)], axis=0)
    upd = _dot_tn(lhs.astype(BF16), rhs.astype(BF16))
    e_col = upd[:, vd:]
    if vd > LANES:
        e_col = jnp.concatenate([e_col] * (vd // LANES), axis=1)
    return e_col * s + upd[:, :vd]


def _head_out(o, gn, gate_act):
    return (o * lax.rsqrt(jnp.mean(o * o, axis=-1, keepdims=True) + EPS) * gn * gate_act).astype(BF16)


def _ffn_kernel(*refs, nf, post):
    x_ref, g_ref, w1_ref, w3_ref, w2_ref = refs[:5]
    pg_ref = refs[5] if post else None
    outs = refs[6 if post else 5:-1]
    xn_ref = refs[-1]
    acc_ref = outs[0]
    f = pl.program_id(1)

    @pl.when(f == 0)
    def _():
        xn_ref[...] = _rms(x_ref[...], g_ref[...]).astype(BF16)
        acc_ref[...] = jnp.zeros_like(acc_ref)

    xn = xn_ref[...]
    half = w1_ref.shape[1] // 2
    acts = []
    for c in range(2):
        cols = slice(c * half, (c + 1) * half)
        h1 = _dot(xn, w1_ref[:, cols])
        h3 = _dot(xn, w3_ref[:, cols])
        acts.append((_silu(h1) * h3).astype(BF16))
    acc_ref[...] += _dot(jnp.concatenate(acts, axis=1), w2_ref[...])

    @pl.when(f == nf - 1)
    def _():
        y = x_ref[...] + 0.5 * acc_ref[...]
        if post == "final":
            outs[0][...] = _rms(y, pg_ref[...])
        else:
            outs[0][...] = y
            if post == "norm_out":
                outs[1][...] = _rms(y, pg_ref[...]).astype(BF16)


def _ffn(x, gain, w1, w3, w2, l, idx, post=None, post_gain=None):
    m = x.shape[0]
    nf = FF_DIM // TF
    tm = TM if post else TM_WIDE
    in_specs = [
        pl.BlockSpec((tm, D_MODEL), lambda i, f: (i, 0)),
        pl.BlockSpec((1, D_MODEL), lambda i, f: (0, 0)),
        pl.BlockSpec((None, None, D_MODEL, TF), lambda i, f: (l, idx, 0, f)),
        pl.BlockSpec((None, None, D_MODEL, TF), lambda i, f: (l, idx, 0, f)),
        pl.BlockSpec((None, None, TF, D_MODEL), lambda i, f: (l, idx, f, 0)),
    ]
    args = [x, gain.reshape(1, D_MODEL), w1, w3, w2]
    if post:
        in_specs.append(pl.BlockSpec((1, D_MODEL), lambda i, f: (0, 0)))
        args.append(post_gain.reshape(1, D_MODEL))
    row_spec = pl.BlockSpec((tm, D_MODEL), lambda i, f: (i, 0))
    out_specs, out_shape = [row_spec], [jax.ShapeDtypeStruct((m, D_MODEL), F32)]
    if post == "norm_out":
        out_specs.append(row_spec)
        out_shape.append(jax.ShapeDtypeStruct((m, D_MODEL), BF16))
    res = pl.pallas_call(
        functools.partial(_ffn_kernel, nf=nf, post=post),
        grid=(m // tm, nf),
        in_specs=in_specs,
        out_specs=out_specs,
        out_shape=out_shape,
        scratch_shapes=[pltpu.VMEM((tm, D_MODEL), BF16)],
        compiler_params=_cparams(("parallel", "arbitrary")),
        name="ffn_" + post if post else "ffn",
    )(*args)
    return res if post == "norm_out" else res[0]


def _permute_cols_kernel(w_ref, o_ref, *, pieces):
    x = w_ref[...]
    parts = [x[:, a:b] for a, b in pieces]
    width = sum(b - a for a, b in pieces)
    parts.append(jnp.zeros((x.shape[0], o_ref.shape[1] - width), x.dtype))
    o_ref[...] = jnp.concatenate(parts, axis=1).astype(o_ref.dtype)


def _permute_cols(w, j, pieces, n_out, rows_per_step=256):
    k = w.shape[1]
    return pl.pallas_call(
        functools.partial(_permute_cols_kernel, pieces=pieces),
        grid=(k // rows_per_step,),
        in_specs=[pl.BlockSpec((None, rows_per_step, w.shape[2]), lambda i: (j, i, 0))],
        out_specs=pl.BlockSpec((rows_per_step, n_out), lambda i: (i, 0)),
        out_shape=jax.ShapeDtypeStruct((k, n_out), BF16),
        compiler_params=_cparams(("parallel",)),
        name="permute_cols",
    )(w)


def _inproj_kernel(h_ref, w_ref, z_ref):
    res = _dot(h_ref[...], w_ref[...])
    for c in range(z_ref.shape[0]):
        z_ref[c] = res[:, c * LANES:(c + 1) * LANES]


def _inproj(h, w):
    m, n = h.shape[0], w.shape[1]
    tn = n // N_SPLIT
    return pl.pallas_call(
        _inproj_kernel,
        grid=(N_SPLIT, m // TM_IN),
        in_specs=[
            pl.BlockSpec((TM_IN, D_MODEL), lambda j, i: (i, 0)),
            pl.BlockSpec((D_MODEL, tn), lambda j, i: (0, j)),
        ],
        out_specs=pl.BlockSpec((tn // LANES, TM_IN, LANES), lambda j, i: (j, i, 0)),
        out_shape=jax.ShapeDtypeStruct((n // LANES, m, LANES), F32),
        compiler_params=_cparams(("parallel", "parallel")),
        name="inproj",
    )(h, w)


def _outproj_kernel(x_ref, y_ref, w_ref, o_ref):
    o_ref[...] = x_ref[...] + _dot(y_ref[...], w_ref[...])


def _outproj(x, y, w):
    m, kdim = y.shape
    return pl.pallas_call(
        _outproj_kernel,
        grid=(m // TM_OUT,),
        in_specs=[
            pl.BlockSpec((TM_OUT, D_MODEL), lambda i: (i, 0)),
            pl.BlockSpec((TM_OUT, kdim), lambda i: (i, 0)),
            pl.BlockSpec((kdim, D_MODEL), lambda i: (0, 0)),
        ],
        out_specs=pl.BlockSpec((TM_OUT, D_MODEL), lambda i: (i, 0)),
        out_shape=jax.ShapeDtypeStruct((m, D_MODEL), F32),
        compiler_params=_cparams(("parallel",)),
        name="outproj",
    )(x, y, w)


def _states(per_seq, nblk, ins, outs):
    if per_seq:
        def get(i):
            return tuple(r[i, 0] for r in ins)

        def put(i, vals):
            for o, val in zip(outs, vals):
                o[i, 0] = val
        return get, put

    @pl.when(pl.program_id(2) == 0)
    def _():
        for o, r in zip(outs, ins):
            o[...] = r[...]

    carried = {"v": tuple(o[0, 0] for o in outs)}

    def get(i):
        return carried["v"]

    def put(i, vals):
        carried["v"] = tuple(vals)
        if i == nblk - 1:
            for o, val in zip(outs, vals):
                o[0, 0] = val
    return get, put


def _ret_kernel(q_ref, k_ref, v_ref, gt_ref, cos_ref, sin_ref, gn_ref, s_in, *rest, blk, nblk, per_seq):
    y_ref, s_out = rest[-2:]
    rows = blk * nblk
    hf = jnp.full((1, 1), pl.program_id(1), jnp.int32).astype(F32)
    lg = jnp.log1p(-jnp.exp2(-5.0 - hf))
    cos = _cat([cos_ref[...]] * (rows // cos_ref.shape[0]))
    sin = _cat([sin_ref[...]] * (rows // sin_ref.shape[0]))
    q = q_ref[...]
    k = k_ref[...]
    q = q * cos + pltpu.roll(q, DK_A // 2, 1) * sin
    k = (k * cos + pltpu.roll(k, DK_A // 2, 1) * sin) * (DK_A ** -0.5)
    vb = _rd(v_ref).astype(BF16)
    mask, dist = _block_mask(rows, blk)
    dec = jnp.where(mask, jnp.exp(jnp.maximum(dist, 0).astype(F32) * lg), 0.0)
    a = _dot_nt(q.astype(BF16), k.astype(BF16)) * dec
    o = _dot(a.astype(BF16), vb)
    tloc = jnp.bitwise_and(lax.broadcasted_iota(jnp.int32, (rows, 1), 0), blk - 1).astype(F32)
    qt = (q * jnp.exp((tloc + 1.0) * lg)).astype(BF16)
    kw = (k * jnp.exp((blk - 1.0 - tloc) * lg)).astype(BF16)
    e_blk = jnp.exp(blk * lg)
    get, put = _states(per_seq, nblk, (s_in,), (s_out,))
    parts = []
    for i in range(nblk):
        r = slice(i * blk, (i + 1) * blk)
        s, = get(i)
        parts.append(o[r] + _dot(qt[r], s.astype(BF16)))
        put(i, (e_blk * s + _dot_tn(kw[r], vb[r]),))
    y_ref[...] = _head_out(_cat(parts), gn_ref[...], _silu(_rd(gt_ref)))


def _key_decay_slab(q, k, v, g, mk_ref, blk, nblk, get, put):
    rows = q.shape[0]
    tile = min(rows, PAIR_TILE)
    tiles = [slice(i * tile, (i + 1) * tile) for i in range(rows // tile)]
    g2 = g * LOG2E
    b = _prefix(g2, blk)
    b_last = _block_last(b, blk)
    qt = (q * jnp.exp2(b)).astype(BF16)
    kw = k * jnp.exp2(b_last - b)
    e_last = jnp.exp2(b_last)
    vb = v.astype(BF16)
    qb, kb = q.astype(BF16), k.astype(BF16)
    a = {(i, i): mk_ref[0] * _dot_nt(qb[t], kb[t]) for i, t in enumerate(tiles)}
    for li, m in enumerate(_levels(blk)):
        w = jnp.exp2(_level_exponent(b, g2, m))
        ql, kl = (q * w).astype(BF16), (k * w).astype(BF16)
        if m < tile:
            for i, t in enumerate(tiles):
                a[i, i] = a[i, i] + mk_ref[li + 1] * _dot_nt(ql[t], kl[t])
        else:
            for r0 in range(0, rows, 2 * m):
                for i in range((r0 + m) // tile, (r0 + 2 * m) // tile):
                    for j in range(r0 // tile, (r0 + m) // tile):
                        a[i, j] = _dot_nt(ql[tiles[i]], kl[tiles[j]])
    o_tiles = []
    for i in range(len(tiles)):
        terms = [_dot(a[i, j].astype(BF16), vb[tiles[j]]) for j in range(len(tiles)) if (i, j) in a]
        o_tiles.append(functools.reduce(lambda x, y: x + y, terms))
    o = _cat(o_tiles)
    parts = []
    for i in range(nblk):
        r = slice(i * blk, (i + 1) * blk)
        s, = get(i)
        parts.append(o[r] + _dot(qt[r], s.astype(BF16)))
        put(i, (_state_update(kw[r], v[r], e_last[i * blk:i * blk + 1], s),))
    return _cat(parts)


def _hgrn_kernel(q_ref, f_ref, i_ref, gt_ref, lb_ref, gn_ref, *rest, blk, nblk, per_seq, layer):
    mk_ref, s_in = rest[:2]
    y_ref, s_out = rest[-2:]
    lbl = lb_ref[...]
    e = jnp.exp(lbl - jnp.max(lbl, axis=0, keepdims=True))
    lb = jnp.sum(e[0:layer + 1], axis=0, keepdims=True) / jnp.sum(e, axis=0, keepdims=True)
    get, put = _states(per_seq, nblk, (s_in,), (s_out,))
    k = (1.0 - lb) * (1.0 - jax.nn.sigmoid(f_ref[...]))
    o = _key_decay_slab(_silu(q_ref[...]), k, i_ref[...], jnp.log(1.0 - k), mk_ref, blk, nblk, get, put)
    y_ref[...] = _head_out(o, gn_ref[...], _silu(gt_ref[...]))


def _gla_kernel(q_ref, k_ref, v_ref, gt_ref, tail_ref, w2_ref, ab_ref, gn_ref, *rest, blk, nblk, per_seq):
    mk_ref, s_in = rest[:2]
    y_ref, s_out = rest[-2:]
    wh, wm, wl = (p.astype(BF16) for p in _pieces3(w2_ref[...]))
    get, put = _states(per_seq, nblk, (s_in,), (s_out,))
    th, tm_, tl = (p.astype(BF16) for p in _pieces3(tail_ref[...]))
    x = (_dot(th, wh) + (_dot(th, wm) + _dot(tm_, wh))
         + (_dot(th, wl) + _dot(tm_, wm) + _dot(tl, wh)))
    g = _log_sigmoid(x + ab_ref[...]) / GLA_TAU
    o = _key_decay_slab(q_ref[...] * (DK_D ** -0.5), k_ref[...], _rd(v_ref), g, mk_ref, blk, nblk, get, put)
    y_ref[...] = _head_out(o, gn_ref[...], _silu(_rd(gt_ref)))


def _mlstm_kernel(q_ref, k_ref, v_ref, og_ref, tail_ref, wq_ref, wk_ref, bias_ref, gn_ref,
                  cq_in, ck_in, c_in, n_in, m_in, *rest, blk, nblk, per_seq):
    y_ref, c_out, n_out, m_out, cq_out, ck_out, uq_ref, uk_ref, hq_ref, hk_ref = rest[-10:]
    assert per_seq or nblk == 1
    rows = blk * nblk
    h = pl.program_id(1)
    hist = CONV_W - 1
    lo = SUBLANES - hist

    if not per_seq:
        @pl.when(pl.program_id(2) == 0)
        def _():
            for h_ref, cin in ((hq_ref, cq_in), (hk_ref, ck_in)):
                h_ref[...] = jnp.zeros_like(h_ref)
                h_ref[lo:SUBLANES, :] = cin[0]

    def conv(u_ref, h_ref, cin, cout, raw_ref, w_ref):
        parts = []
        for i in range(nblk):
            raw = _rd(raw_ref, slice(i * blk, (i + 1) * blk))
            if per_seq:
                u_ref[i, lo:SUBLANES, :] = cin[i]
            else:
                u_ref[i, 0:SUBLANES, :] = h_ref[...]
                h_ref[...] = raw[blk - SUBLANES:blk]
            u_ref[i, SUBLANES:SUBLANES + blk, :] = raw
            cout[i] = raw[blk - hist:blk]
            acc = u_ref[i, lo:lo + blk, :] * w_ref[0:1, :]
            for j in range(1, CONV_W):
                acc = acc + u_ref[i, lo + j:lo + j + blk, :] * w_ref[j:j + 1, :]
            parts.append(acc)
        return _silu(_cat(parts))

    q = conv(uq_ref, hq_ref, cq_in, cq_out, q_ref, wq_ref)
    k = conv(uk_ref, hk_ref, ck_in, ck_out, k_ref, wk_ref) * (DK_C ** -0.5)
    qb = q.astype(BF16)
    vb = _rd(v_ref).astype(BF16)

    lane = lax.broadcasted_iota(jnp.int32, (rows, LANES), 1)
    tl = tail_ref[...] + bias_ref[...]
    li = jnp.sum(jnp.where(lane == h, tl, 0.0), axis=-1, keepdims=True)
    lf = jnp.sum(jnp.where(lane == H_C + h, _log_sigmoid(tl), 0.0), axis=-1, keepdims=True)
    b = _prefix(jnp.broadcast_to(lf, (rows, LANES)), blk)[:, :1]

    get, put = _states(per_seq, nblk, (c_in, n_in, m_in), (c_out, n_out, m_out))
    states = [get(i) for i in range(nblk)] if per_seq else [get(0)]
    m_prev = _cat([jnp.broadcast_to(st[2][:, :1], (blk, 1)) for st in states])

    mask, _ = _block_mask(rows, blk)
    d = jnp.where(mask, b + _row_bcast(li - b), -jnp.inf)
    m_inter = b + m_prev
    m_t = jnp.maximum(m_inter, jnp.max(d, axis=-1, keepdims=True))
    w_inter = jnp.exp(m_inter - m_t)
    sc = _dot_nt(qb, k.astype(BF16)) * jnp.exp(d - m_t)
    num = _dot(sc.astype(BF16), vb)
    den = jnp.sum(sc, axis=-1, keepdims=True)
    m_new = _block_last(m_t, blk)
    b_last = _block_last(b, blk)
    kw = k * jnp.exp(b_last - b + li - m_new)
    w_c = jnp.exp(b_last + m_prev - m_new)
    nums, dens = [], []
    for i in range(nblk):
        r = slice(i * blk, (i + 1) * blk)
        cst, nrow, _ = states[i]
        nums.append(num[r] + w_inter[r] * _dot(qb[r], cst.astype(BF16)))
        dens.append(den[r] + w_inter[r] * jnp.sum(q[r] * nrow, axis=-1, keepdims=True))
        wc = w_c[i * blk:i * blk + 1]
        put(i, (wc * cst + _dot_tn(kw[r].astype(BF16), vb[r]),
                wc * nrow + jnp.sum(kw[r], axis=0, keepdims=True),
                jnp.broadcast_to(m_new[i * blk:i * blk + 1], (1, LANES))))
    hh = _cat(nums) / jnp.maximum(jnp.abs(_cat(dens)), jnp.exp(-m_t))
    y_ref[...] = _head_out(hh, gn_ref[...], jax.nn.sigmoid(_rd(og_ref)))


def _geometry(n_seq, seq_len, row0, kind):
    slab, chunk = PROMPT_SLAB[kind]
    if seq_len >= slab:
        blk, rows, per_seq, nseq_blk = chunk, slab, False, 1
        grid_seq, grid_slab = n_seq, seq_len // slab
    else:
        blk, nseq_blk, per_seq = seq_len, SEQ_PER_STEP, True
        rows = nseq_blk * blk
        grid_seq, grid_slab = n_seq // nseq_blk, 1
    base = row0 // rows
    return dict(blk=blk, rows=rows, nblk=rows // blk, per_seq=per_seq, nseq_blk=nseq_blk,
                grid_seq=grid_seq, grid_slab=grid_slab, rowblk=lambda b, s: base + b * grid_slab + s)


def _zspec(geo, width, col):
    rb = geo["rowblk"]
    tiles = width // LANES
    return pl.BlockSpec((None if tiles == 1 else tiles, geo["rows"], LANES), lambda b, h, s: (col + h, rb(b, s), 0))


def _rd(ref, rows=slice(None)):
    if len(ref.shape) == 2:
        return ref[rows, :]
    return jnp.concatenate([ref[c, rows, :] for c in range(ref.shape[0])], axis=1)


def _const_spec(shape):
    return pl.BlockSpec(shape, lambda b, h, s: (0,) * len(shape))


def _head_spec(width, col=0):
    return pl.BlockSpec((1, width), lambda b, h, s: (0, col + h))


def _state_spec(geo, shape):
    return pl.BlockSpec((geo["nseq_blk"], 1) + shape, lambda b, h, s: (b, h, 0, 0))


def _mixer_call(kern, geo, heads, vd, y_col, y_prev, n_tok, in_specs, args, extra_specs, extra_shapes,
                scratch, name):
    if y_prev is not None:
        in_specs = in_specs + [pl.BlockSpec(memory_space=pl.ANY)]
        args = args + [y_prev]
    rb = geo["rowblk"]
    return pl.pallas_call(
        kern,
        grid=(geo["grid_seq"], heads, geo["grid_slab"]),
        in_specs=in_specs,
        out_specs=[pl.BlockSpec((geo["rows"], vd), lambda b, h, s: (rb(b, s), y_col // vd + h))] + extra_specs,
        out_shape=[jax.ShapeDtypeStruct((n_tok, MIX), BF16)] + extra_shapes,
        input_output_aliases={} if y_prev is None else {len(args) - 1: 0},
        scratch_shapes=scratch,
        compiler_params=_cparams(("parallel", "parallel", "arbitrary")),
        name=name + ("_sample" if geo["per_seq"] else "_prompt"),
    )(*args)


def _decay_mixer(kind, z, geo, state, extra, y_prev, y_col):
    heads, kd, vd = {"ret": (H_A, DK_A, DV_A), "hgrn": (H_B, DK_B, DV_B), "gla": (H_D, DK_D, DV_D)}[kind]
    common = dict(blk=geo["blk"], nblk=geo["nblk"], per_seq=geo["per_seq"])
    kb, vb = LANES, vd
    if kind == "ret":
        cos, sin = extra["rope"]
        tab = pl.BlockSpec((geo["blk"] if geo["per_seq"] else geo["rows"], DK_A),
                           lambda b, h, s: (0 if geo["per_seq"] else s, 0))
        in_specs = [_zspec(geo, kb, 0), _zspec(geo, kb, H_A), _zspec(geo, vb, (2 * H_A * DK_A) // vb),
                    _zspec(geo, vb, (2 * H_A * DK_A) // vb + H_A), tab, tab, _head_spec(vd)]
        args = [z, z, z, z, cos, sin, extra["gn"].reshape(1, -1)]
        kern = functools.partial(_ret_kernel, **common)
    elif kind == "hgrn":
        off = (2 * H_A * DK_A + 2 * H_A * DV_A) // LANES
        in_specs = [_zspec(geo, kb, off), _zspec(geo, kb, off + H_B), _zspec(geo, kb, off + 2 * H_B),
                    _zspec(geo, kb, off + 3 * H_B),
                    pl.BlockSpec((DEPTH + 1, DK_B), lambda b, h, s: (0, h)), _head_spec(vd)]
        args = [z, z, z, z, extra["lb_logits"], extra["gn"].reshape(1, -1)]
        kern = functools.partial(_hgrn_kernel, layer=extra["layer"], **common)
    else:
        off = (2 * H_C * DK_C + 2 * H_C * DV_C) // LANES
        voff = (2 * H_C * DK_C + 2 * H_C * DV_C + 2 * H_D * DK_D) // vb
        in_specs = [_zspec(geo, kb, off), _zspec(geo, kb, off + H_D), _zspec(geo, vb, voff),
                    _zspec(geo, vb, voff + H_D),
                    pl.BlockSpec((None, geo["rows"], CD_TAIL), lambda b, h, s: (CD_MAIN // CD_TAIL, geo["rowblk"](b, s), 0)),
                    pl.BlockSpec((CD_TAIL, DK_D), lambda b, h, s: (0, h)),
                    _head_spec(DK_D), _head_spec(vd)]
        args = [z, z, z, z, z, extra["w2"], extra["ab"].reshape(1, -1), extra["gn"].reshape(1, -1)]
        kern = functools.partial(_gla_kernel, **common)
    if kind != "ret":
        tile = min(geo["rows"], PAIR_TILE)
        mk = _pair_masks(tile, min(geo["blk"], tile))
        in_specs.append(_const_spec(mk.shape))
        args.append(mk)
    in_specs.append(_state_spec(geo, (kd, vd)))
    args.append(state)
    return _mixer_call(kern, geo, heads, vd, y_col, y_prev, z.shape[1], in_specs, args,
                       [_state_spec(geo, (kd, vd))], [jax.ShapeDtypeStruct(state.shape, F32)], [], kind)


def _mlstm_mixer(z, geo, c0, n0, m0, conv0, w, y_prev, y_col):
    n_seq = c0.shape[0]
    blk, nb = geo["blk"], geo["nseq_blk"]
    bias = jnp.zeros((1, LANES), F32).at[0, :H_C].set(w["i_bias"]).at[0, H_C:2 * H_C].set(w["f_bias"])
    kk = (H_C * DK_C) // DK_C
    kv = (2 * H_C * DK_C) // DV_C
    conv_spec = lambda col: pl.BlockSpec((nb, CONV_W - 1, DK_C), lambda b, h, s: (b, 0, col + h))
    in_specs = [
        _zspec(geo, DK_C, 0), _zspec(geo, DK_C, kk), _zspec(geo, DV_C, kv), _zspec(geo, DV_C, kv + H_C),
        pl.BlockSpec((None, geo["rows"], CD_TAIL), lambda b, h, s: (CD_MAIN // CD_TAIL, geo["rowblk"](b, s), 0)),
        pl.BlockSpec((CONV_W, DK_C), lambda b, h, s: (0, h)),
        pl.BlockSpec((CONV_W, DK_C), lambda b, h, s: (0, H_C + h)),
        _const_spec((1, LANES)), _head_spec(DV_C),
        conv_spec(0), conv_spec(H_C),
        _state_spec(geo, (DK_C, DV_C)), _state_spec(geo, (1, DK_C)), _state_spec(geo, (1, LANES)),
    ]
    n4 = n0.reshape(n_seq, H_C, 1, DK_C)
    m4 = jnp.broadcast_to(m0[:, :, None, None], (n_seq, H_C, 1, LANES))
    args = [z, z, z, z, z, w["conv_w"], w["conv_w"], bias, w["gn_c"].reshape(1, -1),
            conv0, conv0, c0, n4, m4]
    conv_shape = jax.ShapeDtypeStruct((n_seq, CONV_W - 1, H_C * DK_C), F32)
    y, c_new, n_new, m_new, cq, ck = _mixer_call(
        functools.partial(_mlstm_kernel, blk=blk, nblk=geo["nblk"], per_seq=geo["per_seq"]),
        geo, H_C, DV_C, y_col, y_prev, z.shape[1], in_specs, args,
        [_state_spec(geo, (DK_C, DV_C)), _state_spec(geo, (1, DK_C)), _state_spec(geo, (1, LANES)),
         conv_spec(0), conv_spec(0)],
        [jax.ShapeDtypeStruct(c0.shape, F32), jax.ShapeDtypeStruct(n4.shape, F32),
         jax.ShapeDtypeStruct(m4.shape, F32), conv_shape, conv_shape],
        [pltpu.VMEM((geo["nblk"], SUBLANES + blk, DK_C), F32), pltpu.VMEM((geo["nblk"], SUBLANES + blk, DK_C), F32),
         pltpu.VMEM((SUBLANES, DK_C), F32), pltpu.VMEM((SUBLANES, DK_C), F32)],
        "mlstm")
    return (y, c_new, n_new.reshape(n_seq, H_C, DK_C), m_new[:, :, 0, 0],
            jnp.concatenate([cq, ck], axis=-1))


def _rope_tables(pos0, n):
    half = DK_A // 2
    inv = ROPE_BASE ** (-jnp.arange(half, dtype=F32) / half)
    ang = (jnp.arange(n, dtype=F32) + float(pos0))[:, None] * inv[None, :]
    cos, sin = jnp.cos(ang), jnp.sin(ang)
    return jnp.concatenate([cos, cos], axis=1), jnp.concatenate([-sin, sin], axis=1)


def kernel(x_prompt, x_sample, state_ret, state_hgrn, state_mlstm_c, state_mlstm_n, state_mlstm_m,
           state_mlstm_conv, state_gla, norm_gain, ffn_w1, ffn_w3, ffn_w2, ab_w_in, ab_w_out, ab_norm_a,
           ab_norm_b, lb_logits, cd_w_in, cd_w_out, cd_conv_w, cd_i_bias, cd_f_bias, cd_norm_c,
           cd_alpha_w2, cd_alpha_b, cd_norm_d, final_norm):
    bp, tp, _ = x_prompt.shape
    bs, ts, _ = x_sample.shape
    n_p, n_s = bp * tp, bs * ts
    x = jnp.concatenate([x_prompt.reshape(n_p, D_MODEL), x_sample.reshape(n_s, D_MODEL)], axis=0)
    w1, w3, w2 = ffn_w1.astype(BF16), ffn_w3.astype(BF16), ffn_w2.astype(BF16)
    groups = (("p", bp, {k: _geometry(bp, tp, 0, k) for k in PROMPT_SLAB}),
              ("s", bs, {k: _geometry(bs, ts, n_p, k) for k in PROMPT_SLAB}))
    rope = {"p": _rope_tables(0, tp), "s": _rope_tables(PAST_LEN, ts)}
    zeros = lambda *shape: jnp.zeros(shape, F32)
    outs = {}

    def keep(name, tag, val):
        outs.setdefault(name + "_" + tag, []).append(val)

    for l in range(DEPTH):
        j = l // 2
        x, hn = _ffn(x, norm_gain[l, 0], w1, w3, w2, l, 0, post="norm_out", post_gain=norm_gain[l, 1])
        y = None
        if l % 2 == 0:
            z = _inproj(hn, ab_w_in[j].astype(BF16))
            for tag, nseq, geo in groups:
                s_ret = zeros(nseq, H_A, DK_A, DV_A) if tag == "p" else state_ret[j]
                s_hg = zeros(nseq, H_B, DK_B, DV_B) if tag == "p" else state_hgrn[j]
                y, r_new = _decay_mixer("ret", z, geo["ret"], s_ret, dict(rope=rope[tag], gn=ab_norm_a[j]), y, 0)
                y, g_new = _decay_mixer("hgrn", z, geo["hgrn"], s_hg,
                                        dict(lb_logits=lb_logits, gn=ab_norm_b[j], layer=l), y, H_A * DV_A)
                keep("ret", tag, r_new)
                keep("hgrn", tag, g_new)
            x = _outproj(x, y, ab_w_out[j].astype(BF16))
        else:
            gates = 2 * H_C
            main = 2 * H_C * DK_C + 2 * H_C * DV_C
            rest = CD_MAIN - main
            n_pad = -(-(CD_MAIN + CD_TAIL) // (N_SPLIT * LANES)) * (N_SPLIT * LANES)
            lane_pad = -cd_w_in.shape[2] % LANES
            w_perm = _permute_cols(jnp.pad(cd_w_in.astype(BF16), ((0, 0), (0, 0), (0, lane_pad))), j, ((0, main), (main + gates, main + gates + rest),
                                                (main, main + gates), (main + gates + rest, cd_w_in.shape[2])), n_pad)
            z = _inproj(hn, w_perm)
            w2p = jnp.zeros((CD_TAIL, H_D * DK_D), F32).at[gates:gates + GLA_RANK].set(cd_alpha_w2[j])
            wts = dict(conv_w=cd_conv_w[j], i_bias=cd_i_bias[j], f_bias=cd_f_bias[j], gn_c=cd_norm_c[j])
            for tag, nseq, geo in groups:
                if tag == "p":
                    c0, n0, m0 = zeros(nseq, H_C, DK_C, DV_C), zeros(nseq, H_C, DK_C), zeros(nseq, H_C)
                    cv0, s_gl = zeros(nseq, CONV_W - 1, 2 * H_C * DK_C), zeros(nseq, H_D, DK_D, DV_D)
                else:
                    c0, n0, m0 = state_mlstm_c[j], state_mlstm_n[j], state_mlstm_m[j]
                    cv0, s_gl = state_mlstm_conv[j], state_gla[j]
                y, c_new, n_new, m_new, cv_new = _mlstm_mixer(z, geo["mlstm"], c0, n0, m0, cv0, wts, y, 0)
                y, gl_new = _decay_mixer("gla", z, geo["gla"], s_gl,
                                         dict(w2=w2p, ab=cd_alpha_b[j], gn=cd_norm_d[j]), y, H_C * DV_C)
                for name, val in (("c", c_new), ("n", n_new), ("m", m_new), ("conv", cv_new), ("gla", gl_new)):
                    keep(name, tag, val)
            x = _outproj(x, y, cd_w_out[j].astype(BF16))
        last = l == DEPTH - 1
        x = _ffn(x, norm_gain[l, 2], w1, w3, w2, l, 1, post="final" if last else None,
                 post_gain=final_norm if last else None)

    y_p = x[:n_p].reshape(bp, tp, D_MODEL)
    y_s = x[n_p:].reshape(bs, ts, D_MODEL)
    st = lambda name: jnp.stack(outs[name])
    return (y_p, y_s, st("ret_p"), st("ret_s"), st("hgrn_p"), st("hgrn_s"), st("c_p"), st("c_s"),
            st("n_p"), st("n_s"), st("m_p"), st("m_s"), st("conv_p"), st("conv_s"), st("gla_p"), st("gla_s"))
```
